```python
import jax, jax.numpy as jnp
from jax import lax
import numpy as np

D_MODEL = 2048
BATCH = 4
SEQ = 4096
DEPTH = 1

PLE_DIM = 256
RW_HEADS = 16
RW_HEAD_DIM = 64
RW_WIDTH = RW_HEADS * RW_HEAD_DIM
DECAY_LORA = 64
AAA_LORA = 64
FOX_HEADS = 16
FOX_HEAD_DIM = 64
FOX_WIDTH = FOX_HEADS * FOX_HEAD_DIM
Q_BLOCK = 128
NORM_EPS = 1e-6
GN_EPS = 64e-5

RW_COLS = 4 * RW_WIDTH + DECAY_LORA + AAA_LORA
FOX_COLS = 4 * FOX_WIDTH + FOX_HEADS
GATE_COLS = 2 * D_MODEL
N_IN = RW_COLS + FOX_COLS + GATE_COLS

kernel_name = 'hybrid_rwkv7_fox_gated_merge'


def _rmsnorm(x, g):
    xf = x.astype(jnp.float32)
    y = xf * lax.rsqrt(jnp.mean(xf * xf, axis=-1, keepdims=True) + NORM_EPS)
    return (y * g.astype(jnp.float32)).astype(x.dtype)


def _token_shift(z, mu):
    z_prev = jnp.pad(z, ((0, 0), (1, 0), (0, 0)))[:, :-1]
    return z + (z_prev - z) * mu


def _rwkv7_scan(r, w, k, v, kk, a):
    def step(S, inp):
        r_t, w_t, k_t, v_t, kk_t, a_t = inp
        s_kk = jnp.einsum('bhvk,bhk->bhv', S, kk_t)
        S = (S * w_t[:, :, None, :]
             - s_kk[..., None] * (kk_t * a_t)[:, :, None, :]
             + v_t[..., None] * k_t[:, :, None, :])
        y_t = jnp.einsum('bhvk,bhk->bhv', S, r_t)
        return S, y_t
    B, T, H, N = r.shape
    S0 = jnp.zeros((B, H, N, N), jnp.float32)
    xs = tuple(jnp.swapaxes(t, 0, 1) for t in (r, w, k, v, kk, a))
    _, ys = lax.scan(step, S0, xs)
    return jnp.swapaxes(ys, 0, 1)


def _rwkv7_branch(z, w0, w_lora_up, a0, a_lora_up, k_k, k_a, r_k, ln_g, ln_b):
    B, T, _ = z.shape
    f32 = jnp.float32
    C = RW_WIDTH
    r = z[..., 0:C]
    k = z[..., C:2 * C]
    v = z[..., 2 * C:3 * C]
    g = z[..., 3 * C:4 * C]
    wl = z[..., 4 * C:4 * C + DECAY_LORA]
    al = z[..., 4 * C + DECAY_LORA:]
    w_raw = (w0 + jnp.tanh(wl) @ w_lora_up).astype(f32)
    decay = jnp.exp(-jnp.exp(-jax.nn.softplus(-w_raw) - 0.5))
    a = jax.nn.sigmoid((a0 + al @ a_lora_up).astype(f32))
    hs = lambda t: t.astype(f32).reshape(B, T, RW_HEADS, RW_HEAD_DIM)
    r_h, k_h, v_h, a_h, w_h = hs(r), hs(k), hs(v), hs(a), hs(decay)
    kk = k_h * k_k.astype(f32).reshape(RW_HEADS, RW_HEAD_DIM)
    kk = kk / jnp.maximum(jnp.sqrt(jnp.sum(kk * kk, axis=-1, keepdims=True)), 1e-12)
    k_h = k_h * (1.0 + (a_h - 1.0) * k_a.astype(f32).reshape(RW_HEADS, RW_HEAD_DIM))
    y = _rwkv7_scan(r_h, w_h, k_h, v_h, kk, a_h)
    mu = jnp.mean(y, axis=-1, keepdims=True)
    var = jnp.mean(jnp.square(y - mu), axis=-1, keepdims=True)
    y = ((y - mu) * lax.rsqrt(var + GN_EPS)).reshape(B, T, C)
    y = y * ln_g.astype(f32) + ln_b.astype(f32)
    bonus = jnp.sum(r_h * k_h * r_k.astype(f32), axis=-1, keepdims=True) * v_h
    y = y + bonus.reshape(B, T, C)
    return (y * jax.nn.silu(g.astype(f32))).astype(z.dtype)


def _fox_branch(z, b_f):
    B, T, _ = z.shape
    f32 = jnp.float32
    C = FOX_WIDTH
    to_heads = lambda t: t.astype(f32).reshape(B, T, FOX_HEADS, FOX_HEAD_DIM).transpose(0, 2, 1, 3)
    qh = to_heads(z[..., 0:C])
    kh = to_heads(z[..., C:2 * C])
    vh = to_heads(z[..., 2 * C:3 * C])
    g = z[..., 3 * C:4 * C]
    fl = z[..., 4 * C:]
    log_f = jax.nn.log_sigmoid((fl + b_f).astype(f32))
    c = jnp.cumsum(log_f, axis=1).transpose(0, 2, 1)
    scale = FOX_HEAD_DIM ** -0.5
    outs = []
    for blk in range(T // Q_BLOCK):
        s = blk * Q_BLOCK
        e = s + Q_BLOCK
        logits = jnp.einsum('bhqd,bhkd->bhqk', qh[:, :, s:e], kh[:, :, :e]) * scale
        logits = logits + (c[:, :, s:e, None] - c[:, :, None, :e])
        causal = jnp.arange(s, e)[:, None] >= jnp.arange(e)[None, :]
        logits = jnp.where(causal, logits, -jnp.inf)
        probs = jax.nn.softmax(logits, axis=-1)
        outs.append(jnp.einsum('bhqk,bhkd->bhqd', probs, vh[:, :, :e]))
    o = jnp.concatenate(outs, axis=2).transpose(0, 2, 1, 3).reshape(B, T, C)
    return (o * jax.nn.silu(g.astype(f32))).astype(z.dtype)


def setup_inputs(seed: int = 0) -> dict:
    key = jax.random.key(seed)
    ks = jax.random.split(key, 24)
    f32 = jnp.float32
    nrm = lambda k, shape, s: jax.random.normal(k, shape, f32) * s
    C = RW_WIDTH
    return {
        'x': nrm(ks[0], (BATCH, SEQ, D_MODEL), 1.0),
        'p': nrm(ks[1], (DEPTH, BATCH, SEQ, PLE_DIM), 1.0),
        'norm_g': 1.0 + nrm(ks[2], (DEPTH, D_MODEL), 0.02),
        'w_in': nrm(ks[3], (DEPTH, D_MODEL, N_IN), D_MODEL ** -0.5),
        'rw_shift_mu': jax.random.uniform(ks[4], (DEPTH, RW_COLS), f32),
        'rw_w0': jax.random.uniform(ks[5], (DEPTH, C), f32, -6.0, -1.0),
        'rw_w_lora_up': nrm(ks[6], (DEPTH, DECAY_LORA, C), 0.1 * DECAY_LORA ** -0.5),
        'rw_a0': nrm(ks[7], (DEPTH, C), 0.1),
        'rw_a_lora_up': nrm(ks[8], (DEPTH, AAA_LORA, C), 0.1 * AAA_LORA ** -0.5),
        'rw_k_k': 0.85 + nrm(ks[9], (DEPTH, C), 0.02),
        'rw_k_a': 1.0 + nrm(ks[10], (DEPTH, C), 0.02),
        'rw_r_k': nrm(ks[11], (DEPTH, RW_HEADS, RW_HEAD_DIM), 0.1),
        'rw_ln_g': 1.0 + nrm(ks[12], (DEPTH, C), 0.02),
        'rw_ln_b': nrm(ks[13], (DEPTH, C), 0.02),
        'fox_b_f': jax.random.uniform(ks[14], (DEPTH, FOX_HEADS), f32, 1.0, 5.0),
        'w_up_rwkv': nrm(ks[15], (DEPTH, RW_WIDTH, D_MODEL), RW_WIDTH ** -0.5),
        'w_up_fox': nrm(ks[16], (DEPTH, FOX_WIDTH, D_MODEL), FOX_WIDTH ** -0.5),
        'w_out': nrm(ks[17], (DEPTH, D_MODEL, D_MODEL), D_MODEL ** -0.5),
        'ple_proj': nrm(ks[18], (DEPTH, PLE_DIM, D_MODEL), PLE_DIM ** -0.5),
        'ple_gate_w': nrm(ks[19], (DEPTH, D_MODEL, D_MODEL), D_MODEL ** -0.5),
        'ple_norm_g': 1.0 + nrm(ks[20], (DEPTH, D_MODEL), 0.02),
        'final_norm_g': 1.0 + nrm(ks[21], (D_MODEL,), 0.02),
    }


def reference(x, p, norm_g, w_in, rw_shift_mu, rw_w0, rw_w_lora_up, rw_a0, rw_a_lora_up,
              rw_k_k, rw_k_a, rw_r_k, rw_ln_g, rw_ln_b, fox_b_f, w_up_rwkv, w_up_fox, w_out,
              ple_proj, ple_gate_w, ple_norm_g, final_norm_g):
    for i in range(DEPTH):
        h = _rmsnorm(x, norm_g[i])
        z = h @ w_in[i]
        z_rw = _token_shift(z[..., :RW_COLS], rw_shift_mu[i])
        z_fox = z[..., RW_COLS:RW_COLS + FOX_COLS]
        z_gate = z[..., RW_COLS + FOX_COLS:]
        y_rw = _rwkv7_branch(z_rw, rw_w0[i], rw_w_lora_up[i], rw_a0[i], rw_a_lora_up[i],
                             rw_k_k[i], rw_k_a[i], rw_r_k[i], rw_ln_g[i], rw_ln_b[i])
        y_fox = _fox_branch(z_fox, fox_b_f[i])
        u_rw = y_rw @ w_up_rwkv[i]
        u_fox = y_fox @ w_up_fox[i]
        merged = (jax.nn.sigmoid(z_gate[..., :D_MODEL]) * u_rw
                  + jax.nn.sigmoid(z_gate[..., D_MODEL:]) * u_fox)
        x = x + merged @ w_out[i]
        ple = p[i] @ ple_proj[i]
        x = x + ple * jax.nn.sigmoid(_rmsnorm(x, ple_norm_g[i]) @ ple_gate_w[i])
    return _rmsnorm(x, final_norm_g)
```

```python
import functools
import math

import jax
import jax.numpy as jnp
from jax import lax
from jax.experimental import pallas as pl
from jax.experimental.pallas import tpu as pltpu

F32 = jnp.float32
BF16 = jnp.bfloat16

HEAD_DIM = 64
LANES = 128
NORM_EPS = 1e-6
GN_EPS = 64e-5
Q_SCALE = HEAD_DIM ** -0.5
EXP_NEG_HALF = math.exp(-0.5)
NEG_BIG = -1e30
VMEM_LIMIT = 56 * 1024 * 1024

RW_CHUNK = 64


def _dot(a, b):
    return jnp.dot(a, b, preferred_element_type=F32)


def _dot_nt(a, b):
    return lax.dot_general(a, b, (((1,), (1,)), ((), ())), preferred_element_type=F32)


def _dot_tn(a, b):
    return lax.dot_general(a, b, (((0,), (0,)), ((), ())), preferred_element_type=F32)


def _split_bf16(x, n):
    parts = []
    rem = x
    for i in range(n):
        p = rem.astype(BF16)
        parts.append(p)
        if i + 1 < n:
            rem = rem - p.astype(F32)
    return parts


def _sigmoid(x):
    return 1.0 / (1.0 + jnp.exp(-x))


def _rmsnorm_kernel(x_ref, g_ref, o_ref):
    x = x_ref[...]
    y = x * lax.rsqrt(jnp.mean(x * x, axis=-1, keepdims=True) + NORM_EPS)
    o_ref[...] = (y * g_ref[...]).astype(o_ref.dtype)


def _rmsnorm(x, g, tm):
    m, d = x.shape
    return pl.pallas_call(
        _rmsnorm_kernel,
        grid=(m // tm,),
        in_specs=[pl.BlockSpec((tm, d), lambda i: (i, 0)),
                  pl.BlockSpec((1, d), lambda i: (0, 0))],
        out_specs=pl.BlockSpec((tm, d), lambda i: (i, 0)),
        out_shape=jax.ShapeDtypeStruct((m, d), BF16),
        compiler_params=pltpu.CompilerParams(dimension_semantics=("arbitrary",),
                                             vmem_limit_bytes=VMEM_LIMIT),
        name="rmsnorm",
    )(x, g.reshape(1, d))


def _inproj_kernel(h_ref, w_ref, mu_ref, o_ref, carry_ref, *, tiles_per_seq):
    z = _dot(h_ref[...], w_ref[...])
    tm = z.shape[0]
    first = (pl.program_id(1) % tiles_per_seq) == 0
    prev_last = jnp.where(first, 0.0, carry_ref[7:8, :])
    row = lax.broadcasted_iota(jnp.int32, z.shape, 0)
    z_prev = jnp.where(row == 0, prev_last, pltpu.roll(z, 1, 0))
    o_ref[...] = z + (z_prev - z) * mu_ref[...]
    carry_ref[...] = z[tm - 8:tm, :]


def _inproj(h, w, mu, seq, tm, tn):
    m, d = h.shape
    n = w.shape[1]
    return pl.pallas_call(
        functools.partial(_inproj_kernel, tiles_per_seq=seq // tm),
        grid=(n // tn, m // tm),
        in_specs=[pl.BlockSpec((tm, d), lambda j, i: (i, 0)),
                  pl.BlockSpec((d, tn), lambda j, i: (0, j)),
                  pl.BlockSpec((1, tn), lambda j, i: (0, j))],
        out_specs=pl.BlockSpec((tm, tn), lambda j, i: (i, j)),
        out_shape=jax.ShapeDtypeStruct((m, n), F32),
        scratch_shapes=[pltpu.VMEM((8, tn), F32)],
        compiler_params=pltpu.CompilerParams(dimension_semantics=("arbitrary", "arbitrary"),
                                             vmem_limit_bytes=VMEM_LIMIT),
        name="inproj",
    )(h, w, mu)


def _rwkv_pair(r, k, v, gate, lora_b, lora_t, wup, aup, w0, a0, kkp, kap, rkp, lng, lnb, s_old):
    c = r.shape[0]
    lane = lax.broadcasted_iota(jnp.int32, (1, LANES), 1)
    m0 = lane < HEAD_DIM

    ri = lax.broadcasted_iota(jnp.int32, (LANES, LANES), 0)
    ci = lax.broadcasted_iota(jnp.int32, (LANES, LANES), 1)
    same_head = (ri // HEAD_DIM) == (ci // HEAD_DIM)
    ones_bd = jnp.where(same_head, 1.0, 0.0).astype(BF16)

    def segsum(x):
        return sum(_dot(p, ones_bd) for p in _split_bf16(x, 3))

    w_raw = w0 + _dot(lora_t, wup)
    logw = -EXP_NEG_HALF * _sigmoid(w_raw)
    a = _sigmoid(a0 + _dot(lora_b, aup))
    kkr = k * kkp
    kk = kkr / jnp.maximum(jnp.sqrt(segsum(kkr * kkr)), 1e-12)
    b = kk * a
    k2 = k * (1.0 + (a - 1.0) * kap)

    ti = lax.broadcasted_iota(jnp.int32, (c, c), 0)
    si = lax.broadcasted_iota(jnp.int32, (c, c), 1)
    tri = jnp.where(si <= ti, 1.0, 0.0).astype(BF16)
    g = sum(_dot(tri, p) for p in _split_bf16(logw, 3))
    g_last = g[c - 1:c, :]
    eg = jnp.exp(g)
    egp = jnp.exp(g - logw)
    eng = jnp.exp(-g)
    egd = jnp.exp(g_last - g)
    rg = r * eg
    kkg = kk * egp
    kn = k2 * eng
    bn = b * eng
    kd = k2 * egd
    bd = b * egd

    def stack(x):
        return jnp.concatenate([jnp.where(m0, x, 0.0), jnp.where(m0, 0.0, x)], axis=0)

    kkg_s = stack(kkg).astype(BF16)
    rg_s = stack(rg).astype(BF16)
    lhs_a = jnp.concatenate([kkg_s, rg_s], axis=0)
    rhs_a = jnp.concatenate([jnp.where(m0, bn, kn), jnp.where(m0, kn, bn)], axis=0).astype(BF16)
    aa = _dot_nt(lhs_a, rhs_a)
    aa_k = aa[0:2 * c]
    aa_r = aa[2 * c:4 * c]
    ri2 = lax.broadcasted_iota(jnp.int32, (2 * c, 2 * c), 0)
    ci2 = lax.broadcasted_iota(jnp.int32, (2 * c, 2 * c), 1)
    t_in = ri2 % c
    s_in = ci2 % c
    diag_blk = (ri2 // c) == (ci2 // c)
    strict = s_in < t_in
    l2 = jnp.where(strict & diag_blk, aa_k, 0.0)
    akk2 = jnp.where(strict & jnp.logical_not(diag_blk), aa_k, 0.0)
    ar2 = jnp.where(s_in <= t_in, aa_r, 0.0)

    eye = jnp.where(ri2 == ci2, 1.0, 0.0)
    p = eye - l2
    lb = l2.astype(BF16)
    lp = _dot(lb, lb)
    n_sq = int(math.log2(c)) - 1
    for i in range(n_sq):
        lpb = lp.astype(BF16)
        p = p + _dot(p.astype(BF16), lpb)
        if i + 1 < n_sq:
            lp = _dot(lpb, lpb)

    s_b = s_old.astype(BF16)
    v_sw = jnp.concatenate([jnp.where(m0, 0.0, v), jnp.where(m0, v, 0.0)], axis=0).astype(BF16)
    x2 = _dot_nt(kkg_s, s_b) + _dot(akk2.astype(BF16), v_sw)
    u2 = _dot(p.astype(BF16), x2.astype(BF16))
    u = u2[0:c] + u2[c:2 * c]
    z12 = jnp.concatenate([jnp.where(m0, -u, v), jnp.where(m0, v, -u)], axis=0).astype(BF16)
    y2 = _dot_nt(rg_s, s_b) + _dot(ar2.astype(BF16), z12)
    y = jnp.where(m0, y2[0:c], y2[c:2 * c])

    vu = jnp.concatenate([v, -u], axis=0).astype(BF16)
    kb = jnp.concatenate([kd, bd], axis=0).astype(BF16)
    s_new = s_old * jnp.exp(g_last) + jnp.where(same_head, _dot_tn(vu, kb), 0.0)

    inv_n = 1.0 / HEAD_DIM
    mean = segsum(y) * inv_n
    dlt = y - mean
    var = segsum(dlt * dlt) * inv_n
    yn = dlt * lax.rsqrt(var + GN_EPS) * lng + lnb
    bonus = segsum(r * k2 * rkp) * v
    out = (yn + bonus) * (gate * _sigmoid(gate))
    return out, s_new


def _rwkv_kernel(r_ref, k_ref, v_ref, g_ref, lora_ref, wup_ref, aup_ref, w0_ref, a0_ref, kk_ref, ka_ref,
                 rk_ref, lng_ref, lnb_ref, o_ref, s_ref, *, n_pairs):
    @pl.when(pl.program_id(1) == 0)
    def _():
        s_ref[...] = jnp.zeros_like(s_ref)

    lora = lora_ref[0][:, 0:LANES]
    lora_b = lora.astype(BF16)
    lora_t = jnp.tanh(lora).astype(BF16)
    for p in range(n_pairs):
        sl = slice(p * LANES, (p + 1) * LANES)
        out, s_new = _rwkv_pair(
            r_ref[0, :, sl], k_ref[0, :, sl], v_ref[0, :, sl], g_ref[0, :, sl], lora_b, lora_t,
            wup_ref[:, sl], aup_ref[:, sl], w0_ref[:, sl], a0_ref[:, sl], kk_ref[:, sl], ka_ref[:, sl],
            rk_ref[:, sl], lng_ref[:, sl], lnb_ref[:, sl], s_ref[p])
        s_ref[p] = s_new
        o_ref[0, :, sl] = out.astype(o_ref.dtype)


def _rwkv(z3, lora_blk, wup, aup, w0, a0, k_k, k_a, r_k, ln_g, ln_b, width):
    bsz, seq, _ = z3.shape
    c = RW_CHUNK
    n_pairs = width // LANES
    zspec = lambda j: pl.BlockSpec((1, c, width), lambda b, t, j=j: (b, t, j))
    pspec = lambda rows: pl.BlockSpec((rows, width), lambda b, t: (0, 0))
    return pl.pallas_call(
        functools.partial(_rwkv_kernel, n_pairs=n_pairs),
        grid=(bsz, seq // c),
        in_specs=[zspec(0), zspec(1), zspec(2), zspec(3),
                  pl.BlockSpec((1, c, 2 * LANES), lambda b, t: (b, t, lora_blk)),
                  pspec(LANES), pspec(LANES)] + [pspec(1)] * 7,
        out_specs=pl.BlockSpec((1, c, width), lambda b, t: (b, t, 0)),
        out_shape=jax.ShapeDtypeStruct((bsz, seq, width), BF16),
        scratch_shapes=[pltpu.VMEM((n_pairs, LANES, LANES), F32)],
        compiler_params=pltpu.CompilerParams(dimension_semantics=("arbitrary", "arbitrary"),
                                             vmem_limit_bytes=VMEM_LIMIT),
        name="rwkv7_scan",
    )(z3, z3, z3, z3, z3, wup, aup, w0, a0, k_k, k_a, r_k, ln_g, ln_b)


def _fcum_kernel(l_ref, bf_ref, o_ref, carry_ref):
    @pl.when(pl.program_id(1) == 0)
    def _():
        carry_ref[...] = jnp.zeros_like(carry_ref)

    x = l_ref[0] + bf_ref[...]
    log_f = jnp.minimum(x, 0.0) - jnp.log(1.0 + jnp.exp(-jnp.abs(x)))
    tc = x.shape[0]
    si = lax.broadcasted_iota(jnp.int32, (tc, tc), 0)
    ti = lax.broadcasted_iota(jnp.int32, (tc, tc), 1)
    triu = jnp.where(si <= ti, 1.0, 0.0).astype(BF16)
    c_t = sum(_dot_tn(p, triu) for p in _split_bf16(log_f, 3)) + carry_ref[...]
    carry_ref[...] = jnp.broadcast_to(c_t[:, tc - 1:tc], carry_ref.shape)
    n_heads = o_ref.shape[1]
    o_ref[0] = c_t[LANES:LANES + n_heads, :]


def _fcum(z3, lora_blk, bf_pad, n_heads, tc):
    bsz, seq, _ = z3.shape
    return pl.pallas_call(
        _fcum_kernel,
        grid=(bsz, seq // tc),
        in_specs=[pl.BlockSpec((1, tc, 2 * LANES), lambda b, t: (b, t, lora_blk)),
                  pl.BlockSpec((1, 2 * LANES), lambda b, t: (0, 0))],
        out_specs=pl.BlockSpec((1, n_heads, tc), lambda b, t: (b, 0, t)),
        out_shape=jax.ShapeDtypeStruct((bsz, n_heads, seq), F32),
        scratch_shapes=[pltpu.VMEM((2 * LANES, tc), F32)],
        compiler_params=pltpu.CompilerParams(dimension_semantics=("arbitrary", "arbitrary"),
                                             vmem_limit_bytes=VMEM_LIMIT),
        name="fox_cumsum",
    )(z3, bf_pad)


def _fox_kernel(q_ref, k_ref, v_ref, g_ref, c_ref, o_ref, m_ref, l_ref, acc_ref, *, tq):
    pair = pl.program_id(1)
    qi = pl.program_id(2)
    lane = lax.broadcasted_iota(jnp.int32, (1, LANES), 1)
    m0 = lane < HEAD_DIM
    q = q_ref[0] * Q_SCALE
    qh = (jnp.where(m0, q, 0.0).astype(BF16), jnp.where(m0, 0.0, q).astype(BF16))
    m_ref[...] = jnp.full_like(m_ref, NEG_BIG)
    l_ref[...] = jnp.zeros_like(l_ref)
    acc_ref[...] = jnp.zeros_like(acc_ref)

    def step(j, masked):
        ks = pl.multiple_of(j * tq, tq)
        kb = k_ref[0, pl.ds(ks, tq), :].astype(BF16)
        vb = v_ref[0, pl.ds(ks, tq), :].astype(BF16)
        for h in range(2):
            c_row = c_ref[0, pl.ds(2 * pair + h, 1), pl.ds(ks, tq)]
            s = _dot_nt(qh[h], kb) - c_row
            if masked:
                rr = lax.broadcasted_iota(jnp.int32, (tq, tq), 0)
                cc = lax.broadcasted_iota(jnp.int32, (tq, tq), 1)
                s = jnp.where(rr >= cc, s, NEG_BIG)
            m_old = m_ref[h]
            m_new = jnp.maximum(m_old, jnp.max(s, axis=1, keepdims=True))
            alpha = jnp.exp(m_old - m_new)
            pexp = jnp.exp(s - m_new)
            l_ref[h] = alpha * l_ref[h] + jnp.sum(pexp, axis=1, keepdims=True)
            acc_ref[h] = alpha * acc_ref[h] + _dot(pexp.astype(BF16), vb)
            m_ref[h] = m_new

    def body(j, carry):
        step(j, False)
        return carry

    lax.fori_loop(0, qi, body, 0)
    step(qi, True)
    o0 = acc_ref[0] / l_ref[0]
    o1 = acc_ref[1] / l_ref[1]
    gate = g_ref[0]
    o_ref[0] = (jnp.where(m0, o0, o1) * (gate * _sigmoid(gate))).astype(o_ref.dtype)


def _fox(z3, c_t, col0, width, tq):
    bsz, seq, _ = z3.shape
    n_pairs = width // LANES
    blk = lambda j: col0 // LANES + j * n_pairs
    return pl.pallas_call(
        functools.partial(_fox_kernel, tq=tq),
        grid=(bsz, n_pairs, seq // tq),
        in_specs=[pl.BlockSpec((1, tq, LANES), lambda b, p, i: (b, i, blk(0) + p)),
                  pl.BlockSpec((1, seq, LANES), lambda b, p, i: (b, 0, blk(1) + p)),
                  pl.BlockSpec((1, seq, LANES), lambda b, p, i: (b, 0, blk(2) + p)),
                  pl.BlockSpec((1, tq, LANES), lambda b, p, i: (b, i, blk(3) + p)),
                  pl.BlockSpec((1, c_t.shape[1], seq), lambda b, p, i: (b, 0, 0))],
        out_specs=pl.BlockSpec((1, tq, LANES), lambda b, p, i: (b, i, p)),
        out_shape=jax.ShapeDtypeStruct((bsz, seq, width), BF16),
        scratch_shapes=[pltpu.VMEM((2, tq, 1), F32), pltpu.VMEM((2, tq, 1), F32),
                        pltpu.VMEM((2, tq, LANES), F32)],
        compiler_params=pltpu.CompilerParams(dimension_semantics=("arbitrary",) * 3,
                                             vmem_limit_bytes=VMEM_LIMIT),
        name="fox_attention",
    )(z3, z3, z3, z3, c_t)


def _merge_kernel(yr_ref, yf_ref, gr_ref, gf_ref, x_ref, wr_ref, wf_ref, wo_ref, o_ref):
    u_rw = _dot(yr_ref[...], wr_ref[...])
    u_fox = _dot(yf_ref[...], wf_ref[...])
    merged = _sigmoid(gr_ref[...]) * u_rw + _sigmoid(gf_ref[...]) * u_fox
    o_ref[...] = x_ref[...] + _dot(merged.astype(BF16), wo_ref[...])


def _merge(y_rw, y_fox, z, gate_col0, x, w_up_rw, w_up_fox, w_out, tm):
    m, d = x.shape
    cw = y_rw.shape[1]
    gblk = gate_col0 // d
    const = lambda shape: pl.BlockSpec(shape, lambda i: (0, 0), pipeline_mode=pl.Buffered(1))
    return pl.pallas_call(
        _merge_kernel,
        grid=(m // tm,),
        in_specs=[pl.BlockSpec((tm, cw), lambda i: (i, 0)),
                  pl.BlockSpec((tm, cw), lambda i: (i, 0)),
                  pl.BlockSpec((tm, d), lambda i: (i, gblk)),
                  pl.BlockSpec((tm, d), lambda i: (i, gblk + 1)),
                  pl.BlockSpec((tm, d), lambda i: (i, 0)),
                  const((cw, d)), const((cw, d)), const((d, d))],
        out_specs=pl.BlockSpec((tm, d), lambda i: (i, 0)),
        out_shape=jax.ShapeDtypeStruct((m, d), F32),
        compiler_params=pltpu.CompilerParams(dimension_semantics=("arbitrary",),
                                             vmem_limit_bytes=VMEM_LIMIT),
        name="merge_outproj",
    )(y_rw, y_fox, z, z, x, w_up_rw, w_up_fox, w_out)


def _ple_kernel(x_ref, p_ref, ng_ref, fg_ref, wp_ref, wg_ref, o_ref):
    x = x_ref[...]
    hn = x * lax.rsqrt(jnp.mean(x * x, axis=-1, keepdims=True) + NORM_EPS) * ng_ref[...]
    gate = _sigmoid(_dot(hn.astype(BF16), wg_ref[...]))
    ple = _dot(p_ref[...].astype(BF16), wp_ref[...])
    x2 = x + ple * gate
    o_ref[...] = x2 * lax.rsqrt(jnp.mean(x2 * x2, axis=-1, keepdims=True) + NORM_EPS) * fg_ref[...]


def _ple(x1, p, ple_norm_g, final_norm_g, ple_proj, ple_gate_w, tm):
    m, d = x1.shape
    pd = p.shape[1]
    const = lambda shape: pl.BlockSpec(shape, lambda i: (0, 0), pipeline_mode=pl.Buffered(1))
    return pl.pallas_call(
        _ple_kernel,
        grid=(m // tm,),
        in_specs=[pl.BlockSpec((tm, d), lambda i: (i, 0)),
                  pl.BlockSpec((tm, pd), lambda i: (i, 0)),
                  const((1, d)), const((1, d)), const((pd, d)), const((d, d))],
        out_specs=pl.BlockSpec((tm, d), lambda i: (i, 0)),
        out_shape=jax.ShapeDtypeStruct((m, d), F32),
        compiler_params=pltpu.CompilerParams(dimension_semantics=("arbitrary",),
                                             vmem_limit_bytes=VMEM_LIMIT),
        name="ple_final_norm",
    )(x1, p, ple_norm_g.reshape(1, d), final_norm_g.reshape(1, d), ple_proj, ple_gate_w)


def _pad_cols(a, n):
    return jnp.pad(a, ((0, 0), (0, n - a.shape[1])))


def _layer(x2d, p2d, seq, norm_g, w_in, rw_shift_mu, rw_w0, rw_w_lora_up, rw_a0, rw_a_lora_up, rw_k_k, rw_k_a,
           rw_r_k, rw_ln_g, rw_ln_b, fox_b_f, w_up_rwkv, w_up_fox, w_out, ple_proj, ple_gate_w, ple_norm_g,
           final_norm_g, *, tiles):
    m, d = x2d.shape
    bsz = m // seq
    cw = w_up_rwkv.shape[0]
    n_heads = fox_b_f.shape[0]
    lw = rw_w_lora_up.shape[0]
    assert 2 * lw == LANES and cw % LANES == 0

    rw_cols = 4 * cw + 2 * lw
    fox_cols = 4 * cw + n_heads
    fox0, gate0, small0 = 4 * cw, 8 * cw, 8 * cw + 2 * d
    n_pad = small0 + 2 * LANES
    w_cat = jnp.concatenate([
        w_in[:, 0:4 * cw], w_in[:, rw_cols:rw_cols + 4 * cw], w_in[:, rw_cols + fox_cols:],
        w_in[:, 4 * cw:rw_cols], _pad_cols(w_in[:, rw_cols + 4 * cw:rw_cols + fox_cols], LANES)],
        axis=1).astype(BF16)
    mu_cat = jnp.concatenate([
        rw_shift_mu[0:4 * cw], jnp.zeros((small0 - 4 * cw,), F32), rw_shift_mu[4 * cw:],
        jnp.zeros((LANES,), F32)]).reshape(1, n_pad)

    h = _rmsnorm(x2d, norm_g, tiles["norm_tm"])
    z = _inproj(h, w_cat, mu_cat, seq, tiles["in_tm"], tiles["in_tn"])
    z3 = z.reshape(bsz, seq, n_pad)
    lora_blk = small0 // (2 * LANES)

    row = lambda a: a.reshape(1, cw)
    wup = jnp.pad(rw_w_lora_up, ((0, lw), (0, 0))).astype(BF16)
    aup = jnp.pad(rw_a_lora_up, ((lw, 0), (0, 0))).astype(BF16)
    y_rw = _rwkv(z3, lora_blk, wup, aup, row(rw_w0), row(rw_a0), row(rw_k_k), row(rw_k_a), row(rw_r_k),
                 row(rw_ln_g), row(rw_ln_b), cw)

    bf_pad = jnp.pad(fox_b_f, (LANES, LANES - n_heads)).reshape(1, 2 * LANES)
    c_t = _fcum(z3, lora_blk, bf_pad, n_heads, tiles["fc_tc"])
    y_fox = _fox(z3, c_t, fox0, cw, tiles["fox_tq"])

    x1 = _merge(y_rw.reshape(m, cw), y_fox.reshape(m, cw), z, gate0, x2d, w_up_rwkv.astype(BF16),
                w_up_fox.astype(BF16), w_out.astype(BF16), tiles["merge_tm"])
    return _ple(x1, p2d, ple_norm_g, final_norm_g, ple_proj.astype(BF16), ple_gate_w.astype(BF16),
                tiles["ple_tm"])


_TILES = dict(norm_tm=512, in_tm=512, in_tn=1792, fc_tc=256, fox_tq=256, merge_tm=256, ple_tm=256)


def kernel(x, p, norm_g, w_in, rw_shift_mu, rw_w0, rw_w_lora_up, rw_a0, rw_a_lora_up, rw_k_k, rw_k_a, rw_r_k,
           rw_ln_g, rw_ln_b, fox_b_f, w_up_rwkv, w_up_fox, w_out, ple_proj, ple_gate_w, ple_norm_g,
           final_norm_g):
    bsz, seq, d = x.shape
    depth = p.shape[0]
    assert depth == 1, "the final norm is fused into the (single) layer"
    out = _layer(x.reshape(bsz * seq, d), p[0].reshape(bsz * seq, -1), seq, norm_g[0], w_in[0], rw_shift_mu[0],
                 rw_w0[0], rw_w_lora_up[0], rw_a0[0], rw_a_lora_up[0], rw_k_k[0], rw_k_a[0],
                 rw_r_k[0].reshape(-1), rw_ln_g[0], rw_ln_b[0], fox_b_f[0], w_up_rwkv[0], w_up_fox[0], w_out[0],
                 ple_proj[0], ple_gate_w[0], ple_norm_g[0], final_norm_g, tiles=_TILES)
    return out.reshape(bsz, seq, d)
```

```python
import functools
import math

import jax
import jax.numpy as jnp
from jax import lax
from jax.experimental import pallas as pl
from jax.experimental.pallas import tpu as pltpu

F32 = jnp.float32
BF16 = jnp.bfloat16

HEAD_DIM = 64
LANES = 128
NORM_EPS = 1e-6
GN_EPS = 64e-5
Q_SCALE = HEAD_DIM ** -0.5
EXP_NEG_HALF = math.exp(-0.5)
NEG_BIG = -1e30
VMEM_LIMIT = 56 * 1024 * 1024

SPLIT_TERMS = 2
RW_CHUNK = 64
C_TERMS = 3


def _dot(a, b):
    return jnp.dot(a, b, preferred_element_type=F32)


def _dot_nt(a, b):
    return lax.dot_general(a, b, (((1,), (1,)), ((), ())), preferred_element_type=F32)


def _dot_tn(a, b):
    return lax.dot_general(a, b, (((0,), (0,)), ((), ())), preferred_element_type=F32)


def _split_bf16(x, n):
    parts = []
    rem = x
    for i in range(n):
        p = rem.astype(BF16)
        parts.append(p)
        if i + 1 < n:
            rem = rem - p.astype(F32)
    return parts


def _sigmoid(x):
    return 1.0 / (1.0 + jnp.exp(-x))


def _rmsnorm_kernel(x_ref, g_ref, o_ref):
    x = x_ref[...]
    y = x * lax.rsqrt(jnp.mean(x * x, axis=-1, keepdims=True) + NORM_EPS)
    o_ref[...] = (y * g_ref[...]).astype(o_ref.dtype)


def _rmsnorm(x, g, tm):
    m, d = x.shape
    return pl.pallas_call(
        _rmsnorm_kernel,
        grid=(m // tm,),
        in_specs=[pl.BlockSpec((tm, d), lambda i: (i, 0)),
                  pl.BlockSpec((1, d), lambda i: (0, 0))],
        out_specs=pl.BlockSpec((tm, d), lambda i: (i, 0)),
        out_shape=jax.ShapeDtypeStruct((m, d), BF16),
        compiler_params=pltpu.CompilerParams(dimension_semantics=("arbitrary",),
                                             vmem_limit_bytes=VMEM_LIMIT),
        name="rmsnorm",
    )(x, g.reshape(1, d))


def _inproj_kernel(h_ref, w_ref, mu_ref, o_ref, carry_ref, *, tiles_per_seq):
    z = _dot(h_ref[...], w_ref[...])
    tm = z.shape[0]
    first = (pl.program_id(1) % tiles_per_seq) == 0
    prev_last = jnp.where(first, 0.0, carry_ref[7:8, :])
    row = lax.broadcasted_iota(jnp.int32, z.shape, 0)
    z_prev = jnp.where(row == 0, prev_last, pltpu.roll(z, 1, 0))
    o_ref[...] = z + (z_prev - z) * mu_ref[...]
    carry_ref[...] = z[tm - 8:tm, :]


def _inproj(h, w, mu, seq, tm, tn):
    m, d = h.shape
    n = w.shape[1]
    return pl.pallas_call(
        functools.partial(_inproj_kernel, tiles_per_seq=seq // tm),
        grid=(n // tn, m // tm),
        in_specs=[pl.BlockSpec((tm, d), lambda j, i: (i, 0)),
                  pl.BlockSpec((d, tn), lambda j, i: (0, j)),
                  pl.BlockSpec((1, tn), lambda j, i: (0, j))],
        out_specs=pl.BlockSpec((tm, tn), lambda j, i: (i, j)),
        out_shape=jax.ShapeDtypeStruct((m, n), F32),
        scratch_shapes=[pltpu.VMEM((8, tn), F32)],
        compiler_params=pltpu.CompilerParams(dimension_semantics=("arbitrary", "arbitrary"),
                                             vmem_limit_bytes=VMEM_LIMIT),
        name="inproj",
    )(h, w, mu)


def _rwkv_chunk(r, k, v, gate, lora_b, lora_t, wup, aup, w0, a0, kkp, kap, rkp, lng, lnb, s_old):
    c, width = r.shape
    pairs = range(width // LANES)
    cols = lambda x: [x[:, p * LANES:(p + 1) * LANES] for p in pairs]
    lane = lax.broadcasted_iota(jnp.int32, (1, LANES), 1)
    m0 = lane < HEAD_DIM

    ri = lax.broadcasted_iota(jnp.int32, (LANES, LANES), 0)
    ci = lax.broadcasted_iota(jnp.int32, (LANES, LANES), 1)
    same_head = (ri // HEAD_DIM) == (ci // HEAD_DIM)
    ones_bd = jnp.where(same_head, 1.0, 0.0).astype(BF16)

    def segsum(x):
        xr = jnp.concatenate(cols(x), axis=0)
        sr = sum(_dot(part, ones_bd) for part in _split_bf16(xr, SPLIT_TERMS))
        return jnp.concatenate([sr[p * c:(p + 1) * c] for p in pairs], axis=1)

    w_raw = w0 + _dot(lora_t, wup)
    logw = -EXP_NEG_HALF * _sigmoid(w_raw)
    a = _sigmoid(a0 + _dot(lora_b, aup))
    kkr = k * kkp
    kk = kkr / jnp.maximum(jnp.sqrt(segsum(kkr * kkr)), 1e-12)
    b = kk * a
    k2 = k * (1.0 + (a - 1.0) * kap)

    ti = lax.broadcasted_iota(jnp.int32, (c, c), 0)
    si = lax.broadcasted_iota(jnp.int32, (c, c), 1)
    tri = jnp.where(si <= ti, 1.0, 0.0).astype(BF16)
    g = sum(_dot(tri, part) for part in _split_bf16(logw, SPLIT_TERMS))
    g_last = g[c - 1:c, :]
    egd = jnp.exp(g_last - g)
    eng = jnp.exp(-g)
    rg = cols(r * jnp.exp(g))
    kkg = cols(kk * jnp.exp(g - logw))
    kn = cols(k2 * eng)
    bn = cols(b * eng)
    kd = cols(k2 * egd)
    bd = cols(b * egd)
    vs = cols(v)
    decay_c = cols(jnp.exp(g_last))

    def stack(x):
        return jnp.concatenate([jnp.where(m0, x, 0.0), jnp.where(m0, 0.0, x)], axis=0).astype(BF16)

    ri2 = lax.broadcasted_iota(jnp.int32, (2 * c, 2 * c), 0)
    ci2 = lax.broadcasted_iota(jnp.int32, (2 * c, 2 * c), 1)
    t_in = ri2 % c
    s_in = ci2 % c
    diag_blk = (ri2 // c) == (ci2 // c)
    strict = s_in < t_in
    m_l = strict & diag_blk
    m_kk = strict & jnp.logical_not(diag_blk)
    m_r = s_in <= t_in
    eye = jnp.where(ri2 == ci2, 1.0, 0.0)

    lhs_s = [jnp.concatenate([stack(kkg[p]), stack(rg[p])], axis=0) for p in pairs]
    rhs_a = [jnp.concatenate([jnp.where(m0, bn[p], kn[p]), jnp.where(m0, kn[p], bn[p])], axis=0).astype(BF16)
             for p in pairs]
    aa = [_dot_nt(lhs_s[p], rhs_a[p]) for p in pairs]
    l2 = [jnp.where(m_l, aa[p][0:2 * c], 0.0) for p in pairs]
    akk2 = [jnp.where(m_kk, aa[p][0:2 * c], 0.0).astype(BF16) for p in pairs]
    ar2 = [jnp.where(m_r, aa[p][2 * c:4 * c], 0.0).astype(BF16) for p in pairs]

    tinv = [eye - l2[p] for p in pairs]
    lb = [l2[p].astype(BF16) for p in pairs]
    lpow = [_dot(lb[p], lb[p]) for p in pairs]
    n_sq = int(math.log2(c)) - 1
    for i in range(n_sq):
        last = i + 1 == n_sq
        lpb = [lpow[p].astype(BF16) for p in pairs]
        lhs = [tinv[p].astype(BF16) if last else jnp.concatenate([tinv[p], lpow[p]], axis=0).astype(BF16)
               for p in pairs]
        prod = [_dot(lhs[p], lpb[p]) for p in pairs]
        tinv = [tinv[p] + prod[p][0:2 * c] for p in pairs]
        if not last:
            lpow = [prod[p][2 * c:4 * c] for p in pairs]
    tinv = [t.astype(BF16) for t in tinv]

    s_b = [s_old[p].astype(BF16) for p in pairs]
    sh = [_dot_nt(lhs_s[p], s_b[p]) for p in pairs]
    v_sw = [jnp.concatenate([jnp.where(m0, 0.0, vs[p]), jnp.where(m0, vs[p], 0.0)], axis=0).astype(BF16)
            for p in pairs]
    x2 = [sh[p][0:2 * c] + _dot(akk2[p], v_sw[p]) for p in pairs]
    u2 = [_dot(tinv[p], x2[p].astype(BF16)) for p in pairs]
    u = [u2[p][0:c] + u2[p][c:2 * c] for p in pairs]
    z12 = [jnp.concatenate([jnp.where(m0, -u[p], vs[p]), jnp.where(m0, vs[p], -u[p])], axis=0).astype(BF16)
           for p in pairs]
    y2 = [sh[p][2 * c:4 * c] + _dot(ar2[p], z12[p]) for p in pairs]
    y = jnp.concatenate([jnp.where(m0, y2[p][0:c], y2[p][c:2 * c]) for p in pairs], axis=1)

    vu = [jnp.concatenate([vs[p], -u[p]], axis=0).astype(BF16) for p in pairs]
    kb = [jnp.concatenate([kd[p], bd[p]], axis=0).astype(BF16) for p in pairs]
    s_new = [s_old[p] * decay_c[p] + jnp.where(same_head, _dot_tn(vu[p], kb[p]), 0.0) for p in pairs]

    inv_n = 1.0 / HEAD_DIM
    mean = segsum(y) * inv_n
    dlt = y - mean
    var = segsum(dlt * dlt) * inv_n
    yn = dlt * lax.rsqrt(var + GN_EPS) * lng + lnb
    bonus = segsum(r * k2 * rkp) * v
    out = (yn + bonus) * (gate * _sigmoid(gate))
    return out, s_new


def _rwkv_kernel(r_ref, k_ref, v_ref, g_ref, lora_ref, wup_ref, aup_ref, w0_ref, a0_ref, kk_ref, ka_ref,
                 rk_ref, lng_ref, lnb_ref, o_ref, s_ref, *, n_pairs):
    @pl.when(pl.program_id(1) == 0)
    def _():
        s_ref[...] = jnp.zeros_like(s_ref)

    lora = lora_ref[0][:, 0:LANES]
    out, s_new = _rwkv_chunk(
        r_ref[0], k_ref[0], v_ref[0], g_ref[0], lora.astype(BF16), jnp.tanh(lora).astype(BF16),
        wup_ref[...], aup_ref[...], w0_ref[...], a0_ref[...], kk_ref[...], ka_ref[...], rk_ref[...],
        lng_ref[...], lnb_ref[...], [s_ref[p] for p in range(n_pairs)])
    for p in range(n_pairs):
        s_ref[p] = s_new[p]
    o_ref[0] = out.astype(o_ref.dtype)


def _rwkv(z3, lora_blk, wup, aup, w0, a0, k_k, k_a, r_k, ln_g, ln_b, width):
    bsz, seq, _ = z3.shape
    c = RW_CHUNK
    n_pairs = width // LANES
    zspec = lambda j: pl.BlockSpec((1, c, width), lambda b, t, j=j: (b, t, j))
    pspec = lambda rows: pl.BlockSpec((rows, width), lambda b, t: (0, 0))
    return pl.pallas_call(
        functools.partial(_rwkv_kernel, n_pairs=n_pairs),
        grid=(bsz, seq // c),
        in_specs=[zspec(0), zspec(1), zspec(2), zspec(3),
                  pl.BlockSpec((1, c, 2 * LANES), lambda b, t: (b, t, lora_blk)),
                  pspec(LANES), pspec(LANES)] + [pspec(1)] * 7,
        out_specs=pl.BlockSpec((1, c, width), lambda b, t: (b, t, 0)),
        out_shape=jax.ShapeDtypeStruct((bsz, seq, width), BF16),
        scratch_shapes=[pltpu.VMEM((n_pairs, LANES, LANES), F32)],
        compiler_params=pltpu.CompilerParams(dimension_semantics=("arbitrary", "arbitrary"),
                                             vmem_limit_bytes=VMEM_LIMIT),
        name="rwkv7_scan",
    )(z3, z3, z3, z3, z3, wup, aup, w0, a0, k_k, k_a, r_k, ln_g, ln_b)


def _fcum_kernel(l_ref, bf_ref, o_ref, carry_ref):
    @pl.when(pl.program_id(1) == 0)
    def _():
        carry_ref[...] = jnp.zeros_like(carry_ref)

    x = l_ref[0] + bf_ref[...]
    log_f = jnp.minimum(x, 0.0) - jnp.log(1.0 + jnp.exp(-jnp.abs(x)))
    tc = x.shape[0]
    ti = lax.broadcasted_iota(jnp.int32, (tc, tc), 0)
    si = lax.broadcasted_iota(jnp.int32, (tc, tc), 1)
    tril = jnp.where(si <= ti, 1.0, 0.0).astype(BF16)
    c = sum(_dot(tril, part) for part in _split_bf16(log_f, 3)) + carry_ref[0:1, :]
    carry_ref[...] = jnp.broadcast_to(c[tc - 1:tc, :], carry_ref.shape)
    lin = lax.broadcasted_iota(jnp.int32, (2 * LANES, LANES), 0) - LANES
    lout = lax.broadcasted_iota(jnp.int32, (2 * LANES, LANES), 1)
    terms = _split_bf16(-c, C_TERMS)
    sel = [jnp.where((lin >= 0) & (lout == C_TERMS * lin + i), 1.0, 0.0).astype(BF16) for i in range(C_TERMS)]
    o_ref[0] = sum(_dot(terms[i], sel[i]) for i in range(C_TERMS)).astype(o_ref.dtype)


def _fcum(z3, lora_blk, bf_pad, tc):
    bsz, seq, _ = z3.shape
    return pl.pallas_call(
        _fcum_kernel,
        grid=(bsz, seq // tc),
        in_specs=[pl.BlockSpec((1, tc, 2 * LANES), lambda b, t: (b, t, lora_blk)),
                  pl.BlockSpec((1, 2 * LANES), lambda b, t: (0, 0))],
        out_specs=pl.BlockSpec((1, tc, LANES), lambda b, t: (b, t, 0)),
        out_shape=jax.ShapeDtypeStruct((bsz, seq, LANES), BF16),
        scratch_shapes=[pltpu.VMEM((8, 2 * LANES), F32)],
        compiler_params=pltpu.CompilerParams(dimension_semantics=("arbitrary", "arbitrary"),
                                             vmem_limit_bytes=VMEM_LIMIT),
        name="fox_cumsum",
    )(z3, bf_pad)


def _fox_kernel(q_ref, k_ref, v_ref, g_ref, c_ref, o_ref, vt_ref, m_ref, l_ref, acc_ref, *, tq, t_chunk):
    pair = pl.program_id(1)
    qi = pl.program_id(2)
    seq = k_ref.shape[1]

    @pl.when(qi == 0)
    def _():
        for t0 in range(0, seq, t_chunk):
            vt_ref[:, t0:t0 + t_chunk] = v_ref[0, t0:t0 + t_chunk, :].T.astype(BF16)

    lane = lax.broadcasted_iota(jnp.int32, (1, LANES), 1)
    q = q_ref[0] * Q_SCALE
    q_t = []
    for h in range(2):
        in_head = (lane // HEAD_DIM) == h
        c_lane0 = C_TERMS * (2 * pair + h)
        ones = jnp.where((lane >= c_lane0) & (lane < c_lane0 + C_TERMS), 1.0, 0.0)
        qa = jnp.concatenate([jnp.where(in_head, q, 0.0), jnp.broadcast_to(ones, q.shape)], axis=1)
        q_t.append(qa.T.astype(BF16))
    m_ref[...] = jnp.full_like(m_ref, NEG_BIG)
    l_ref[...] = jnp.zeros_like(l_ref)
    acc_ref[...] = jnp.zeros_like(acc_ref)

    def step(j, masked):
        ks = pl.multiple_of(j * tq, tq)
        ka = jnp.concatenate([k_ref[0, pl.ds(ks, tq), :].astype(BF16), c_ref[0, pl.ds(ks, tq), :]], axis=1)
        heads = range(2)
        s = [_dot(ka, q_t[h]) for h in heads]
        if masked:
            kk_i = lax.broadcasted_iota(jnp.int32, (tq, tq), 0)
            qq_i = lax.broadcasted_iota(jnp.int32, (tq, tq), 1)
            s = [jnp.where(kk_i <= qq_i, s[h], NEG_BIG) for h in heads]
        m_old = [m_ref[h] for h in heads]
        m_new = [jnp.maximum(m_old[h], jnp.max(s[h], axis=0, keepdims=True)) for h in heads]
        alpha = [jnp.exp(m_old[h] - m_new[h]) for h in heads]
        pexp = [jnp.exp(s[h] - m_new[h]) for h in heads]
        l_new = [alpha[h] * l_ref[h] + jnp.sum(pexp[h], axis=0, keepdims=True) for h in heads]
        pv = [_dot(vt_ref[h * HEAD_DIM:(h + 1) * HEAD_DIM, pl.ds(ks, tq)], pexp[h].astype(BF16)) for h in heads]
        for h in heads:
            acc_ref[h] = alpha[h] * acc_ref[h] + pv[h]
            l_ref[h] = l_new[h]
            m_ref[h] = m_new[h]

    def body(j, carry):
        step(j, False)
        return carry

    lax.fori_loop(0, qi, body, 0)
    step(qi, True)
    o_t = jnp.concatenate([acc_ref[0] / l_ref[0], acc_ref[1] / l_ref[1]], axis=0)
    gate = g_ref[0]
    o_ref[0] = (o_t.T * (gate * _sigmoid(gate))).astype(o_ref.dtype)


def _fox(z3, c_sp, col0, width, tq):
    bsz, seq, _ = z3.shape
    n_pairs = width // LANES
    blk = lambda j: col0 // LANES + j * n_pairs
    return pl.pallas_call(
        functools.partial(_fox_kernel, tq=tq, t_chunk=min(seq, 512)),
        grid=(bsz, n_pairs, seq // tq),
        in_specs=[pl.BlockSpec((1, tq, LANES), lambda b, p, i: (b, i, blk(0) + p)),
                  pl.BlockSpec((1, seq, LANES), lambda b, p, i: (b, 0, blk(1) + p)),
                  pl.BlockSpec((1, seq, LANES), lambda b, p, i: (b, 0, blk(2) + p)),
                  pl.BlockSpec((1, tq, LANES), lambda b, p, i: (b, i, blk(3) + p)),
                  pl.BlockSpec((1, seq, LANES), lambda b, p, i: (b, 0, 0))],
        out_specs=pl.BlockSpec((1, tq, LANES), lambda b, p, i: (b, i, p)),
        out_shape=jax.ShapeDtypeStruct((bsz, seq, width), BF16),
        scratch_shapes=[pltpu.VMEM((LANES, seq), BF16),
                        pltpu.VMEM((2, 1, tq), F32), pltpu.VMEM((2, 1, tq), F32),
                        pltpu.VMEM((2, HEAD_DIM, tq), F32)],
        compiler_params=pltpu.CompilerParams(dimension_semantics=("arbitrary",) * 3,
                                             vmem_limit_bytes=VMEM_LIMIT),
        name="fox_attention",
    )(z3, z3, z3, z3, c_sp)


def _merge_kernel(yr_ref, yf_ref, gr_ref, gf_ref, x_ref, wr_ref, wf_ref, wo_ref, o_ref):
    u_rw = _dot(yr_ref[...], wr_ref[...])
    u_fox = _dot(yf_ref[...], wf_ref[...])
    merged = _sigmoid(gr_ref[...]) * u_rw + _sigmoid(gf_ref[...]) * u_fox
    o_ref[...] = x_ref[...] + _dot(merged.astype(BF16), wo_ref[...])


def _merge(y_rw, y_fox, z, gate_col0, x, w_up_rw, w_up_fox, w_out, tm):
    m, d = x.shape
    cw = y_rw.shape[1]
    gblk = gate_col0 // d
    const = lambda shape: pl.BlockSpec(shape, lambda i: (0, 0), pipeline_mode=pl.Buffered(1))
    return pl.pallas_call(
        _merge_kernel,
        grid=(m // tm,),
        in_specs=[pl.BlockSpec((tm, cw), lambda i: (i, 0)),
                  pl.BlockSpec((tm, cw), lambda i: (i, 0)),
                  pl.BlockSpec((tm, d), lambda i: (i, gblk)),
                  pl.BlockSpec((tm, d), lambda i: (i, gblk + 1)),
                  pl.BlockSpec((tm, d), lambda i: (i, 0)),
                  const((cw, d)), const((cw, d)), const((d, d))],
        out_specs=pl.BlockSpec((tm, d), lambda i: (i, 0)),
        out_shape=jax.ShapeDtypeStruct((m, d), F32),
        compiler_params=pltpu.CompilerParams(dimension_semantics=("arbitrary",),
                                             vmem_limit_bytes=VMEM_LIMIT),
        name="merge_outproj",
    )(y_rw, y_fox, z, z, x, w_up_rw, w_up_fox, w_out)


def _ple_kernel(x_ref, p_ref, ng_ref, fg_ref, wp_ref, wg_ref, o_ref):
    x = x_ref[...]
    hn = x * lax.rsqrt(jnp.mean(x * x, axis=-1, keepdims=True) + NORM_EPS) * ng_ref[...]
    gate = _sigmoid(_dot(hn.astype(BF16), wg_ref[...]))
    ple = _dot(p_ref[...].astype(BF16), wp_ref[...])
    x2 = x + ple * gate
    o_ref[...] = x2 * lax.rsqrt(jnp.mean(x2 * x2, axis=-1, keepdims=True) + NORM_EPS) * fg_ref[...]


def _ple(x1, p, ple_norm_g, final_norm_g, ple_proj, ple_gate_w, tm):
    m, d = x1.shape
    pd = p.shape[1]
    const = lambda shape: pl.BlockSpec(shape, lambda i: (0, 0), pipeline_mode=pl.Buffered(1))
    return pl.pallas_call(
        _ple_kernel,
        grid=(m // tm,),
        in_specs=[pl.BlockSpec((tm, d), lambda i: (i, 0)),
                  pl.BlockSpec((tm, pd), lambda i: (i, 0)),
                  const((1, d)), const((1, d)), const((pd, d)), const((d, d))],
        out_specs=pl.BlockSpec((tm, d), lambda i: (i, 0)),
        out_shape=jax.ShapeDtypeStruct((m, d), F32),
        compiler_params=pltpu.CompilerParams(dimension_semantics=("arbitrary",),
                                             vmem_limit_bytes=VMEM_LIMIT),
        name="ple_final_norm",
    )(x1, p, ple_norm_g.reshape(1, d), final_norm_g.reshape(1, d), ple_proj, ple_gate_w)


def _pad_cols(a, n):
    return jnp.pad(a, ((0, 0), (0, n - a.shape[1])))


def _layer(x2d, p2d, seq, norm_g, w_in, rw_shift_mu, rw_w0, rw_w_lora_up, rw_a0, rw_a_lora_up, rw_k_k, rw_k_a,
           rw_r_k, rw_ln_g, rw_ln_b, fox_b_f, w_up_rwkv, w_up_fox, w_out, ple_proj, ple_gate_w, ple_norm_g,
           final_norm_g, *, tiles):
    m, d = x2d.shape
    bsz = m // seq
    cw = w_up_rwkv.shape[0]
    n_heads = fox_b_f.shape[0]
    lw = rw_w_lora_up.shape[0]
    assert 2 * lw == LANES and cw % LANES == 0 and C_TERMS * n_heads <= LANES

    rw_cols = 4 * cw + 2 * lw
    fox_cols = 4 * cw + n_heads
    fox0, gate0, small0 = 4 * cw, 8 * cw, 8 * cw + 2 * d
    n_pad = small0 + 2 * LANES
    w_cat = jnp.concatenate([
        w_in[:, 0:4 * cw], w_in[:, rw_cols:rw_cols + 4 * cw], w_in[:, rw_cols + fox_cols:],
        w_in[:, 4 * cw:rw_cols], _pad_cols(w_in[:, rw_cols + 4 * cw:rw_cols + fox_cols], LANES)],
        axis=1).astype(BF16)
    mu_cat = jnp.concatenate([
        rw_shift_mu[0:4 * cw], jnp.zeros((small0 - 4 * cw,), F32), rw_shift_mu[4 * cw:],
        jnp.zeros((LANES,), F32)]).reshape(1, n_pad)

    h = _rmsnorm(x2d, norm_g, tiles["norm_tm"])
    z = _inproj(h, w_cat, mu_cat, seq, tiles["in_tm"], tiles["in_tn"])
    z3 = z.reshape(bsz, seq, n_pad)
    lora_blk = small0 // (2 * LANES)

    row = lambda a: a.reshape(1, cw)
    wup = jnp.pad(rw_w_lora_up, ((0, lw), (0, 0))).astype(BF16)
    aup = jnp.pad(rw_a_lora_up, ((lw, 0), (0, 0))).astype(BF16)
    y_rw = _rwkv(z3, lora_blk, wup, aup, row(rw_w0), row(rw_a0), row(rw_k_k), row(rw_k_a), row(rw_r_k),
                 row(rw_ln_g), row(rw_ln_b), cw)

    bf_pad = jnp.pad(fox_b_f, (LANES, LANES - n_heads)).reshape(1, 2 * LANES)
    c_sp = _fcum(z3, lora_blk, bf_pad, tiles["fc_tc"])
    y_fox = _fox(z3, c_sp, fox0, cw, tiles["fox_tq"])

    x1 = _merge(y_rw.reshape(m, cw), y_fox.reshape(m, cw), z, gate0, x2d, w_up_rwkv.astype(BF16),
                w_up_fox.astype(BF16), w_out.astype(BF16), tiles["merge_tm"])
    return _ple(x1, p2d, ple_norm_g, final_norm_g, ple_proj.astype(BF16), ple_gate_w.astype(BF16),
                tiles["ple_tm"])


_TILES = dict(norm_tm=512, in_tm=512, in_tn=1792, fc_tc=256, fox_tq=512, merge_tm=256, ple_tm=256)


def kernel(x, p, norm_g, w_in, rw_shift_mu, rw_w0, rw_w_lora_up, rw_a0, rw_a_lora_up, rw_k_k, rw_k_a, rw_r_k,
           rw_ln_g, rw_ln_b, fox_b_f, w_up_rwkv, w_up_fox, w_out, ple_proj, ple_gate_w, ple_norm_g,
           final_norm_g):
    bsz, seq, d = x.shape
    depth = p.shape[0]
    assert depth == 1, "the final norm is fused into the (single) layer"
    out = _layer(x.reshape(bsz * seq, d), p[0].reshape(bsz * seq, -1), seq, norm_g[0], w_in[0], rw_shift_mu[0],
                 rw_w0[0], rw_w_lora_up[0], rw_a0[0], rw_a_lora_up[0], rw_k_k[0], rw_k_a[0],
                 rw_r_k[0].reshape(-1), rw_ln_g[0], rw_ln_b[0], fox_b_f[0], w_up_rwkv[0], w_up_fox[0], w_out[0],
                 ple_proj[0], ple_gate_w[0], ple_norm_g[0], final_norm_g, tiles=_TILES)
    return out.reshape(bsz, seq, d)
```

```python
import functools
import math

import jax
import jax.numpy as jnp
from jax import lax
from jax.experimental import pallas as pl
from jax.experimental.pallas import tpu as pltpu

F32 = jnp.float32
BF16 = jnp.bfloat16

HEAD_DIM = 64
LANES = 128
NORM_EPS = 1e-6
GN_EPS = 64e-5
Q_SCALE = HEAD_DIM ** -0.5
EXP_NEG_HALF = math.exp(-0.5)
LOG2_E = math.log2(math.e)
NEG_BIG = -1e30
VMEM_LIMIT = 56 * 1024 * 1024

SPLIT_TERMS = 2
RW_CHUNK = 64
RW_CHUNKS_PER_STEP = 2
C_TERMS = 3


def _dot(a, b):
    return jnp.dot(a, b, preferred_element_type=F32)


def _dot_nt(a, b):
    return lax.dot_general(a, b, (((1,), (1,)), ((), ())), preferred_element_type=F32)


def _dot_tn(a, b):
    return lax.dot_general(a, b, (((0,), (0,)), ((), ())), preferred_element_type=F32)


def _split_bf16(x, n):
    parts = []
    rem = x
    for i in range(n):
        p = rem.astype(BF16)
        parts.append(p)
        if i + 1 < n:
            rem = rem - p.astype(F32)
    return parts


def _sigmoid(x):
    return 1.0 / (1.0 + jnp.exp(-x))


def _rmsnorm_kernel(x_ref, g_ref, o_ref):
    x = x_ref[...]
    y = x * lax.rsqrt(jnp.mean(x * x, axis=-1, keepdims=True) + NORM_EPS)
    o_ref[...] = (y * g_ref[...]).astype(o_ref.dtype)


def _rmsnorm(x, g, tm):
    m, d = x.shape
    return pl.pallas_call(
        _rmsnorm_kernel,
        grid=(m // tm,),
        in_specs=[pl.BlockSpec((tm, d), lambda i: (i, 0)),
                  pl.BlockSpec((1, d), lambda i: (0, 0))],
        out_specs=pl.BlockSpec((tm, d), lambda i: (i, 0)),
        out_shape=jax.ShapeDtypeStruct((m, d), BF16),
        compiler_params=pltpu.CompilerParams(dimension_semantics=("arbitrary",),
                                             vmem_limit_bytes=VMEM_LIMIT),
        name="rmsnorm",
    )(x, g.reshape(1, d))


def _inproj_kernel(h_ref, w_ref, mu_ref, o_ref, carry_ref, *, tiles_per_seq):
    z = _dot(h_ref[...], w_ref[...])
    tm = z.shape[0]
    first = (pl.program_id(1) % tiles_per_seq) == 0
    prev_last = jnp.where(first, 0.0, carry_ref[7:8, :])
    row = lax.broadcasted_iota(jnp.int32, z.shape, 0)
    z_prev = jnp.where(row == 0, prev_last, pltpu.roll(z, 1, 0))
    o_ref[...] = z + (z_prev - z) * mu_ref[...]
    carry_ref[...] = z[tm - 8:tm, :]


def _inproj(h, w, mu, seq, tm, tn):
    m, d = h.shape
    n = w.shape[1]
    return pl.pallas_call(
        functools.partial(_inproj_kernel, tiles_per_seq=seq // tm),
        grid=(n // tn, m // tm),
        in_specs=[pl.BlockSpec((tm, d), lambda j, i: (i, 0)),
                  pl.BlockSpec((d, tn), lambda j, i: (0, j)),
                  pl.BlockSpec((1, tn), lambda j, i: (0, j))],
        out_specs=pl.BlockSpec((tm, tn), lambda j, i: (i, j)),
        out_shape=jax.ShapeDtypeStruct((m, n), F32),
        scratch_shapes=[pltpu.VMEM((8, tn), F32)],
        compiler_params=pltpu.CompilerParams(dimension_semantics=("arbitrary", "arbitrary"),
                                             vmem_limit_bytes=VMEM_LIMIT),
        name="inproj",
    )(h, w, mu)


def _rwkv_chunk(r, k, v, gate, lora_b, lora_t, wup, aup, w0, a0, kkp, kap, rkp, lng, lnb, s_old, *, c):
    rows, width = r.shape
    n_pairs = width // LANES
    pairs = range(n_pairs)
    subs = range(rows // c)
    blocks = [(i, p) for i in subs for p in pairs]
    cols = lambda x: [x[i * c:(i + 1) * c, p * LANES:(p + 1) * LANES] for i, p in blocks]
    lane = lax.broadcasted_iota(jnp.int32, (1, LANES), 1)
    m0 = lane < HEAD_DIM

    ri = lax.broadcasted_iota(jnp.int32, (LANES, LANES), 0)
    ci = lax.broadcasted_iota(jnp.int32, (LANES, LANES), 1)
    same_head = (ri // HEAD_DIM) == (ci // HEAD_DIM)
    ones_bd = jnp.where(same_head, 1.0, 0.0).astype(BF16)

    def segsum(x):
        xr = jnp.concatenate(cols(x), axis=0)
        sr = sum(_dot(part, ones_bd) for part in _split_bf16(xr, SPLIT_TERMS))
        return jnp.concatenate([jnp.concatenate([sr[(i * n_pairs + p) * c:(i * n_pairs + p + 1) * c] for p in pairs],
                                                axis=1) for i in subs], axis=0)

    w_raw = w0 + _dot(lora_t, wup)
    logw = -EXP_NEG_HALF * _sigmoid(w_raw)
    a = _sigmoid(a0 + _dot(lora_b, aup))
    kkr = k * kkp
    kk = kkr / jnp.maximum(jnp.sqrt(segsum(kkr * kkr)), 1e-12)
    b = kk * a
    k2 = k * (1.0 + (a - 1.0) * kap)

    ti = lax.broadcasted_iota(jnp.int32, (rows, rows), 0)
    si = lax.broadcasted_iota(jnp.int32, (rows, rows), 1)
    tri = jnp.where((si <= ti) & (si // c == ti // c), 1.0, 0.0).astype(BF16)
    g = sum(_dot(tri, part) for part in _split_bf16(logw, SPLIT_TERMS))
    g_ends = [g[(i + 1) * c - 1:(i + 1) * c, :] for i in subs]
    g_last = jnp.concatenate([jnp.broadcast_to(ge, (c, width)) for ge in g_ends], axis=0)
    egd = jnp.exp(g_last - g)
    eng = jnp.exp(-g)
    rg = cols(r * jnp.exp(g))
    kkg = cols(kk * jnp.exp(g - logw))
    kn = cols(k2 * eng)
    bn = cols(b * eng)
    kd = cols(k2 * egd)
    bdec = cols(b * egd)
    vs = cols(v)
    decay_c = [jnp.exp(g_ends[i][:, p * LANES:(p + 1) * LANES]) for i, p in blocks]
    nb = range(len(blocks))

    def stack(x):
        return jnp.concatenate([jnp.where(m0, x, 0.0), jnp.where(m0, 0.0, x)], axis=0).astype(BF16)

    t_in = lax.broadcasted_iota(jnp.int32, (c, LANES), 0)
    s_in = lax.broadcasted_iota(jnp.int32, (c, LANES), 1) % HEAD_DIM
    strict = s_in < t_in
    incl = s_in <= t_in
    eye = jnp.where(s_in == t_in, 1.0, 0.0)

    lhs_kr = [jnp.concatenate([kkg[b], rg[b]], axis=0).astype(BF16) for b in nb]
    rhs_bk = [jnp.concatenate([stack(bn[b]), stack(kn[b])], axis=0) for b in nb]
    aa = [_dot_nt(lhs_kr[b], rhs_bk[b]) for b in nb]
    a_kb = [jnp.where(strict, aa[b][0:c, 0:LANES], 0.0) for b in nb]
    a_kk = [jnp.where(strict, aa[b][0:c, LANES:2 * LANES], 0.0).astype(BF16) for b in nb]
    a_rb = [jnp.where(incl, aa[b][c:2 * c, 0:LANES], 0.0) for b in nb]
    a_rk = [jnp.where(incl, aa[b][c:2 * c, LANES:2 * LANES], 0.0) for b in nb]
    a_r = [jnp.concatenate([a_rk[b], a_rb[b]], axis=1).astype(BF16) for b in nb]
    v_st = [stack(vs[b]) for b in nb]
    kb = [jnp.concatenate([kd[b], bdec[b]], axis=0).astype(BF16) for b in nb]
    av = [_dot(a_kk[b], v_st[b]) for b in nb]

    tinv = [eye - a_kb[b] for b in nb]
    lpow = [_dot(a_kb[b].astype(BF16), stack(a_kb[b])) for b in nb]
    n_sq = int(math.log2(c)) - 1
    for i in range(n_sq):
        last = i + 1 == n_sq
        w_pow = [stack(lpow[b]) for b in nb]
        lhs = [tinv[b].astype(BF16) if last else jnp.concatenate([tinv[b], lpow[b]], axis=0).astype(BF16)
               for b in nb]
        prod = [_dot(lhs[b], w_pow[b]) for b in nb]
        tinv = [tinv[b] + prod[b][0:c] for b in nb]
        if not last:
            lpow = [prod[b][c:2 * c] for b in nb]
    tinv = [t.astype(BF16) for t in tinv]

    s_cur = list(s_old)
    ys = []
    for i in subs:
        bi = [i * n_pairs + p for p in pairs]
        s_b = [s_cur[p].astype(BF16) for p in pairs]
        sh = [_dot_nt(lhs_kr[bi[p]], s_b[p]) for p in pairs]
        x = [sh[p][0:c] + av[bi[p]] for p in pairs]
        u = [_dot(tinv[bi[p]], stack(x[p])) for p in pairs]
        vu_st = [jnp.concatenate([v_st[bi[p]], stack(-u[p])], axis=0) for p in pairs]
        ys.append(jnp.concatenate([sh[p][c:2 * c] + _dot(a_r[bi[p]], vu_st[p]) for p in pairs], axis=1))
        vu = [jnp.concatenate([vs[bi[p]], -u[p]], axis=0).astype(BF16) for p in pairs]
        s_cur = [s_cur[p] * decay_c[bi[p]] + jnp.where(same_head, _dot_tn(vu[p], kb[bi[p]]), 0.0) for p in pairs]
    y = jnp.concatenate(ys, axis=0)
    s_new = s_cur

    inv_n = 1.0 / HEAD_DIM
    mean = segsum(y) * inv_n
    dlt = y - mean
    var = segsum(dlt * dlt) * inv_n
    yn = dlt * lax.rsqrt(var + GN_EPS) * lng + lnb
    bonus = segsum(r * k2 * rkp) * v
    out = (yn + bonus) * (gate * _sigmoid(gate))
    return out, s_new


def _rwkv_kernel(r_ref, k_ref, v_ref, g_ref, lora_ref, wup_ref, aup_ref, w0_ref, a0_ref, kk_ref, ka_ref,
                 rk_ref, lng_ref, lnb_ref, o_ref, s_ref):
    @pl.when(pl.program_id(1) == 0)
    def _():
        s_ref[...] = jnp.zeros_like(s_ref)

    lora = lora_ref[0][:, 0:LANES]
    out, s_new = _rwkv_chunk(
        r_ref[0], k_ref[0], v_ref[0], g_ref[0], lora.astype(BF16), jnp.tanh(lora).astype(BF16),
        wup_ref[...], aup_ref[...], w0_ref[...], a0_ref[...], kk_ref[...], ka_ref[...], rk_ref[...],
        lng_ref[...], lnb_ref[...], [s_ref[p] for p in range(s_ref.shape[0])], c=RW_CHUNK)
    for p in range(s_ref.shape[0]):
        s_ref[p] = s_new[p]
    o_ref[0] = out.astype(o_ref.dtype)


def _rwkv(z3, lora_blk, wup, aup, w0, a0, k_k, k_a, r_k, ln_g, ln_b, width):
    bsz, seq, _ = z3.shape
    c = min(seq, RW_CHUNK * RW_CHUNKS_PER_STEP)
    assert width % LANES == 0
    zspec = lambda j: pl.BlockSpec((1, c, width), lambda b, t, j=j: (b, t, j))
    pspec = lambda rows: pl.BlockSpec((rows, width), lambda b, t: (0, 0))
    return pl.pallas_call(
        _rwkv_kernel,
        grid=(bsz, seq // c),
        in_specs=[zspec(0), zspec(1), zspec(2), zspec(3),
                  pl.BlockSpec((1, c, 2 * LANES), lambda b, t: (b, t, lora_blk)),
                  pspec(LANES), pspec(LANES)] + [pspec(1)] * 7,
        out_specs=pl.BlockSpec((1, c, width), lambda b, t: (b, t, 0)),
        out_shape=jax.ShapeDtypeStruct((bsz, seq, width), BF16),
        scratch_shapes=[pltpu.VMEM((width // LANES, LANES, LANES), F32)],
        compiler_params=pltpu.CompilerParams(dimension_semantics=("arbitrary", "arbitrary"),
                                             vmem_limit_bytes=VMEM_LIMIT),
        name="rwkv7_scan",
    )(z3, z3, z3, z3, z3, wup, aup, w0, a0, k_k, k_a, r_k, ln_g, ln_b)


def _fcum_kernel(l_ref, bf_ref, o_ref, carry_ref):
    @pl.when(pl.program_id(1) == 0)
    def _():
        carry_ref[...] = jnp.zeros_like(carry_ref)

    x = l_ref[0] + bf_ref[...]
    log_f = jnp.minimum(x, 0.0) - jnp.log(1.0 + jnp.exp(-jnp.abs(x)))
    tc = x.shape[0]
    ti = lax.broadcasted_iota(jnp.int32, (tc, tc), 0)
    si = lax.broadcasted_iota(jnp.int32, (tc, tc), 1)
    tril = jnp.where(si <= ti, 1.0, 0.0).astype(BF16)
    c = sum(_dot(tril, part) for part in _split_bf16(log_f, 3)) + carry_ref[0:1, :]
    carry_ref[...] = jnp.broadcast_to(c[tc - 1:tc, :], carry_ref.shape)
    lin = lax.broadcasted_iota(jnp.int32, (2 * LANES, LANES), 0) - LANES
    lout = lax.broadcasted_iota(jnp.int32, (2 * LANES, LANES), 1)
    terms = _split_bf16(-LOG2_E * c, C_TERMS)
    sel = [jnp.where((lin >= 0) & (lout == C_TERMS * lin + i), 1.0, 0.0).astype(BF16) for i in range(C_TERMS)]
    o_ref[0] = sum(_dot(terms[i], sel[i]) for i in range(C_TERMS)).astype(o_ref.dtype)


def _fcum(z3, lora_blk, bf_pad, tc):
    bsz, seq, _ = z3.shape
    return pl.pallas_call(
        _fcum_kernel,
        grid=(bsz, seq // tc),
        in_specs=[pl.BlockSpec((1, tc, 2 * LANES), lambda b, t: (b, t, lora_blk)),
                  pl.BlockSpec((1, 2 * LANES), lambda b, t: (0, 0))],
        out_specs=pl.BlockSpec((1, tc, LANES), lambda b, t: (b, t, 0)),
        out_shape=jax.ShapeDtypeStruct((bsz, seq, LANES), BF16),
        scratch_shapes=[pltpu.VMEM((8, 2 * LANES), F32)],
        compiler_params=pltpu.CompilerParams(dimension_semantics=("arbitrary", "arbitrary"),
                                             vmem_limit_bytes=VMEM_LIMIT),
        name="fox_cumsum",
    )(z3, bf_pad)


def _fox_kernel(q_ref, k_ref, v_ref, g_ref, c_ref, o_ref, vt_ref, m_ref, l_ref, acc_ref, sa_ref, sb_ref, *, tq, tqs,
                t_chunk):
    pair = pl.program_id(1)
    qi = pl.program_id(2)
    seq = k_ref.shape[1]

    @pl.when(qi == 0)
    def _():
        for t0 in range(0, seq, t_chunk):
            vt_ref[:, t0:t0 + t_chunk] = v_ref[0, t0:t0 + t_chunk, :].T.astype(BF16)

    lane = lax.broadcasted_iota(jnp.int32, (1, LANES), 1)
    q = q_ref[0] * (Q_SCALE * LOG2_E)
    q_t = []
    for h in range(2):
        in_head = (lane // HEAD_DIM) == h
        c_lane0 = C_TERMS * (2 * pair + h)
        ones = jnp.where((lane >= c_lane0) & (lane < c_lane0 + C_TERMS), 1.0, 0.0)
        qa = jnp.concatenate([jnp.where(in_head, q, 0.0), jnp.broadcast_to(ones, q.shape)], axis=1)
        q_t.append(qa.T.astype(BF16))
    m_ref[...] = jnp.full_like(m_ref, NEG_BIG)
    l_ref[...] = jnp.zeros_like(l_ref)
    acc_ref[...] = jnp.zeros_like(acc_ref)

    chains = [(h, slice(q0, q0 + tqs)) for h in range(2) for q0 in range(0, tq, tqs)]

    def scores(j, dst_ref):
        ks = pl.multiple_of(j * tq, tq)
        ka = jnp.concatenate([k_ref[0, pl.ds(ks, tq), :].astype(BF16), c_ref[0, pl.ds(ks, tq), :]], axis=1)
        for i, (h, qs) in enumerate(chains):
            dst_ref[i] = _dot(ka, q_t[h][:, qs])

    def update(src_ref, j, masked):
        ks = pl.multiple_of(j * tq, tq)
        nk = [qs.stop if masked else tq for h, qs in chains]
        s = [src_ref[i, 0:nk[i], :] for i in range(len(chains))]
        if masked:
            s = [jnp.where(lax.broadcasted_iota(jnp.int32, (nk[i], tqs), 0)
                           <= lax.broadcasted_iota(jnp.int32, (nk[i], tqs), 1) + qs.start, s[i], NEG_BIG)
                 for i, (h, qs) in enumerate(chains)]
        m_old = [m_ref[h, :, qs] for h, qs in chains]
        m_new = [jnp.maximum(m_old[i], jnp.max(s[i], axis=0, keepdims=True)) for i in range(len(chains))]
        alpha = [jnp.exp2(m_old[i] - m_new[i]) for i in range(len(chains))]
        pexp = [jnp.exp2(s[i] - m_new[i]) for i in range(len(chains))]
        l_new = [alpha[i] * l_ref[h, :, qs] + jnp.sum(pexp[i], axis=0, keepdims=True)
                 for i, (h, qs) in enumerate(chains)]
        pv = [_dot(vt_ref[h * HEAD_DIM:(h + 1) * HEAD_DIM, pl.ds(ks, nk[i])], pexp[i].astype(BF16))
              for i, (h, qs) in enumerate(chains)]
        for i, (h, qs) in enumerate(chains):
            acc_ref[h, :, qs] = alpha[i] * acc_ref[h, :, qs] + pv[i]
            l_ref[h, :, qs] = l_new[i]
            m_ref[h, :, qs] = m_new[i]

    scores(0, sa_ref)

    def body(jj, carry):
        j = 2 * jj
        scores(j + 1, sb_ref)
        update(sa_ref, j, False)
        scores(j + 2, sa_ref)
        update(sb_ref, j + 1, False)
        return carry

    lax.fori_loop(0, qi // 2, body, 0)

    @pl.when(qi % 2 == 0)
    def _():
        update(sa_ref, qi, True)

    @pl.when(qi % 2 == 1)
    def _():
        scores(qi, sb_ref)
        update(sa_ref, qi - 1, False)
        update(sb_ref, qi, True)
    o_t = jnp.concatenate([acc_ref[0] / l_ref[0], acc_ref[1] / l_ref[1]], axis=0)
    gate = g_ref[0]
    o_ref[0] = (o_t.T * (gate * _sigmoid(gate))).astype(o_ref.dtype)


def _fox(z3, c_sp, col0, width, tq):
    bsz, seq, _ = z3.shape
    n_pairs = width // LANES
    blk = lambda j: col0 // LANES + j * n_pairs
    tqs = min(tq, 2 * LANES)
    s_buf = pltpu.VMEM((2 * tq // tqs, tq, tqs), F32)
    return pl.pallas_call(
        functools.partial(_fox_kernel, tq=tq, tqs=tqs, t_chunk=min(seq, 512)),
        grid=(bsz, n_pairs, seq // tq),
        in_specs=[pl.BlockSpec((1, tq, LANES), lambda b, p, i: (b, i, blk(0) + p)),
                  pl.BlockSpec((1, seq, LANES), lambda b, p, i: (b, 0, blk(1) + p)),
                  pl.BlockSpec((1, seq, LANES), lambda b, p, i: (b, 0, blk(2) + p)),
                  pl.BlockSpec((1, tq, LANES), lambda b, p, i: (b, i, blk(3) + p)),
                  pl.BlockSpec((1, seq, LANES), lambda b, p, i: (b, 0, 0))],
        out_specs=pl.BlockSpec((1, tq, LANES), lambda b, p, i: (b, i, p)),
        out_shape=jax.ShapeDtypeStruct((bsz, seq, width), BF16),
        scratch_shapes=[pltpu.VMEM((LANES, seq), BF16),
                        pltpu.VMEM((2, 1, tq), F32), pltpu.VMEM((2, 1, tq), F32),
                        pltpu.VMEM((2, HEAD_DIM, tq), F32), s_buf, s_buf],
        compiler_params=pltpu.CompilerParams(dimension_semantics=("arbitrary",) * 3,
                                             vmem_limit_bytes=VMEM_LIMIT),
        name="fox_attention",
    )(z3, z3, z3, z3, c_sp)


def _merge_kernel(yr_ref, yf_ref, gr_ref, gf_ref, x_ref, wr_ref, wf_ref, wo_ref, o_ref):
    u_rw = _dot(yr_ref[...], wr_ref[...])
    u_fox = _dot(yf_ref[...], wf_ref[...])
    merged = _sigmoid(gr_ref[...]) * u_rw + _sigmoid(gf_ref[...]) * u_fox
    o_ref[...] = x_ref[...] + _dot(merged.astype(BF16), wo_ref[...])


def _merge(y_rw, y_fox, z, gate_col0, x, w_up_rw, w_up_fox, w_out, tm):
    m, d = x.shape
    cw = y_rw.shape[1]
    gblk = gate_col0 // d
    const = lambda shape: pl.BlockSpec(shape, lambda i: (0, 0), pipeline_mode=pl.Buffered(1))
    return pl.pallas_call(
        _merge_kernel,
        grid=(m // tm,),
        in_specs=[pl.BlockSpec((tm, cw), lambda i: (i, 0)),
                  pl.BlockSpec((tm, cw), lambda i: (i, 0)),
                  pl.BlockSpec((tm, d), lambda i: (i, gblk)),
                  pl.BlockSpec((tm, d), lambda i: (i, gblk + 1)),
                  pl.BlockSpec((tm, d), lambda i: (i, 0)),
                  const((cw, d)), const((cw, d)), const((d, d))],
        out_specs=pl.BlockSpec((tm, d), lambda i: (i, 0)),
        out_shape=jax.ShapeDtypeStruct((m, d), F32),
        compiler_params=pltpu.CompilerParams(dimension_semantics=("arbitrary",),
                                             vmem_limit_bytes=VMEM_LIMIT),
        name="merge_outproj",
    )(y_rw, y_fox, z, z, x, w_up_rw, w_up_fox, w_out)


def _ple_kernel(x_ref, p_ref, ng_ref, fg_ref, wp_ref, wg_ref, o_ref):
    x = x_ref[...]
    hn = x * lax.rsqrt(jnp.mean(x * x, axis=-1, keepdims=True) + NORM_EPS) * ng_ref[...]
    gate = _sigmoid(_dot(hn.astype(BF16), wg_ref[...]))
    ple = _dot(p_ref[...].astype(BF16), wp_ref[...])
    x2 = x + ple * gate
    o_ref[...] = x2 * lax.rsqrt(jnp.mean(x2 * x2, axis=-1, keepdims=True) + NORM_EPS) * fg_ref[...]


def _ple(x1, p, ple_norm_g, final_norm_g, ple_proj, ple_gate_w, tm):
    m, d = x1.shape
    pd = p.shape[1]
    const = lambda shape: pl.BlockSpec(shape, lambda i: (0, 0), pipeline_mode=pl.Buffered(1))
    return pl.pallas_call(
        _ple_kernel,
        grid=(m // tm,),
        in_specs=[pl.BlockSpec((tm, d), lambda i: (i, 0)),
                  pl.BlockSpec((tm, pd), lambda i: (i, 0)),
                  const((1, d)), const((1, d)), const((pd, d)), const((d, d))],
        out_specs=pl.BlockSpec((tm, d), lambda i: (i, 0)),
        out_shape=jax.ShapeDtypeStruct((m, d), F32),
        compiler_params=pltpu.CompilerParams(dimension_semantics=("arbitrary",),
                                             vmem_limit_bytes=VMEM_LIMIT),
        name="ple_final_norm",
    )(x1, p, ple_norm_g.reshape(1, d), final_norm_g.reshape(1, d), ple_proj, ple_gate_w)


def _pad_cols(a, n):
    return jnp.pad(a, ((0, 0), (0, n - a.shape[1])))


def _layer(x2d, p2d, seq, norm_g, w_in, rw_shift_mu, rw_w0, rw_w_lora_up, rw_a0, rw_a_lora_up, rw_k_k, rw_k_a,
           rw_r_k, rw_ln_g, rw_ln_b, fox_b_f, w_up_rwkv, w_up_fox, w_out, ple_proj, ple_gate_w, ple_norm_g,
           final_norm_g, *, tiles):
    m, d = x2d.shape
    bsz = m // seq
    cw = w_up_rwkv.shape[0]
    n_heads = fox_b_f.shape[0]
    lw = rw_w_lora_up.shape[0]
    assert 2 * lw == LANES and cw % LANES == 0 and C_TERMS * n_heads <= LANES

    rw_cols = 4 * cw + 2 * lw
    fox_cols = 4 * cw + n_heads
    fox0, gate0, small0 = 4 * cw, 8 * cw, 8 * cw + 2 * d
    n_pad = small0 + 2 * LANES
    w_cat = jnp.concatenate([
        w_in[:, 0:4 * cw], w_in[:, rw_cols:rw_cols + 4 * cw], w_in[:, rw_cols + fox_cols:],
        w_in[:, 4 * cw:rw_cols], _pad_cols(w_in[:, rw_cols + 4 * cw:rw_cols + fox_cols], LANES)],
        axis=1).astype(BF16)
    mu_cat = jnp.concatenate([
        rw_shift_mu[0:4 * cw], jnp.zeros((small0 - 4 * cw,), F32), rw_shift_mu[4 * cw:],
        jnp.zeros((LANES,), F32)]).reshape(1, n_pad)

    h = _rmsnorm(x2d, norm_g, tiles["norm_tm"])
    z = _inproj(h, w_cat, mu_cat, seq, tiles["in_tm"], tiles["in_tn"])
    z3 = z.reshape(bsz, seq, n_pad)
    lora_blk = small0 // (2 * LANES)

    row = lambda a: a.reshape(1, cw)
    wup = jnp.pad(rw_w_lora_up, ((0, lw), (0, 0))).astype(BF16)
    aup = jnp.pad(rw_a_lora_up, ((lw, 0), (0, 0))).astype(BF16)
    y_rw = _rwkv(z3, lora_blk, wup, aup, row(rw_w0), row(rw_a0), row(rw_k_k), row(rw_k_a), row(rw_r_k),
                 row(rw_ln_g), row(rw_ln_b), cw)

    bf_pad = jnp.pad(fox_b_f, (LANES, LANES - n_heads)).reshape(1, 2 * LANES)
    c_sp = _fcum(z3, lora_blk, bf_pad, tiles["fc_tc"])
    y_fox = _fox(z3, c_sp, fox0, cw, tiles["fox_tq"])

    x1 = _merge(y_rw.reshape(m, cw), y_fox.reshape(m, cw), z, gate0, x2d, w_up_rwkv.astype(BF16),
                w_up_fox.astype(BF16), w_out.astype(BF16), tiles["merge_tm"])
    return _ple(x1, p2d, ple_norm_g, final_norm_g, ple_proj.astype(BF16), ple_gate_w.astype(BF16),
                tiles["ple_tm"])


_TILES = dict(norm_tm=512, in_tm=512, in_tn=1792, fc_tc=256, fox_tq=512, merge_tm=256, ple_tm=256)


def kernel(x, p, norm_g, w_in, rw_shift_mu, rw_w0, rw_w_lora_up, rw_a0, rw_a_lora_up, rw_k_k, rw_k_a, rw_r_k,
           rw_ln_g, rw_ln_b, fox_b_f, w_up_rwkv, w_up_fox, w_out, ple_proj, ple_gate_w, ple_norm_g,
           final_norm_g):
    bsz, seq, d = x.shape
    depth = p.shape[0]
    assert depth == 1, "the final norm is fused into the (single) layer"
    out = _layer(x.reshape(bsz * seq, d), p[0].reshape(bsz * seq, -1), seq, norm_g[0], w_in[0], rw_shift_mu[0],
                 rw_w0[0], rw_w_lora_up[0], rw_a0[0], rw_a_lora_up[0], rw_k_k[0], rw_k_a[0],
                 rw_r_k[0].reshape(-1), rw_ln_g[0], rw_ln_b[0], fox_b_f[0], w_up_rwkv[0], w_up_fox[0], w_out[0],
                 ple_proj[0], ple_gate_w[0], ple_norm_g[0], final_norm_g, tiles=_TILES)
    return out.reshape(bsz, seq, d)
```

```python
import functools
import math

import jax
import jax.numpy as jnp
from jax import lax
from jax.experimental import pallas as pl
from jax.experimental.pallas import tpu as pltpu

F32 = jnp.float32
BF16 = jnp.bfloat16

HEAD_DIM = 64
LANES = 128
NORM_EPS = 1e-6
GN_EPS = 64e-5
Q_SCALE = HEAD_DIM ** -0.5
EXP_NEG_HALF = math.exp(-0.5)
LOG2_E = math.log2(math.e)
NEG_BIG = -1e30
VMEM_LIMIT = 56 * 1024 * 1024

SPLIT_TERMS = 2
RW_CHUNK = 64
RW_CHUNKS_PER_STEP = 2
C_TERMS = 3
VT_ROWS = HEAD_DIM + 16


def _dot(a, b):
    return jnp.dot(a, b, preferred_element_type=F32)


def _dot_nt(a, b):
    return lax.dot_general(a, b, (((1,), (1,)), ((), ())), preferred_element_type=F32)


def _dot_tn(a, b):
    return lax.dot_general(a, b, (((0,), (0,)), ((), ())), preferred_element_type=F32)


def _split_bf16(x, n):
    parts = []
    rem = x
    for i in range(n):
        p = rem.astype(BF16)
        parts.append(p)
        if i + 1 < n:
            rem = rem - p.astype(F32)
    return parts


def _sigmoid(x):
    return 1.0 / (1.0 + jnp.exp(-x))


def _inproj_kernel(x_ref, g_ref, w_ref, mu_ref, o_ref, carry_ref, *, tiles_per_seq):
    x = x_ref[...]
    rstd = lax.rsqrt(jnp.mean(x * x, axis=-1, keepdims=True) + NORM_EPS)
    z = _dot((x * g_ref[...]).astype(BF16), w_ref[...]) * rstd
    tm = z.shape[0]
    first = (pl.program_id(1) % tiles_per_seq) == 0
    prev_last = jnp.where(first, 0.0, carry_ref[7:8, :])
    row = lax.broadcasted_iota(jnp.int32, z.shape, 0)
    z_prev = jnp.where(row == 0, prev_last, pltpu.roll(z, 1, 0))
    o_ref[...] = z + (z_prev - z) * mu_ref[...]
    carry_ref[...] = z[tm - 8:tm, :]


def _inproj(x, g, w, mu, seq, tm, tn):
    m, d = x.shape
    n = w.shape[1]
    return pl.pallas_call(
        functools.partial(_inproj_kernel, tiles_per_seq=seq // tm),
        grid=(n // tn, m // tm),
        in_specs=[pl.BlockSpec((tm, d), lambda j, i: (i, 0)),
                  pl.BlockSpec((1, d), lambda j, i: (0, 0)),
                  pl.BlockSpec((d, tn), lambda j, i: (0, j)),
                  pl.BlockSpec((1, tn), lambda j, i: (0, j))],
        out_specs=pl.BlockSpec((tm, tn), lambda j, i: (i, j)),
        out_shape=jax.ShapeDtypeStruct((m, n), F32),
        scratch_shapes=[pltpu.VMEM((8, tn), F32)],
        compiler_params=pltpu.CompilerParams(dimension_semantics=("arbitrary", "arbitrary"),
                                             vmem_limit_bytes=VMEM_LIMIT),
        name="inproj",
    )(x, g.reshape(1, d), w, mu)


def _rwkv_chunk(r, k, v, gate, lora_b, lora_t, wup, aup, w0, a0, kkp, kap, rkp, lng, lnb, s_old, *, c):
    rows, width = r.shape
    n_pairs = width // LANES
    pairs = range(n_pairs)
    subs = range(rows // c)
    blocks = [(i, p) for i in subs for p in pairs]
    cols = lambda x: [x[i * c:(i + 1) * c, p * LANES:(p + 1) * LANES] for i, p in blocks]
    lane = lax.broadcasted_iota(jnp.int32, (1, LANES), 1)
    m0 = lane < HEAD_DIM

    ri = lax.broadcasted_iota(jnp.int32, (LANES, LANES), 0)
    ci = lax.broadcasted_iota(jnp.int32, (LANES, LANES), 1)
    same_head = (ri // HEAD_DIM) == (ci // HEAD_DIM)
    ones_bd = jnp.where(same_head, 1.0, 0.0).astype(BF16)

    def segsum(x):
        xr = jnp.concatenate(cols(x), axis=0)
        sr = sum(_dot(part, ones_bd) for part in _split_bf16(xr, SPLIT_TERMS))
        return jnp.concatenate([jnp.concatenate([sr[(i * n_pairs + p) * c:(i * n_pairs + p + 1) * c] for p in pairs],
                                                axis=1) for i in subs], axis=0)

    w_raw = w0 + _dot(lora_t, wup)
    logw = -EXP_NEG_HALF * _sigmoid(w_raw)
    a = _sigmoid(a0 + _dot(lora_b, aup))
    kkr = k * kkp
    kk = kkr / jnp.maximum(jnp.sqrt(segsum(kkr * kkr)), 1e-12)
    b = kk * a
    k2 = k * (1.0 + (a - 1.0) * kap)

    ti = lax.broadcasted_iota(jnp.int32, (rows, rows), 0)
    si = lax.broadcasted_iota(jnp.int32, (rows, rows), 1)
    tri = jnp.where((si <= ti) & (si // c == ti // c), 1.0, 0.0).astype(BF16)
    g = sum(_dot(tri, part) for part in _split_bf16(logw, SPLIT_TERMS))
    g_ends = [g[(i + 1) * c - 1:(i + 1) * c, :] for i in subs]
    g_last = jnp.concatenate([jnp.broadcast_to(ge, (c, width)) for ge in g_ends], axis=0)
    egd = jnp.exp(g_last - g)
    eng = jnp.exp(-g)
    rg = cols(r * jnp.exp(g))
    kkg = cols(kk * jnp.exp(g - logw))
    kn = cols(k2 * eng)
    bn = cols(b * eng)
    kd = cols(k2 * egd)
    bdec = cols(b * egd)
    vs = cols(v)
    decay_c = [jnp.exp(g_ends[i][:, p * LANES:(p + 1) * LANES]) for i, p in blocks]
    nb = range(len(blocks))

    def stack(x):
        return jnp.concatenate([jnp.where(m0, x, 0.0), jnp.where(m0, 0.0, x)], axis=0).astype(BF16)

    t_in = lax.broadcasted_iota(jnp.int32, (c, LANES), 0)
    s_in = lax.broadcasted_iota(jnp.int32, (c, LANES), 1) % HEAD_DIM
    strict = s_in < t_in
    incl = s_in <= t_in
    eye = jnp.where(s_in == t_in, 1.0, 0.0)

    lhs_kr = [jnp.concatenate([kkg[b], rg[b]], axis=0).astype(BF16) for b in nb]
    rhs_bk = [jnp.concatenate([stack(bn[b]), stack(kn[b])], axis=0) for b in nb]
    aa = [_dot_nt(lhs_kr[b], rhs_bk[b]) for b in nb]
    a_kb = [jnp.where(strict, aa[b][0:c, 0:LANES], 0.0) for b in nb]
    a_kk = [jnp.where(strict, aa[b][0:c, LANES:2 * LANES], 0.0).astype(BF16) for b in nb]
    a_rb = [jnp.where(incl, aa[b][c:2 * c, 0:LANES], 0.0) for b in nb]
    a_rk = [jnp.where(incl, aa[b][c:2 * c, LANES:2 * LANES], 0.0) for b in nb]
    a_r = [jnp.concatenate([a_rk[b], a_rb[b]], axis=1).astype(BF16) for b in nb]
    v_st = [stack(vs[b]) for b in nb]
    kb = [jnp.concatenate([kd[b], bdec[b]], axis=0).astype(BF16) for b in nb]
    av = [_dot(a_kk[b], v_st[b]) for b in nb]

    tinv = [eye - a_kb[b] for b in nb]
    lpow = [_dot(a_kb[b].astype(BF16), stack(a_kb[b])) for b in nb]
    n_sq = int(math.log2(c)) - 1
    for i in range(n_sq):
        last = i + 1 == n_sq
        w_pow = [stack(lpow[b]) for b in nb]
        lhs = [tinv[b].astype(BF16) if last else jnp.concatenate([tinv[b], lpow[b]], axis=0).astype(BF16)
               for b in nb]
        prod = [_dot(lhs[b], w_pow[b]) for b in nb]
        tinv = [tinv[b] + prod[b][0:c] for b in nb]
        if not last:
            lpow = [prod[b][c:2 * c] for b in nb]
    tinv = [t.astype(BF16) for t in tinv]

    s_cur = list(s_old)
    ys = []
    for i in subs:
        bi = [i * n_pairs + p for p in pairs]
        s_b = [s_cur[p].astype(BF16) for p in pairs]
        sh = [_dot_nt(lhs_kr[bi[p]], s_b[p]) for p in pairs]
        x = [sh[p][0:c] + av[bi[p]] for p in pairs]
        u = [_dot(tinv[bi[p]], stack(x[p])) for p in pairs]
        vu_st = [jnp.concatenate([v_st[bi[p]], stack(-u[p])], axis=0) for p in pairs]
        ys.append(jnp.concatenate([sh[p][c:2 * c] + _dot(a_r[bi[p]], vu_st[p]) for p in pairs], axis=1))
        vu = [jnp.concatenate([vs[bi[p]], -u[p]], axis=0).astype(BF16) for p in pairs]
        s_cur = [s_cur[p] * decay_c[bi[p]] + jnp.where(same_head, _dot_tn(vu[p], kb[bi[p]]), 0.0) for p in pairs]
    y = jnp.concatenate(ys, axis=0)
    s_new = s_cur

    inv_n = 1.0 / HEAD_DIM
    mean = segsum(y) * inv_n
    dlt = y - mean
    var = segsum(dlt * dlt) * inv_n
    yn = dlt * lax.rsqrt(var + GN_EPS) * lng + lnb
    bonus = segsum(r * k2 * rkp) * v
    out = (yn + bonus) * (gate * _sigmoid(gate))
    return out, s_new


def _rwkv_kernel(r_ref, k_ref, v_ref, g_ref, lora_ref, wup_ref, aup_ref, w0_ref, a0_ref, kk_ref, ka_ref,
                 rk_ref, lng_ref, lnb_ref, o_ref, s_ref):
    @pl.when(pl.program_id(1) == 0)
    def _():
        s_ref[...] = jnp.zeros_like(s_ref)

    lora = lora_ref[0][:, 0:LANES]
    out, s_new = _rwkv_chunk(
        r_ref[0], k_ref[0], v_ref[0], g_ref[0], lora.astype(BF16), jnp.tanh(lora).astype(BF16),
        wup_ref[...], aup_ref[...], w0_ref[...], a0_ref[...], kk_ref[...], ka_ref[...], rk_ref[...],
        lng_ref[...], lnb_ref[...], [s_ref[p] for p in range(s_ref.shape[0])], c=RW_CHUNK)
    for p in range(s_ref.shape[0]):
        s_ref[p] = s_new[p]
    o_ref[0] = out.astype(o_ref.dtype)


def _rwkv(z3, lora_blk, wup, aup, w0, a0, k_k, k_a, r_k, ln_g, ln_b, width):
    bsz, seq, _ = z3.shape
    c = min(seq, RW_CHUNK * RW_CHUNKS_PER_STEP)
    assert width % LANES == 0
    zspec = lambda j: pl.BlockSpec((1, c, width), lambda b, t, j=j: (b, t, j))
    pspec = lambda rows: pl.BlockSpec((rows, width), lambda b, t: (0, 0))
    return pl.pallas_call(
        _rwkv_kernel,
        grid=(bsz, seq // c),
        in_specs=[zspec(0), zspec(1), zspec(2), zspec(3),
                  pl.BlockSpec((1, c, 2 * LANES), lambda b, t: (b, t, lora_blk)),
                  pspec(LANES), pspec(LANES)] + [pspec(1)] * 7,
        out_specs=pl.BlockSpec((1, c, width), lambda b, t: (b, t, 0)),
        out_shape=jax.ShapeDtypeStruct((bsz, seq, width), BF16),
        scratch_shapes=[pltpu.VMEM((width // LANES, LANES, LANES), F32)],
        compiler_params=pltpu.CompilerParams(dimension_semantics=("arbitrary", "arbitrary"),
                                             vmem_limit_bytes=VMEM_LIMIT),
        name="rwkv7_scan",
    )(z3, z3, z3, z3, z3, wup, aup, w0, a0, k_k, k_a, r_k, ln_g, ln_b)


def _c_lane(pair, h):
    return (1 - h) * HEAD_DIM + C_TERMS * pair


def _fcum_kernel(l_ref, bf_ref, o_ref, carry_ref, *, n_heads):
    @pl.when(pl.program_id(1) == 0)
    def _():
        carry_ref[...] = jnp.zeros_like(carry_ref)

    x = l_ref[0] + bf_ref[...]
    log_f = jnp.minimum(x, 0.0) - jnp.log(1.0 + jnp.exp(-jnp.abs(x)))
    tc = x.shape[0]
    ti = lax.broadcasted_iota(jnp.int32, (tc, tc), 0)
    si = lax.broadcasted_iota(jnp.int32, (tc, tc), 1)
    tril = jnp.where(si <= ti, 1.0, 0.0).astype(BF16)
    c = sum(_dot(tril, part) for part in _split_bf16(log_f, 3)) + carry_ref[0:1, :]
    carry_ref[...] = jnp.broadcast_to(c[tc - 1:tc, :], carry_ref.shape)
    lin = lax.broadcasted_iota(jnp.int32, (LANES, LANES), 0)
    lout = lax.broadcasted_iota(jnp.int32, (LANES, LANES), 1)
    terms = _split_bf16(-LOG2_E * c, C_TERMS)
    sel = [jnp.where((lin < n_heads) & (lout == _c_lane(lin // 2, lin % 2) + i), 1.0, 0.0).astype(BF16)
           for i in range(C_TERMS)]
    o_ref[0] = sum(_dot(terms[i], sel[i]) for i in range(C_TERMS)).astype(o_ref.dtype)


def _fcum(z3, lora_blk, bf_pad, n_heads, tc):
    bsz, seq, _ = z3.shape
    return pl.pallas_call(
        functools.partial(_fcum_kernel, n_heads=n_heads),
        grid=(bsz, seq // tc),
        in_specs=[pl.BlockSpec((1, tc, LANES), lambda b, t: (b, t, 2 * lora_blk + 1)),
                  pl.BlockSpec((1, LANES), lambda b, t: (0, 0))],
        out_specs=pl.BlockSpec((1, tc, LANES), lambda b, t: (b, t, 0)),
        out_shape=jax.ShapeDtypeStruct((bsz, seq, LANES), BF16),
        scratch_shapes=[pltpu.VMEM((8, LANES), F32)],
        compiler_params=pltpu.CompilerParams(dimension_semantics=("arbitrary", "arbitrary"),
                                             vmem_limit_bytes=VMEM_LIMIT),
        name="fox_cumsum",
    )(z3, bf_pad)


def _fox_kernel(q_ref, k_ref, v_ref, g_ref, c_ref, o_ref, vt_ref, m_ref, acc_ref, sa_ref, sb_ref, *, tq, tqs,
                t_chunk):
    pair = pl.program_id(1)
    qi = pl.program_id(2)
    seq = k_ref.shape[1]

    @pl.when(qi == 0)
    def _():
        one_row = jnp.where(lax.broadcasted_iota(jnp.int32, (VT_ROWS - HEAD_DIM, t_chunk), 0) == 0, 1.0, 0.0)
        for t0 in range(0, seq, t_chunk):
            v_t = v_ref[0, t0:t0 + t_chunk, :].T
            for h in range(2):
                vt_ref[h, 0:HEAD_DIM, t0:t0 + t_chunk] = v_t[h * HEAD_DIM:(h + 1) * HEAD_DIM].astype(BF16)
                vt_ref[h, HEAD_DIM:VT_ROWS, t0:t0 + t_chunk] = one_row.astype(BF16)

    lane = lax.broadcasted_iota(jnp.int32, (1, LANES), 1)
    row = lax.broadcasted_iota(jnp.int32, (LANES, 1), 0)
    q_all = (q_ref[0] * (Q_SCALE * LOG2_E)).T
    q_t = []
    for h in range(2):
        r0 = _c_lane(pair, h)
        ones = jnp.where((row >= r0) & (row < r0 + C_TERMS), 1.0, 0.0)
        q_t.append(jnp.where((row // HEAD_DIM) == h, q_all, ones).astype(BF16))
    m_ref[...] = jnp.full_like(m_ref, NEG_BIG)
    acc_ref[...] = jnp.zeros_like(acc_ref)

    chains = [(h, slice(q0, q0 + tqs)) for h in range(2) for q0 in range(0, tq, tqs)]

    def scores(j, dst_ref):
        ks = pl.multiple_of(j * tq, tq)
        kb = k_ref[0, pl.ds(ks, tq), :].astype(BF16)
        cb = c_ref[0, pl.ds(ks, tq), :]
        ka = [jnp.where((lane // HEAD_DIM) == h, kb, cb) for h in range(2)]
        for i, (h, qs) in enumerate(chains):
            dst_ref[i] = _dot(ka[h], q_t[h][:, qs])

    def update(src_ref, j, masked):
        ks = pl.multiple_of(j * tq, tq)
        nk = [qs.stop if masked else tq for h, qs in chains]
        s = [src_ref[i, 0:nk[i], :] for i in range(len(chains))]
        if masked:
            s = [jnp.where(lax.broadcasted_iota(jnp.int32, (nk[i], tqs), 0)
                           <= lax.broadcasted_iota(jnp.int32, (nk[i], tqs), 1) + qs.start, s[i], NEG_BIG)
                 for i, (h, qs) in enumerate(chains)]
        m_old = [m_ref[h, :, qs] for h, qs in chains]
        m_new = [jnp.maximum(m_old[i], jnp.max(s[i], axis=0, keepdims=True)) for i in range(len(chains))]
        alpha = [jnp.exp2(m_old[i] - m_new[i]) for i in range(len(chains))]
        pexp = [jnp.exp2(s[i] - m_new[i]).astype(BF16) for i in range(len(chains))]
        pv = [_dot(vt_ref[h, :, pl.ds(ks, nk[i])], pexp[i]) for i, (h, qs) in enumerate(chains)]
        for i, (h, qs) in enumerate(chains):
            acc_ref[h, :, qs] = alpha[i] * acc_ref[h, :, qs] + pv[i]
            m_ref[h, :, qs] = m_new[i]

    scores(0, sa_ref)

    def body(jj, carry):
        j = 2 * jj
        scores(j + 1, sb_ref)
        update(sa_ref, j, False)
        scores(j + 2, sa_ref)
        update(sb_ref, j + 1, False)
        return carry

    lax.fori_loop(0, qi // 2, body, 0)

    @pl.when(qi % 2 == 0)
    def _():
        update(sa_ref, qi, True)

    @pl.when(qi % 2 == 1)
    def _():
        scores(qi, sb_ref)
        update(sa_ref, qi - 1, False)
        update(sb_ref, qi, True)
    o_t = jnp.concatenate([acc_ref[h, 0:HEAD_DIM, :] / acc_ref[h, HEAD_DIM:HEAD_DIM + 1, :] for h in range(2)],
                          axis=0)
    gate = g_ref[0]
    o_ref[0] = (o_t.T * (gate * _sigmoid(gate))).astype(o_ref.dtype)


def _fox(z3, c_sp, col0, width, tq):
    bsz, seq, _ = z3.shape
    n_pairs = width // LANES
    blk = lambda j: col0 // LANES + j * n_pairs
    tqs = min(tq, 2 * LANES)
    s_buf = pltpu.VMEM((2 * tq // tqs, tq, tqs), F32)
    return pl.pallas_call(
        functools.partial(_fox_kernel, tq=tq, tqs=tqs, t_chunk=min(seq, 512)),
        grid=(bsz, n_pairs, seq // tq),
        in_specs=[pl.BlockSpec((1, tq, LANES), lambda b, p, i: (b, i, blk(0) + p)),
                  pl.BlockSpec((1, seq, LANES), lambda b, p, i: (b, 0, blk(1) + p)),
                  pl.BlockSpec((1, seq, LANES), lambda b, p, i: (b, 0, blk(2) + p)),
                  pl.BlockSpec((1, tq, LANES), lambda b, p, i: (b, i, blk(3) + p)),
                  pl.BlockSpec((1, seq, LANES), lambda b, p, i: (b, 0, 0))],
        out_specs=pl.BlockSpec((1, tq, LANES), lambda b, p, i: (b, i, p)),
        out_shape=jax.ShapeDtypeStruct((bsz, seq, width), BF16),
        scratch_shapes=[pltpu.VMEM((2, VT_ROWS, seq), BF16), pltpu.VMEM((2, 1, tq), F32),
                        pltpu.VMEM((2, VT_ROWS, tq), F32), s_buf, s_buf],
        compiler_params=pltpu.CompilerParams(dimension_semantics=("arbitrary",) * 3,
                                             vmem_limit_bytes=VMEM_LIMIT),
        name="fox_attention",
    )(z3, z3, z3, z3, c_sp)


def _merge_kernel(yr_ref, yf_ref, gr_ref, gf_ref, x_ref, wr_ref, wf_ref, wo_ref, o_ref):
    u_rw = _dot(yr_ref[...], wr_ref[...])
    u_fox = _dot(yf_ref[...], wf_ref[...])
    merged = _sigmoid(gr_ref[...]) * u_rw + _sigmoid(gf_ref[...]) * u_fox
    o_ref[...] = x_ref[...] + _dot(merged.astype(BF16), wo_ref[...])


def _merge(y_rw, y_fox, z, gate_col0, x, w_up_rw, w_up_fox, w_out, tm):
    m, d = x.shape
    cw = y_rw.shape[1]
    gblk = gate_col0 // d
    const = lambda shape: pl.BlockSpec(shape, lambda i: (0, 0), pipeline_mode=pl.Buffered(1))
    return pl.pallas_call(
        _merge_kernel,
        grid=(m // tm,),
        in_specs=[pl.BlockSpec((tm, cw), lambda i: (i, 0)),
                  pl.BlockSpec((tm, cw), lambda i: (i, 0)),
                  pl.BlockSpec((tm, d), lambda i: (i, gblk)),
                  pl.BlockSpec((tm, d), lambda i: (i, gblk + 1)),
                  pl.BlockSpec((tm, d), lambda i: (i, 0)),
                  const((cw, d)), const((cw, d)), const((d, d))],
        out_specs=pl.BlockSpec((tm, d), lambda i: (i, 0)),
        out_shape=jax.ShapeDtypeStruct((m, d), F32),
        compiler_params=pltpu.CompilerParams(dimension_semantics=("arbitrary",),
                                             vmem_limit_bytes=VMEM_LIMIT),
        name="merge_outproj",
    )(y_rw, y_fox, z, z, x, w_up_rw, w_up_fox, w_out)


def _ple_kernel(x_ref, p_ref, ng_ref, fg_ref, wp_ref, wg_ref, o_ref):
    x = x_ref[...]
    hn = x * lax.rsqrt(jnp.mean(x * x, axis=-1, keepdims=True) + NORM_EPS) * ng_ref[...]
    gate = _sigmoid(_dot(hn.astype(BF16), wg_ref[...]))
    ple = _dot(p_ref[...].astype(BF16), wp_ref[...])
    x2 = x + ple * gate
    o_ref[...] = x2 * lax.rsqrt(jnp.mean(x2 * x2, axis=-1, keepdims=True) + NORM_EPS) * fg_ref[...]


def _ple(x1, p, ple_norm_g, final_norm_g, ple_proj, ple_gate_w, tm):
    m, d = x1.shape
    pd = p.shape[1]
    const = lambda shape: pl.BlockSpec(shape, lambda i: (0, 0), pipeline_mode=pl.Buffered(1))
    return pl.pallas_call(
        _ple_kernel,
        grid=(m // tm,),
        in_specs=[pl.BlockSpec((tm, d), lambda i: (i, 0)),
                  pl.BlockSpec((tm, pd), lambda i: (i, 0)),
                  const((1, d)), const((1, d)), const((pd, d)), const((d, d))],
        out_specs=pl.BlockSpec((tm, d), lambda i: (i, 0)),
        out_shape=jax.ShapeDtypeStruct((m, d), F32),
        compiler_params=pltpu.CompilerParams(dimension_semantics=("arbitrary",),
                                             vmem_limit_bytes=VMEM_LIMIT),
        name="ple_final_norm",
    )(x1, p, ple_norm_g.reshape(1, d), final_norm_g.reshape(1, d), ple_proj, ple_gate_w)


def _pad_cols(a, n):
    return jnp.pad(a, ((0, 0), (0, n - a.shape[1])))


def _layer(x2d, p2d, seq, norm_g, w_in, rw_shift_mu, rw_w0, rw_w_lora_up, rw_a0, rw_a_lora_up, rw_k_k, rw_k_a,
           rw_r_k, rw_ln_g, rw_ln_b, fox_b_f, w_up_rwkv, w_up_fox, w_out, ple_proj, ple_gate_w, ple_norm_g,
           final_norm_g, *, tiles):
    m, d = x2d.shape
    bsz = m // seq
    cw = w_up_rwkv.shape[0]
    n_heads = fox_b_f.shape[0]
    lw = rw_w_lora_up.shape[0]
    assert 2 * lw == LANES and cw % LANES == 0 and C_TERMS * (cw // LANES) <= HEAD_DIM

    rw_cols = 4 * cw + 2 * lw
    fox_cols = 4 * cw + n_heads
    fox0, gate0, small0 = 4 * cw, 8 * cw, 8 * cw + 2 * d
    n_pad = small0 + 2 * LANES
    w_cat = jnp.concatenate([
        w_in[:, 0:4 * cw], w_in[:, rw_cols:rw_cols + 4 * cw], w_in[:, rw_cols + fox_cols:],
        w_in[:, 4 * cw:rw_cols], _pad_cols(w_in[:, rw_cols + 4 * cw:rw_cols + fox_cols], LANES)],
        axis=1).astype(BF16)
    mu_cat = jnp.concatenate([
        rw_shift_mu[0:4 * cw], jnp.zeros((small0 - 4 * cw,), F32), rw_shift_mu[4 * cw:],
        jnp.zeros((LANES,), F32)]).reshape(1, n_pad)

    z = _inproj(x2d, norm_g, w_cat, mu_cat, seq, tiles["in_tm"], tiles["in_tn"])
    z3 = z.reshape(bsz, seq, n_pad)
    lora_blk = small0 // (2 * LANES)

    row = lambda a: a.reshape(1, cw)
    wup = jnp.pad(rw_w_lora_up, ((0, lw), (0, 0))).astype(BF16)
    aup = jnp.pad(rw_a_lora_up, ((lw, 0), (0, 0))).astype(BF16)
    y_rw = _rwkv(z3, lora_blk, wup, aup, row(rw_w0), row(rw_a0), row(rw_k_k), row(rw_k_a), row(rw_r_k),
                 row(rw_ln_g), row(rw_ln_b), cw)

    bf_pad = jnp.pad(fox_b_f, (0, LANES - n_heads)).reshape(1, LANES)
    c_sp = _fcum(z3, lora_blk, bf_pad, n_heads, tiles["fc_tc"])
    y_fox = _fox(z3, c_sp, fox0, cw, tiles["fox_tq"])

    x1 = _merge(y_rw.reshape(m, cw), y_fox.reshape(m, cw), z, gate0, x2d, w_up_rwkv.astype(BF16),
                w_up_fox.astype(BF16), w_out.astype(BF16), tiles["merge_tm"])
    return _ple(x1, p2d, ple_norm_g, final_norm_g, ple_proj.astype(BF16), ple_gate_w.astype(BF16),
                tiles["ple_tm"])


_TILES = dict(in_tm=512, in_tn=1792, fc_tc=256, fox_tq=512, merge_tm=256, ple_tm=256)


def kernel(x, p, norm_g, w_in, rw_shift_mu, rw_w0, rw_w_lora_up, rw_a0, rw_a_lora_up, rw_k_k, rw_k_a, rw_r_k,
           rw_ln_g, rw_ln_b, fox_b_f, w_up_rwkv, w_up_fox, w_out, ple_proj, ple_gate_w, ple_norm_g,
           final_norm_g):
    bsz, seq, d = x.shape
    depth = p.shape[0]
    assert depth == 1, "the final norm is fused into the (single) layer"
    out = _layer(x.reshape(bsz * seq, d), p[0].reshape(bsz * seq, -1), seq, norm_g[0], w_in[0], rw_shift_mu[0],
                 rw_w0[0], rw_w_lora_up[0], rw_a0[0], rw_a_lora_up[0], rw_k_k[0], rw_k_a[0],
                 rw_r_k[0].reshape(-1), rw_ln_g[0], rw_ln_b[0], fox_b_f[0], w_up_rwkv[0], w_up_fox[0], w_out[0],
                 ple_proj[0], ple_gate_w[0], ple_norm_g[0], final_norm_g, tiles=_TILES)
    return out.reshape(bsz, seq, d)
```

```python
import functools
import math

import jax
import jax.numpy as jnp
from jax import lax
from jax.experimental import pallas as pl
from jax.experimental.pallas import tpu as pltpu

F32 = jnp.float32
BF16 = jnp.bfloat16

HEAD_DIM = 64
LANES = 128
NORM_EPS = 1e-6
GN_EPS = 64e-5
Q_SCALE = HEAD_DIM ** -0.5
EXP_NEG_HALF = math.exp(-0.5)
LOG2_E = math.log2(math.e)
NEG_BIG = -1e30
VMEM_LIMIT = 56 * 1024 * 1024

SPLIT_TERMS = 2
RW_CHUNK = 64
RW_CHUNKS_PER_STEP = 4
C_TERMS = 3
VT_ROWS = HEAD_DIM + 16


def _dot(a, b):
    return jnp.dot(a, b, preferred_element_type=F32)


def _dot_nt(a, b):
    return lax.dot_general(a, b, (((1,), (1,)), ((), ())), preferred_element_type=F32)


def _dot_tn(a, b):
    return lax.dot_general(a, b, (((0,), (0,)), ((), ())), preferred_element_type=F32)


def _split_bf16(x, n):
    parts = []
    rem = x
    for i in range(n):
        p = rem.astype(BF16)
        parts.append(p)
        if i + 1 < n:
            rem = rem - p.astype(F32)
    return parts


def _sigmoid(x):
    return 1.0 / (1.0 + jnp.exp(-x))


def _inproj_kernel(x_ref, g_ref, w_ref, mu_ref, o_ref, carry_ref, *, tiles_per_seq):
    x = x_ref[...]
    rstd = lax.rsqrt(jnp.mean(x * x, axis=-1, keepdims=True) + NORM_EPS)
    z = _dot((x * g_ref[...]).astype(BF16), w_ref[...]) * rstd
    tm = z.shape[0]
    first = (pl.program_id(1) % tiles_per_seq) == 0
    prev_last = jnp.where(first, 0.0, carry_ref[7:8, :])
    row = lax.broadcasted_iota(jnp.int32, z.shape, 0)
    z_prev = jnp.where(row == 0, prev_last, pltpu.roll(z, 1, 0))
    o_ref[...] = z + (z_prev - z) * mu_ref[...]
    carry_ref[...] = z[tm - 8:tm, :]


def _inproj(x, g, w, mu, seq, tm, tn):
    m, d = x.shape
    n = w.shape[1]
    return pl.pallas_call(
        functools.partial(_inproj_kernel, tiles_per_seq=seq // tm),
        grid=(n // tn, m // tm),
        in_specs=[pl.BlockSpec((tm, d), lambda j, i: (i, 0)),
                  pl.BlockSpec((1, d), lambda j, i: (0, 0)),
                  pl.BlockSpec((d, tn), lambda j, i: (0, j)),
                  pl.BlockSpec((1, tn), lambda j, i: (0, j))],
        out_specs=pl.BlockSpec((tm, tn), lambda j, i: (i, j)),
        out_shape=jax.ShapeDtypeStruct((m, n), F32),
        scratch_shapes=[pltpu.VMEM((8, tn), F32)],
        compiler_params=pltpu.CompilerParams(dimension_semantics=("arbitrary", "arbitrary"),
                                             vmem_limit_bytes=VMEM_LIMIT),
        name="inproj",
    )(x, g.reshape(1, d), w, mu)


def _rwkv_chunk(r, k, v, gate, lora_b, lora_t, wup, aup, w0, a0, kkp, kap, rkp, lng, lnb, s_old, *, c):
    rows, width = r.shape
    n_pairs = width // LANES
    pairs = range(n_pairs)
    subs = range(rows // c)
    blocks = [(i, p) for i in subs for p in pairs]
    cols = lambda x: [x[i * c:(i + 1) * c, p * LANES:(p + 1) * LANES] for i, p in blocks]
    lane = lax.broadcasted_iota(jnp.int32, (1, LANES), 1)
    m0 = lane < HEAD_DIM

    ri = lax.broadcasted_iota(jnp.int32, (LANES, LANES), 0)
    ci = lax.broadcasted_iota(jnp.int32, (LANES, LANES), 1)
    same_head = (ri // HEAD_DIM) == (ci // HEAD_DIM)
    ones_bd = jnp.where(same_head, 1.0, 0.0).astype(BF16)

    def segsum(x):
        xr = jnp.concatenate(cols(x), axis=0)
        sr = sum(_dot(part, ones_bd) for part in _split_bf16(xr, SPLIT_TERMS))
        return jnp.concatenate([jnp.concatenate([sr[(i * n_pairs + p) * c:(i * n_pairs + p + 1) * c] for p in pairs],
                                                axis=1) for i in subs], axis=0)

    w_raw = w0 + _dot(lora_t, wup)
    logw = -EXP_NEG_HALF * _sigmoid(w_raw)
    a = _sigmoid(a0 + _dot(lora_b, aup))
    kkr = k * kkp
    kk = kkr / jnp.maximum(jnp.sqrt(segsum(kkr * kkr)), 1e-12)
    b = kk * a
    k2 = k * (1.0 + (a - 1.0) * kap)

    ti = lax.broadcasted_iota(jnp.int32, (rows, rows), 0)
    si = lax.broadcasted_iota(jnp.int32, (rows, rows), 1)
    tri = jnp.where((si <= ti) & (si // c == ti // c), 1.0, 0.0).astype(BF16)
    g = sum(_dot(tri, part) for part in _split_bf16(logw, SPLIT_TERMS))
    g_ends = [g[(i + 1) * c - 1:(i + 1) * c, :] for i in subs]
    g_last = jnp.concatenate([jnp.broadcast_to(ge, (c, width)) for ge in g_ends], axis=0)
    egd = jnp.exp(g_last - g)
    eng = jnp.exp(-g)
    rg = cols(r * jnp.exp(g))
    kkg = cols(kk * jnp.exp(g - logw))
    kn = cols(k2 * eng)
    bn = cols(b * eng)
    kd = cols(k2 * egd)
    bdec = cols(b * egd)
    vs = cols(v)
    decay_c = [jnp.exp(g_ends[i][:, p * LANES:(p + 1) * LANES]) for i, p in blocks]
    nb = range(len(blocks))

    def stack(x):
        return jnp.concatenate([jnp.where(m0, x, 0.0), jnp.where(m0, 0.0, x)], axis=0).astype(BF16)

    t_in = lax.broadcasted_iota(jnp.int32, (c, LANES), 0)
    s_in = lax.broadcasted_iota(jnp.int32, (c, LANES), 1) % HEAD_DIM
    strict = s_in < t_in
    incl = s_in <= t_in
    eye = jnp.where(s_in == t_in, 1.0, 0.0)

    lhs_kr = [jnp.concatenate([kkg[b], rg[b]], axis=0).astype(BF16) for b in nb]
    rhs_bk = [jnp.concatenate([stack(bn[b]), stack(kn[b])], axis=0) for b in nb]
    aa = [_dot_nt(lhs_kr[b], rhs_bk[b]) for b in nb]
    a_kb = [jnp.where(strict, aa[b][0:c, 0:LANES], 0.0) for b in nb]
    a_kk = [jnp.where(strict, aa[b][0:c, LANES:2 * LANES], 0.0).astype(BF16) for b in nb]
    a_rb = [jnp.where(incl, aa[b][c:2 * c, 0:LANES], 0.0) for b in nb]
    a_rk = [jnp.where(incl, aa[b][c:2 * c, LANES:2 * LANES], 0.0) for b in nb]
    a_r = [jnp.concatenate([a_rk[b], a_rb[b]], axis=1).astype(BF16) for b in nb]
    v_st = [stack(vs[b]) for b in nb]
    kb = [jnp.concatenate([kd[b], bdec[b]], axis=0).astype(BF16) for b in nb]
    av = [_dot(a_kk[b], v_st[b]) for b in nb]

    tinv = [eye - a_kb[b] for b in nb]
    lpow = [_dot(a_kb[b].astype(BF16), stack(a_kb[b])) for b in nb]
    n_sq = int(math.log2(c)) - 1
    for i in range(n_sq):
        last = i + 1 == n_sq
        w_pow = [stack(lpow[b]) for b in nb]
        lhs = [tinv[b].astype(BF16) if last else jnp.concatenate([tinv[b], lpow[b]], axis=0).astype(BF16)
               for b in nb]
        prod = [_dot(lhs[b], w_pow[b]) for b in nb]
        tinv = [tinv[b] + prod[b][0:c] for b in nb]
        if not last:
            lpow = [prod[b][c:2 * c] for b in nb]
    tinv = [t.astype(BF16) for t in tinv]

    s_cur = list(s_old)
    ys = []
    for i in subs:
        bi = [i * n_pairs + p for p in pairs]
        s_b = [s_cur[p].astype(BF16) for p in pairs]
        sh = [_dot_nt(lhs_kr[bi[p]], s_b[p]) for p in pairs]
        x = [sh[p][0:c] + av[bi[p]] for p in pairs]
        u = [_dot(tinv[bi[p]], stack(x[p])) for p in pairs]
        vu_st = [jnp.concatenate([v_st[bi[p]], stack(-u[p])], axis=0) for p in pairs]
        ys.append(jnp.concatenate([sh[p][c:2 * c] + _dot(a_r[bi[p]], vu_st[p]) for p in pairs], axis=1))
        vu = [jnp.concatenate([vs[bi[p]], -u[p]], axis=0).astype(BF16) for p in pairs]
        s_cur = [s_cur[p] * decay_c[bi[p]] + jnp.where(same_head, _dot_tn(vu[p], kb[bi[p]]), 0.0) for p in pairs]
    y = jnp.concatenate(ys, axis=0)
    s_new = s_cur

    inv_n = 1.0 / HEAD_DIM
    mean = segsum(y) * inv_n
    dlt = y - mean
    var = segsum(dlt * dlt) * inv_n
    yn = dlt * lax.rsqrt(var + GN_EPS) * lng + lnb
    bonus = segsum(r * k2 * rkp) * v
    out = (yn + bonus) * (gate * _sigmoid(gate))
    return out, s_new


def _rwkv_kernel(r_ref, k_ref, v_ref, g_ref, lora_ref, wup_ref, aup_ref, w0_ref, a0_ref, kk_ref, ka_ref,
                 rk_ref, lng_ref, lnb_ref, o_ref, s_ref):
    @pl.when(pl.program_id(1) == 0)
    def _():
        s_ref[...] = jnp.zeros_like(s_ref)

    lora = lora_ref[0][:, 0:LANES]
    out, s_new = _rwkv_chunk(
        r_ref[0], k_ref[0], v_ref[0], g_ref[0], lora.astype(BF16), jnp.tanh(lora).astype(BF16),
        wup_ref[...], aup_ref[...], w0_ref[...], a0_ref[...], kk_ref[...], ka_ref[...], rk_ref[...],
        lng_ref[...], lnb_ref[...], [s_ref[p] for p in range(s_ref.shape[0])], c=RW_CHUNK)
    for p in range(s_ref.shape[0]):
        s_ref[p] = s_new[p]
    o_ref[0] = out.astype(o_ref.dtype)


def _rwkv(z3, lora_blk, wup, aup, w0, a0, k_k, k_a, r_k, ln_g, ln_b, width):
    bsz, seq, _ = z3.shape
    c = min(seq, RW_CHUNK * RW_CHUNKS_PER_STEP)
    assert width % LANES == 0
    zspec = lambda j: pl.BlockSpec((1, c, width), lambda b, t, j=j: (b, t, j))
    pspec = lambda rows: pl.BlockSpec((rows, width), lambda b, t: (0, 0))
    return pl.pallas_call(
        _rwkv_kernel,
        grid=(bsz, seq // c),
        in_specs=[zspec(0), zspec(1), zspec(2), zspec(3),
                  pl.BlockSpec((1, c, 2 * LANES), lambda b, t: (b, t, lora_blk)),
                  pspec(LANES), pspec(LANES)] + [pspec(1)] * 7,
        out_specs=pl.BlockSpec((1, c, width), lambda b, t: (b, t, 0)),
        out_shape=jax.ShapeDtypeStruct((bsz, seq, width), BF16),
        scratch_shapes=[pltpu.VMEM((width // LANES, LANES, LANES), F32)],
        compiler_params=pltpu.CompilerParams(dimension_semantics=("arbitrary", "arbitrary"),
                                             vmem_limit_bytes=VMEM_LIMIT),
        name="rwkv7_scan",
    )(z3, z3, z3, z3, z3, wup, aup, w0, a0, k_k, k_a, r_k, ln_g, ln_b)


def _c_lane(pair, h):
    return (1 - h) * HEAD_DIM + C_TERMS * pair


def _fcum_kernel(l_ref, bf_ref, o_ref, carry_ref, *, n_heads):
    @pl.when(pl.program_id(1) == 0)
    def _():
        carry_ref[...] = jnp.zeros_like(carry_ref)

    x = l_ref[0] + bf_ref[...]
    log_f = jnp.minimum(x, 0.0) - jnp.log(1.0 + jnp.exp(-jnp.abs(x)))
    tc = x.shape[0]
    ti = lax.broadcasted_iota(jnp.int32, (tc, tc), 0)
    si = lax.broadcasted_iota(jnp.int32, (tc, tc), 1)
    tril = jnp.where(si <= ti, 1.0, 0.0).astype(BF16)
    c = sum(_dot(tril, part) for part in _split_bf16(log_f, 3)) + carry_ref[0:1, :]
    carry_ref[...] = jnp.broadcast_to(c[tc - 1:tc, :], carry_ref.shape)
    lin = lax.broadcasted_iota(jnp.int32, (LANES, LANES), 0)
    lout = lax.broadcasted_iota(jnp.int32, (LANES, LANES), 1)
    terms = _split_bf16(-LOG2_E * c, C_TERMS)
    sel = [jnp.where((lin < n_heads) & (lout == _c_lane(lin // 2, lin % 2) + i), 1.0, 0.0).astype(BF16)
           for i in range(C_TERMS)]
    o_ref[0] = sum(_dot(terms[i], sel[i]) for i in range(C_TERMS)).astype(o_ref.dtype)


def _fcum(z3, lora_blk, bf_pad, n_heads, tc):
    bsz, seq, _ = z3.shape
    return pl.pallas_call(
        functools.partial(_fcum_kernel, n_heads=n_heads),
        grid=(bsz, seq // tc),
        in_specs=[pl.BlockSpec((1, tc, LANES), lambda b, t: (b, t, 2 * lora_blk + 1)),
                  pl.BlockSpec((1, LANES), lambda b, t: (0, 0))],
        out_specs=pl.BlockSpec((1, tc, LANES), lambda b, t: (b, t, 0)),
        out_shape=jax.ShapeDtypeStruct((bsz, seq, LANES), BF16),
        scratch_shapes=[pltpu.VMEM((8, LANES), F32)],
        compiler_params=pltpu.CompilerParams(dimension_semantics=("arbitrary", "arbitrary"),
                                             vmem_limit_bytes=VMEM_LIMIT),
        name="fox_cumsum",
    )(z3, bf_pad)


def _fox_kernel(q_ref, k_ref, v_ref, g_ref, c_ref, o_ref, vt_ref, m_ref, acc_ref, sa_ref, sb_ref, *, tq, tqs,
                t_chunk):
    pair = pl.program_id(1)
    qi = pl.program_id(2)
    seq = k_ref.shape[1]

    @pl.when(qi == 0)
    def _():
        one_row = jnp.where(lax.broadcasted_iota(jnp.int32, (VT_ROWS - HEAD_DIM, t_chunk), 0) == 0, 1.0, 0.0)
        for t0 in range(0, seq, t_chunk):
            v_t = v_ref[0, t0:t0 + t_chunk, :].T
            for h in range(2):
                vt_ref[h, 0:HEAD_DIM, t0:t0 + t_chunk] = v_t[h * HEAD_DIM:(h + 1) * HEAD_DIM].astype(BF16)
                vt_ref[h, HEAD_DIM:VT_ROWS, t0:t0 + t_chunk] = one_row.astype(BF16)

    lane = lax.broadcasted_iota(jnp.int32, (1, LANES), 1)
    row = lax.broadcasted_iota(jnp.int32, (LANES, 1), 0)
    q_all = (q_ref[0] * (Q_SCALE * LOG2_E)).T
    q_t = []
    for h in range(2):
        r0 = _c_lane(pair, h)
        ones = jnp.where((row >= r0) & (row < r0 + C_TERMS), 1.0, 0.0)
        q_t.append(jnp.where((row // HEAD_DIM) == h, q_all, ones).astype(BF16))
    m_ref[...] = jnp.full_like(m_ref, NEG_BIG)
    acc_ref[...] = jnp.zeros_like(acc_ref)

    chains = [(h, slice(q0, q0 + tqs)) for h in range(2) for q0 in range(0, tq, tqs)]

    def scores(j, dst_ref):
        ks = pl.multiple_of(j * tq, tq)
        kb = k_ref[0, pl.ds(ks, tq), :].astype(BF16)
        cb = c_ref[0, pl.ds(ks, tq), :]
        ka = [jnp.where((lane // HEAD_DIM) == h, kb, cb) for h in range(2)]
        for i, (h, qs) in enumerate(chains):
            dst_ref[i] = _dot(ka[h], q_t[h][:, qs])

    def update(src_ref, j, masked):
        ks = pl.multiple_of(j * tq, tq)
        nk = [qs.stop if masked else tq for h, qs in chains]
        s = [src_ref[i, 0:nk[i], :] for i in range(len(chains))]
        if masked:
            s = [jnp.where(lax.broadcasted_iota(jnp.int32, (nk[i], tqs), 0)
                           <= lax.broadcasted_iota(jnp.int32, (nk[i], tqs), 1) + qs.start, s[i], NEG_BIG)
                 for i, (h, qs) in enumerate(chains)]
        m_old = [m_ref[h, :, qs] for h, qs in chains]
        m_new = [jnp.maximum(m_old[i], jnp.max(s[i], axis=0, keepdims=True)) for i in range(len(chains))]
        alpha = [jnp.exp2(m_old[i] - m_new[i]) for i in range(len(chains))]
        pexp = [jnp.exp2(s[i] - m_new[i]).astype(BF16) for i in range(len(chains))]
        pv = [_dot(vt_ref[h, :, pl.ds(ks, nk[i])], pexp[i]) for i, (h, qs) in enumerate(chains)]
        for i, (h, qs) in enumerate(chains):
            acc_ref[h, :, qs] = alpha[i] * acc_ref[h, :, qs] + pv[i]
            m_ref[h, :, qs] = m_new[i]

    scores(0, sa_ref)

    def body(jj, carry):
        j = 2 * jj
        scores(j + 1, sb_ref)
        update(sa_ref, j, False)
        scores(j + 2, sa_ref)
        update(sb_ref, j + 1, False)
        return carry

    lax.fori_loop(0, qi // 2, body, 0)

    @pl.when(qi % 2 == 0)
    def _():
        update(sa_ref, qi, True)

    @pl.when(qi % 2 == 1)
    def _():
        scores(qi, sb_ref)
        update(sa_ref, qi - 1, False)
        update(sb_ref, qi, True)
    o_t = jnp.concatenate([acc_ref[h, 0:HEAD_DIM, :] / acc_ref[h, HEAD_DIM:HEAD_DIM + 1, :] for h in range(2)],
                          axis=0)
    gate = g_ref[0]
    o_ref[0] = (o_t.T * (gate * _sigmoid(gate))).astype(o_ref.dtype)


def _fox(z3, c_sp, col0, width, tq):
    bsz, seq, _ = z3.shape
    n_pairs = width // LANES
    blk = lambda j: col0 // LANES + j * n_pairs
    tqs = min(tq, 2 * LANES)
    s_buf = pltpu.VMEM((2 * tq // tqs, tq, tqs), F32)
    return pl.pallas_call(
        functools.partial(_fox_kernel, tq=tq, tqs=tqs, t_chunk=min(seq, 512)),
        grid=(bsz, n_pairs, seq // tq),
        in_specs=[pl.BlockSpec((1, tq, LANES), lambda b, p, i: (b, i, blk(0) + p)),
                  pl.BlockSpec((1, seq, LANES), lambda b, p, i: (b, 0, blk(1) + p)),
                  pl.BlockSpec((1, seq, LANES), lambda b, p, i: (b, 0, blk(2) + p)),
                  pl.BlockSpec((1, tq, LANES), lambda b, p, i: (b, i, blk(3) + p)),
                  pl.BlockSpec((1, seq, LANES), lambda b, p, i: (b, 0, 0))],
        out_specs=pl.BlockSpec((1, tq, LANES), lambda b, p, i: (b, i, p)),
        out_shape=jax.ShapeDtypeStruct((bsz, seq, width), BF16),
        scratch_shapes=[pltpu.VMEM((2, VT_ROWS, seq), BF16), pltpu.VMEM((2, 1, tq), F32),
                        pltpu.VMEM((2, VT_ROWS, tq), F32), s_buf, s_buf],
        compiler_params=pltpu.CompilerParams(dimension_semantics=("arbitrary",) * 3,
                                             vmem_limit_bytes=VMEM_LIMIT),
        name="fox_attention",
    )(z3, z3, z3, z3, c_sp)


def _merge_kernel(yr_ref, yf_ref, gr_ref, gf_ref, x_ref, wr_ref, wf_ref, wo_ref, o_ref):
    u_rw = _dot(yr_ref[...], wr_ref[...])
    u_fox = _dot(yf_ref[...], wf_ref[...])
    merged = _sigmoid(gr_ref[...]) * u_rw + _sigmoid(gf_ref[...]) * u_fox
    o_ref[...] = x_ref[...] + _dot(merged.astype(BF16), wo_ref[...])


def _merge(y_rw, y_fox, z, gate_col0, x, w_up_rw, w_up_fox, w_out, tm):
    m, d = x.shape
    cw = y_rw.shape[1]
    gblk = gate_col0 // d
    const = lambda shape: pl.BlockSpec(shape, lambda i: (0, 0), pipeline_mode=pl.Buffered(1))
    return pl.pallas_call(
        _merge_kernel,
        grid=(m // tm,),
        in_specs=[pl.BlockSpec((tm, cw), lambda i: (i, 0)),
                  pl.BlockSpec((tm, cw), lambda i: (i, 0)),
                  pl.BlockSpec((tm, d), lambda i: (i, gblk)),
                  pl.BlockSpec((tm, d), lambda i: (i, gblk + 1)),
                  pl.BlockSpec((tm, d), lambda i: (i, 0)),
                  const((cw, d)), const((cw, d)), const((d, d))],
        out_specs=pl.BlockSpec((tm, d), lambda i: (i, 0)),
        out_shape=jax.ShapeDtypeStruct((m, d), F32),
        compiler_params=pltpu.CompilerParams(dimension_semantics=("arbitrary",),
                                             vmem_limit_bytes=VMEM_LIMIT),
        name="merge_outproj",
    )(y_rw, y_fox, z, z, x, w_up_rw, w_up_fox, w_out)


def _ple_kernel(x_ref, p_ref, ng_ref, fg_ref, wp_ref, wg_ref, o_ref):
    x = x_ref[...]
    hn = x * lax.rsqrt(jnp.mean(x * x, axis=-1, keepdims=True) + NORM_EPS) * ng_ref[...]
    gate = _sigmoid(_dot(hn.astype(BF16), wg_ref[...]))
    ple = _dot(p_ref[...].astype(BF16), wp_ref[...])
    x2 = x + ple * gate
    o_ref[...] = x2 * lax.rsqrt(jnp.mean(x2 * x2, axis=-1, keepdims=True) + NORM_EPS) * fg_ref[...]


def _ple(x1, p, ple_norm_g, final_norm_g, ple_proj, ple_gate_w, tm):
    m, d = x1.shape
    pd = p.shape[1]
    const = lambda shape: pl.BlockSpec(shape, lambda i: (0, 0), pipeline_mode=pl.Buffered(1))
    return pl.pallas_call(
        _ple_kernel,
        grid=(m // tm,),
        in_specs=[pl.BlockSpec((tm, d), lambda i: (i, 0)),
                  pl.BlockSpec((tm, pd), lambda i: (i, 0)),
                  const((1, d)), const((1, d)), const((pd, d)), const((d, d))],
        out_specs=pl.BlockSpec((tm, d), lambda i: (i, 0)),
        out_shape=jax.ShapeDtypeStruct((m, d), F32),
        compiler_params=pltpu.CompilerParams(dimension_semantics=("arbitrary",),
                                             vmem_limit_bytes=VMEM_LIMIT),
        name="ple_final_norm",
    )(x1, p, ple_norm_g.reshape(1, d), final_norm_g.reshape(1, d), ple_proj, ple_gate_w)


def _regroup_kernel(main_ref, lora_ref, fl_ref, o_ref, *, n_heads):
    last = pl.num_programs(0) - 1

    @pl.when(pl.program_id(0) < last)
    def _():
        o_ref[...] = main_ref[...].T.astype(BF16)

    @pl.when(pl.program_id(0) == last)
    def _():
        row = lax.broadcasted_iota(jnp.int32, (LANES, 1), 0)
        o_ref[:, 0:LANES] = lora_ref[...].T.astype(BF16)
        o_ref[:, LANES:2 * LANES] = jnp.where(row < n_heads, fl_ref[...], 0.0).T.astype(BF16)


def _regroup_weights(w_t, cw, lw, n_heads, tr):
    n_in, d = w_t.shape
    fl0 = 8 * cw + 2 * lw
    small0 = 8 * cw + 2 * d
    n_pad = small0 + 2 * LANES
    assert tr == 2 * LANES and (4 * cw) % tr == 0 and (2 * d) % tr == 0 and n_in == fl0 + n_heads + 2 * d
    assert (2 * lw) % 8 == 0 and n_heads % 8 == 0

    def main_rows(i):
        r = i * tr
        src = jnp.where(r < 4 * cw, r, jnp.where(r < 8 * cw, r + 2 * lw,
                                                 jnp.where(r < small0, r + 2 * lw + n_heads, 0)))
        return pl.multiple_of(src, 8), 0

    rows = lambda n: (pl.Element(n), pl.Element(d))
    return pl.pallas_call(
        functools.partial(_regroup_kernel, n_heads=n_heads),
        grid=(n_pad // tr,),
        in_specs=[pl.BlockSpec(rows(tr), main_rows),
                  pl.BlockSpec(rows(LANES), lambda i: (4 * cw, 0)),
                  pl.BlockSpec(rows(LANES), lambda i: (fl0, 0))],
        out_specs=pl.BlockSpec((d, tr), lambda i: (0, i)),
        out_shape=jax.ShapeDtypeStruct((d, n_pad), BF16),
        compiler_params=pltpu.CompilerParams(dimension_semantics=("arbitrary",),
                                             vmem_limit_bytes=VMEM_LIMIT),
        name="regroup_w_in",
    )(w_t, w_t, w_t)


def _layer(x2d, p2d, seq, norm_g, w_in, rw_shift_mu, rw_w0, rw_w_lora_up, rw_a0, rw_a_lora_up, rw_k_k, rw_k_a,
           rw_r_k, rw_ln_g, rw_ln_b, fox_b_f, w_up_rwkv, w_up_fox, w_out, ple_proj, ple_gate_w, ple_norm_g,
           final_norm_g, *, tiles):
    m, d = x2d.shape
    bsz = m // seq
    cw = w_up_rwkv.shape[0]
    n_heads = fox_b_f.shape[0]
    lw = rw_w_lora_up.shape[0]
    assert 2 * lw == LANES and cw % LANES == 0 and C_TERMS * (cw // LANES) <= HEAD_DIM

    fox0, gate0, small0 = 4 * cw, 8 * cw, 8 * cw + 2 * d
    n_pad = small0 + 2 * LANES
    w_cat = _regroup_weights(w_in.T, cw, lw, n_heads, tiles["regroup_tr"])
    mu_cat = jnp.concatenate([
        rw_shift_mu[0:4 * cw], jnp.zeros((small0 - 4 * cw,), F32), rw_shift_mu[4 * cw:],
        jnp.zeros((LANES,), F32)]).reshape(1, n_pad)

    z = _inproj(x2d, norm_g, w_cat, mu_cat, seq, tiles["in_tm"], tiles["in_tn"])
    z3 = z.reshape(bsz, seq, n_pad)
    lora_blk = small0 // (2 * LANES)

    row = lambda a: a.reshape(1, cw)
    wup = jnp.pad(rw_w_lora_up, ((0, lw), (0, 0))).astype(BF16)
    aup = jnp.pad(rw_a_lora_up, ((lw, 0), (0, 0))).astype(BF16)
    y_rw = _rwkv(z3, lora_blk, wup, aup, row(rw_w0), row(rw_a0), row(rw_k_k), row(rw_k_a), row(rw_r_k),
                 row(rw_ln_g), row(rw_ln_b), cw)

    bf_pad = jnp.pad(fox_b_f, (0, LANES - n_heads)).reshape(1, LANES)
    c_sp = _fcum(z3, lora_blk, bf_pad, n_heads, tiles["fc_tc"])
    y_fox = _fox(z3, c_sp, fox0, cw, tiles["fox_tq"])

    x1 = _merge(y_rw.reshape(m, cw), y_fox.reshape(m, cw), z, gate0, x2d, w_up_rwkv.astype(BF16),
                w_up_fox.astype(BF16), w_out.astype(BF16), tiles["merge_tm"])
    return _ple(x1, p2d, ple_norm_g, final_norm_g, ple_proj.astype(BF16), ple_gate_w.astype(BF16),
                tiles["ple_tm"])


_TILES = dict(regroup_tr=256, in_tm=512, in_tn=1792, fc_tc=256, fox_tq=512, merge_tm=256, ple_tm=256)


def kernel(x, p, norm_g, w_in, rw_shift_mu, rw_w0, rw_w_lora_up, rw_a0, rw_a_lora_up, rw_k_k, rw_k_a, rw_r_k,
           rw_ln_g, rw_ln_b, fox_b_f, w_up_rwkv, w_up_fox, w_out, ple_proj, ple_gate_w, ple_norm_g,
           final_norm_g):
    bsz, seq, d = x.shape
    depth = p.shape[0]
    assert depth == 1, "the final norm is fused into the (single) layer"
    out = _layer(x.reshape(bsz * seq, d), p[0].reshape(bsz * seq, -1), seq, norm_g[0], w_in[0], rw_shift_mu[0],
                 rw_w0[0], rw_w_lora_up[0], rw_a0[0], rw_a_lora_up[0], rw_k_k[0], rw_k_a[0],
                 rw_r_k[0].reshape(-1), rw_ln_g[0], rw_ln_b[0], fox_b_f[0], w_up_rwkv[0], w_up_fox[0], w_out[0],
                 ple_proj[0], ple_gate_w[0], ple_norm_g[0], final_norm_g, tiles=_TILES)
    return out.reshape(bsz, seq, d)
```

```python
import functools
import math

import jax
import jax.numpy as jnp
from jax import lax
from jax.experimental import pallas as pl
from jax.experimental.pallas import tpu as pltpu

F32 = jnp.float32
BF16 = jnp.bfloat16

HEAD_DIM = 64
LANES = 128
NORM_EPS = 1e-6
GN_EPS = 64e-5
Q_SCALE = HEAD_DIM ** -0.5
EXP_NEG_HALF = math.exp(-0.5)
LOG2_E = math.log2(math.e)
NEG_BIG = -1e30
VMEM_LIMIT = 56 * 1024 * 1024

SPLIT_TERMS = 2
RW_CHUNK = 64
RW_CHUNKS_PER_STEP = 4
C_TERMS = 3
VT_ROWS = HEAD_DIM + 16


def _dot(a, b):
    return jnp.dot(a, b, preferred_element_type=F32)


def _dot_nt(a, b):
    return lax.dot_general(a, b, (((1,), (1,)), ((), ())), preferred_element_type=F32)


def _dot_tn(a, b):
    return lax.dot_general(a, b, (((0,), (0,)), ((), ())), preferred_element_type=F32)


def _split_bf16(x, n):
    parts = []
    rem = x
    for i in range(n):
        p = rem.astype(BF16)
        parts.append(p)
        if i + 1 < n:
            rem = rem - p.astype(F32)
    return parts


def _sigmoid(x):
    return 1.0 / (1.0 + jnp.exp(-x))


def _inproj_kernel(x_ref, g_ref, w_ref, mu_ref, o_ref, carry_ref, *, tiles_per_seq):
    x = x_ref[...]
    rstd = lax.rsqrt(jnp.mean(x * x, axis=-1, keepdims=True) + NORM_EPS)
    z = _dot((x * g_ref[...]).astype(BF16), w_ref[...]) * rstd
    tm = z.shape[0]
    first = (pl.program_id(1) % tiles_per_seq) == 0
    prev_last = jnp.where(first, 0.0, carry_ref[7:8, :])
    row = lax.broadcasted_iota(jnp.int32, z.shape, 0)
    z_prev = jnp.where(row == 0, prev_last, pltpu.roll(z, 1, 0))
    o_ref[...] = z + (z_prev - z) * mu_ref[...]
    carry_ref[...] = z[tm - 8:tm, :]


def _inproj(x, g, w, mu, seq, tm, tn):
    m, d = x.shape
    n = w.shape[1]
    return pl.pallas_call(
        functools.partial(_inproj_kernel, tiles_per_seq=seq // tm),
        grid=(n // tn, m // tm),
        in_specs=[pl.BlockSpec((tm, d), lambda j, i: (i, 0)),
                  pl.BlockSpec((1, d), lambda j, i: (0, 0)),
                  pl.BlockSpec((d, tn), lambda j, i: (0, j)),
                  pl.BlockSpec((1, tn), lambda j, i: (0, j))],
        out_specs=pl.BlockSpec((tm, tn), lambda j, i: (i, j)),
        out_shape=jax.ShapeDtypeStruct((m, n), F32),
        scratch_shapes=[pltpu.VMEM((8, tn), F32)],
        compiler_params=pltpu.CompilerParams(dimension_semantics=("arbitrary", "arbitrary"),
                                             vmem_limit_bytes=VMEM_LIMIT),
        name="inproj",
    )(x, g.reshape(1, d), w, mu)


def _rwkv_chunk(r, k, v, gate, lora_b, lora_t, wup, aup, w0, a0, kkp, kap, rkp, lng, lnb, s_old, *, c):
    rows, width = r.shape
    n_pairs = width // LANES
    pairs = range(n_pairs)
    subs = range(rows // c)
    blocks = [(i, p) for i in subs for p in pairs]
    cols = lambda x: [x[i * c:(i + 1) * c, p * LANES:(p + 1) * LANES] for i, p in blocks]
    lane = lax.broadcasted_iota(jnp.int32, (1, LANES), 1)
    m0 = lane < HEAD_DIM

    ri = lax.broadcasted_iota(jnp.int32, (LANES, LANES), 0)
    ci = lax.broadcasted_iota(jnp.int32, (LANES, LANES), 1)
    same_head = (ri // HEAD_DIM) == (ci // HEAD_DIM)
    ones_bd = jnp.where(same_head, 1.0, 0.0).astype(BF16)

    def segsum(x):
        xr = jnp.concatenate(cols(x), axis=0)
        sr = sum(_dot(part, ones_bd) for part in _split_bf16(xr, SPLIT_TERMS))
        return jnp.concatenate([jnp.concatenate([sr[(i * n_pairs + p) * c:(i * n_pairs + p + 1) * c] for p in pairs],
                                                axis=1) for i in subs], axis=0)

    w_raw = w0 + _dot(lora_t, wup)
    logw = -EXP_NEG_HALF * _sigmoid(w_raw)
    a = _sigmoid(a0 + _dot(lora_b, aup))
    kkr = k * kkp
    kk = kkr / jnp.maximum(jnp.sqrt(segsum(kkr * kkr)), 1e-12)
    b = kk * a
    k2 = k * (1.0 + (a - 1.0) * kap)

    ti = lax.broadcasted_iota(jnp.int32, (rows, rows), 0)
    si = lax.broadcasted_iota(jnp.int32, (rows, rows), 1)
    tri = jnp.where((si <= ti) & (si // c == ti // c), 1.0, 0.0).astype(BF16)
    g = sum(_dot(tri, part) for part in _split_bf16(logw, SPLIT_TERMS))
    g_ends = [g[(i + 1) * c - 1:(i + 1) * c, :] for i in subs]
    g_last = jnp.concatenate([jnp.broadcast_to(ge, (c, width)) for ge in g_ends], axis=0)
    egd = jnp.exp(g_last - g)
    eng = jnp.exp(-g)
    rg = cols(r * jnp.exp(g))
    kkg = cols(kk * jnp.exp(g - logw))
    kn = cols(k2 * eng)
    bn = cols(b * eng)
    kd = cols(k2 * egd)
    bdec = cols(b * egd)
    vs = cols(v)
    decay_c = [jnp.exp(g_ends[i][:, p * LANES:(p + 1) * LANES]) for i, p in blocks]
    nb = range(len(blocks))

    def stack(x):
        return jnp.concatenate([jnp.where(m0, x, 0.0), jnp.where(m0, 0.0, x)], axis=0).astype(BF16)

    t_in = lax.broadcasted_iota(jnp.int32, (c, LANES), 0)
    s_in = lax.broadcasted_iota(jnp.int32, (c, LANES), 1) % HEAD_DIM
    strict = s_in < t_in
    incl = s_in <= t_in
    eye = jnp.where(s_in == t_in, 1.0, 0.0)

    lhs_kr = [jnp.concatenate([kkg[b], rg[b]], axis=0).astype(BF16) for b in nb]
    rhs_bk = [jnp.concatenate([stack(bn[b]), stack(kn[b])], axis=0) for b in nb]
    aa = [_dot_nt(lhs_kr[b], rhs_bk[b]) for b in nb]
    a_kb = [jnp.where(strict, aa[b][0:c, 0:LANES], 0.0) for b in nb]
    a_kk = [jnp.where(strict, aa[b][0:c, LANES:2 * LANES], 0.0).astype(BF16) for b in nb]
    a_rb = [jnp.where(incl, aa[b][c:2 * c, 0:LANES], 0.0) for b in nb]
    a_rk = [jnp.where(incl, aa[b][c:2 * c, LANES:2 * LANES], 0.0) for b in nb]
    a_r = [jnp.concatenate([a_rk[b], a_rb[b]], axis=1).astype(BF16) for b in nb]
    v_st = [stack(vs[b]) for b in nb]
    kb = [jnp.concatenate([kd[b], bdec[b]], axis=0).astype(BF16) for b in nb]
    av = [_dot(a_kk[b], v_st[b]) for b in nb]

    tinv = [eye - a_kb[b] for b in nb]
    lpow = [_dot(a_kb[b].astype(BF16), stack(a_kb[b])) for b in nb]
    n_sq = int(math.log2(c)) - 1
    for i in range(n_sq):
        last = i + 1 == n_sq
        w_pow = [stack(lpow[b]) for b in nb]
        lhs = [tinv[b].astype(BF16) if last else jnp.concatenate([tinv[b], lpow[b]], axis=0).astype(BF16)
               for b in nb]
        prod = [_dot(lhs[b], w_pow[b]) for b in nb]
        tinv = [tinv[b] + prod[b][0:c] for b in nb]
        if not last:
            lpow = [prod[b][c:2 * c] for b in nb]
    tinv = [t.astype(BF16) for t in tinv]

    s_cur = list(s_old)
    ys = []
    for i in subs:
        bi = [i * n_pairs + p for p in pairs]
        s_b = [s_cur[p].astype(BF16) for p in pairs]
        sh = [_dot_nt(lhs_kr[bi[p]], s_b[p]) for p in pairs]
        x = [sh[p][0:c] + av[bi[p]] for p in pairs]
        u = [_dot(tinv[bi[p]], stack(x[p])) for p in pairs]
        vu_st = [jnp.concatenate([v_st[bi[p]], stack(-u[p])], axis=0) for p in pairs]
        ys.append(jnp.concatenate([sh[p][c:2 * c] + _dot(a_r[bi[p]], vu_st[p]) for p in pairs], axis=1))
        vu = [jnp.concatenate([vs[bi[p]], -u[p]], axis=0).astype(BF16) for p in pairs]
        s_cur = [s_cur[p] * decay_c[bi[p]] + jnp.where(same_head, _dot_tn(vu[p], kb[bi[p]]), 0.0) for p in pairs]
    y = jnp.concatenate(ys, axis=0)
    s_new = s_cur

    inv_n = 1.0 / HEAD_DIM
    mean = segsum(y) * inv_n
    dlt = y - mean
    var = segsum(dlt * dlt) * inv_n
    yn = dlt * lax.rsqrt(var + GN_EPS) * lng + lnb
    bonus = segsum(r * k2 * rkp) * v
    out = (yn + bonus) * (gate * _sigmoid(gate))
    return out, s_new


def _rwkv_kernel(r_ref, k_ref, v_ref, g_ref, lora_ref, wup_ref, aup_ref, w0_ref, a0_ref, kk_ref, ka_ref,
                 rk_ref, lng_ref, lnb_ref, o_ref, s_ref):
    @pl.when(pl.program_id(1) == 0)
    def _():
        s_ref[...] = jnp.zeros_like(s_ref)

    lora = lora_ref[0][:, 0:LANES]
    out, s_new = _rwkv_chunk(
        r_ref[0], k_ref[0], v_ref[0], g_ref[0], lora.astype(BF16), jnp.tanh(lora).astype(BF16),
        wup_ref[...], aup_ref[...], w0_ref[...], a0_ref[...], kk_ref[...], ka_ref[...], rk_ref[...],
        lng_ref[...], lnb_ref[...], [s_ref[p] for p in range(s_ref.shape[0])], c=RW_CHUNK)
    for p in range(s_ref.shape[0]):
        s_ref[p] = s_new[p]
    o_ref[0] = out.astype(o_ref.dtype)


def _rwkv(z3, lora_blk, wup, aup, w0, a0, k_k, k_a, r_k, ln_g, ln_b, width):
    bsz, seq, _ = z3.shape
    c = min(seq, RW_CHUNK * RW_CHUNKS_PER_STEP)
    assert width % LANES == 0
    zspec = lambda j: pl.BlockSpec((1, c, width), lambda b, t, j=j: (b, t, j))
    pspec = lambda rows: pl.BlockSpec((rows, width), lambda b, t: (0, 0))
    return pl.pallas_call(
        _rwkv_kernel,
        grid=(bsz, seq // c),
        in_specs=[zspec(0), zspec(1), zspec(2), zspec(3),
                  pl.BlockSpec((1, c, 2 * LANES), lambda b, t: (b, t, lora_blk)),
                  pspec(LANES), pspec(LANES)] + [pspec(1)] * 7,
        out_specs=pl.BlockSpec((1, c, width), lambda b, t: (b, t, 0)),
        out_shape=jax.ShapeDtypeStruct((bsz, seq, width), BF16),
        scratch_shapes=[pltpu.VMEM((width // LANES, LANES, LANES), F32)],
        compiler_params=pltpu.CompilerParams(dimension_semantics=("arbitrary", "arbitrary"),
                                             vmem_limit_bytes=VMEM_LIMIT),
        name="rwkv7_scan",
    )(z3, z3, z3, z3, z3, wup, aup, w0, a0, k_k, k_a, r_k, ln_g, ln_b)


def _c_lane(pair, h):
    return (1 - h) * HEAD_DIM + C_TERMS * pair


def _fcum_kernel(l_ref, bf_ref, o_ref, carry_ref, *, n_heads):
    @pl.when(pl.program_id(1) == 0)
    def _():
        carry_ref[...] = jnp.zeros_like(carry_ref)

    x = l_ref[0] + bf_ref[...]
    log_f = jnp.minimum(x, 0.0) - jnp.log(1.0 + jnp.exp(-jnp.abs(x)))
    tc = x.shape[0]
    ti = lax.broadcasted_iota(jnp.int32, (tc, tc), 0)
    si = lax.broadcasted_iota(jnp.int32, (tc, tc), 1)
    tril = jnp.where(si <= ti, 1.0, 0.0).astype(BF16)
    c = sum(_dot(tril, part) for part in _split_bf16(log_f, 3)) + carry_ref[0:1, :]
    carry_ref[...] = jnp.broadcast_to(c[tc - 1:tc, :], carry_ref.shape)
    lin = lax.broadcasted_iota(jnp.int32, (LANES, LANES), 0)
    lout = lax.broadcasted_iota(jnp.int32, (LANES, LANES), 1)
    terms = _split_bf16(-LOG2_E * c, C_TERMS)
    sel = [jnp.where((lin < n_heads) & (lout == _c_lane(lin // 2, lin % 2) + i), 1.0, 0.0).astype(BF16)
           for i in range(C_TERMS)]
    o_ref[0] = sum(_dot(terms[i], sel[i]) for i in range(C_TERMS)).astype(o_ref.dtype)


def _fcum(z3, lora_blk, bf_pad, n_heads, tc):
    bsz, seq, _ = z3.shape
    return pl.pallas_call(
        functools.partial(_fcum_kernel, n_heads=n_heads),
        grid=(bsz, seq // tc),
        in_specs=[pl.BlockSpec((1, tc, LANES), lambda b, t: (b, t, 2 * lora_blk + 1)),
                  pl.BlockSpec((1, LANES), lambda b, t: (0, 0))],
        out_specs=pl.BlockSpec((1, tc, LANES), lambda b, t: (b, t, 0)),
        out_shape=jax.ShapeDtypeStruct((bsz, seq, LANES), BF16),
        scratch_shapes=[pltpu.VMEM((8, LANES), F32)],
        compiler_params=pltpu.CompilerParams(dimension_semantics=("arbitrary", "arbitrary"),
                                             vmem_limit_bytes=VMEM_LIMIT),
        name="fox_cumsum",
    )(z3, bf_pad)


def _fox_kernel(q_ref, k_ref, v_ref, g_ref, c_ref, o_ref, vt_ref, qt_ref, m_ref, acc_ref, sa_ref, sb_ref, *, tq, tqs,
                t_chunk):
    pair = pl.program_id(1)
    seq = k_ref.shape[1]
    nq = seq // tq
    lane = lax.broadcasted_iota(jnp.int32, (1, LANES), 1)
    row = lax.broadcasted_iota(jnp.int32, (LANES, 1), 0)

    one_row = jnp.where(lax.broadcasted_iota(jnp.int32, (VT_ROWS - HEAD_DIM, t_chunk), 0) == 0, 1.0, 0.0)
    for t0 in range(0, seq, t_chunk):
        v_t = v_ref[0, t0:t0 + t_chunk, :].T
        q_t = (q_ref[0, t0:t0 + t_chunk, :] * (Q_SCALE * LOG2_E)).T
        for h in range(2):
            vt_ref[h, 0:HEAD_DIM, t0:t0 + t_chunk] = v_t[h * HEAD_DIM:(h + 1) * HEAD_DIM].astype(BF16)
            vt_ref[h, HEAD_DIM:VT_ROWS, t0:t0 + t_chunk] = one_row.astype(BF16)
            r0 = _c_lane(pair, h)
            ones = jnp.where((row >= r0) & (row < r0 + C_TERMS), 1.0, 0.0)
            qt_ref[h, :, t0:t0 + t_chunk] = jnp.where((row // HEAD_DIM) == h, q_t, ones).astype(BF16)
    acc_ref[...] = jnp.zeros_like(acc_ref)
    m_ref[...] = jnp.full_like(m_ref, NEG_BIG)

    chains = [(h, slice(q0, q0 + tqs)) for h in range(2) for q0 in range(0, tq, tqs)]

    def scores(qi, j, dst_ref):
        ks = pl.multiple_of(j * tq, tq)
        kb = k_ref[0, pl.ds(ks, tq), :].astype(BF16)
        cb = c_ref[0, pl.ds(ks, tq), :]
        ka = [jnp.where((lane // HEAD_DIM) == h, kb, cb) for h in range(2)]
        for i, (h, qs) in enumerate(chains):
            dst_ref[i] = _dot(ka[h], qt_ref[h, :, pl.ds(pl.multiple_of(qi * tq + qs.start, tqs), tqs)])

    def update(src_ref, qi, j, diagonal):
        ks = pl.multiple_of(j * tq, tq)
        nk = [qs.stop if diagonal else tq for h, qs in chains]
        s = [src_ref[i, 0:nk[i], :] for i in range(len(chains))]
        if diagonal:
            s = [jnp.where(lax.broadcasted_iota(jnp.int32, (nk[i], tqs), 0)
                           <= lax.broadcasted_iota(jnp.int32, (nk[i], tqs), 1) + qs.start, s[i], NEG_BIG)
                 for i, (h, qs) in enumerate(chains)]
        m_old = [jnp.where(j == 0, NEG_BIG, m_ref[h, :, qs]) for h, qs in chains]
        m_new = [jnp.maximum(m_old[i], jnp.max(s[i], axis=0, keepdims=True)) for i in range(len(chains))]
        alpha = [jnp.exp2(m_old[i] - m_new[i]) for i in range(len(chains))]
        pexp = [jnp.exp2(s[i] - m_new[i]).astype(BF16) for i in range(len(chains))]
        pv = [_dot(vt_ref[h, :, pl.ds(ks, nk[i])], pexp[i]) for i, (h, qs) in enumerate(chains)]
        for i, (h, qs) in enumerate(chains):
            acc_ref[h, :, qs] = alpha[i] * acc_ref[h, :, qs] + pv[i]
            m_ref[h, :, qs] = m_new[i]
        if diagonal:
            o_t = jnp.concatenate([acc_ref[h, 0:HEAD_DIM, :] / acc_ref[h, HEAD_DIM:HEAD_DIM + 1, :]
                                   for h in range(2)], axis=0)
            q0 = pl.multiple_of(qi * tq, tq)
            gate = g_ref[0, pl.ds(q0, tq), :]
            o_ref[0, pl.ds(q0, tq), :] = (o_t.T * (gate * _sigmoid(gate))).astype(o_ref.dtype)

    def following(qi, j):
        wrap = j == qi
        return jnp.where(wrap, qi + 1, qi), jnp.where(wrap, 0, j + 1)

    def half_step(cur_ref, nxt_ref, qi, j):
        qn, jn = following(qi, j)
        qn_c = jnp.minimum(qn, nq - 1)

        @pl.when(j < qi)
        def _():
            scores(qn_c, jn, nxt_ref)
            update(cur_ref, qi, j, False)

        @pl.when(j == qi)
        def _():
            scores(qn_c, jn, nxt_ref)
            update(cur_ref, qi, j, True)
        return qn, jn

    scores(0, 0, sa_ref)

    def body(n, carry):
        qi, j = carry
        qi, j = half_step(sa_ref, sb_ref, qi, j)
        return half_step(sb_ref, sa_ref, qi, j)

    lax.fori_loop(0, nq * (nq + 1) // 4, body, (jnp.int32(0), jnp.int32(0)))


def _fox(z3, c_sp, col0, width, tq):
    bsz, seq, _ = z3.shape
    n_pairs = width // LANES
    nq = seq // tq
    assert (nq * (nq + 1) // 2) % 2 == 0
    blk = lambda j: col0 // LANES + j * n_pairs
    tqs = min(tq, 2 * LANES)
    s_buf = pltpu.VMEM((2 * tq // tqs, tq, tqs), F32)
    zspec = lambda j: pl.BlockSpec((1, seq, LANES), lambda b, p, j=j: (b, 0, blk(j) + p))
    return pl.pallas_call(
        functools.partial(_fox_kernel, tq=tq, tqs=tqs, t_chunk=min(seq, 512)),
        grid=(bsz, n_pairs),
        in_specs=[zspec(0), zspec(1), zspec(2), zspec(3),
                  pl.BlockSpec((1, seq, LANES), lambda b, p: (b, 0, 0))],
        out_specs=pl.BlockSpec((1, seq, LANES), lambda b, p: (b, 0, p)),
        out_shape=jax.ShapeDtypeStruct((bsz, seq, width), BF16),
        scratch_shapes=[pltpu.VMEM((2, VT_ROWS, seq), BF16), pltpu.VMEM((2, LANES, seq), BF16),
                        pltpu.VMEM((2, 1, tq), F32), pltpu.VMEM((2, VT_ROWS, tq), F32), s_buf, s_buf],
        compiler_params=pltpu.CompilerParams(dimension_semantics=("arbitrary",) * 2,
                                             vmem_limit_bytes=VMEM_LIMIT),
        name="fox_attention",
    )(z3, z3, z3, z3, c_sp)


def _merge_kernel(yr_ref, yf_ref, gr_ref, gf_ref, x_ref, wr_ref, wf_ref, wo_ref, o_ref):
    u_rw = _dot(yr_ref[...], wr_ref[...])
    u_fox = _dot(yf_ref[...], wf_ref[...])
    merged = _sigmoid(gr_ref[...]) * u_rw + _sigmoid(gf_ref[...]) * u_fox
    o_ref[...] = x_ref[...] + _dot(merged.astype(BF16), wo_ref[...])


def _merge(y_rw, y_fox, z, gate_col0, x, w_up_rw, w_up_fox, w_out, tm):
    m, d = x.shape
    cw = y_rw.shape[1]
    gblk = gate_col0 // d
    const = lambda shape: pl.BlockSpec(shape, lambda i: (0, 0), pipeline_mode=pl.Buffered(1))
    return pl.pallas_call(
        _merge_kernel,
        grid=(m // tm,),
        in_specs=[pl.BlockSpec((tm, cw), lambda i: (i, 0)),
                  pl.BlockSpec((tm, cw), lambda i: (i, 0)),
                  pl.BlockSpec((tm, d), lambda i: (i, gblk)),
                  pl.BlockSpec((tm, d), lambda i: (i, gblk + 1)),
                  pl.BlockSpec((tm, d), lambda i: (i, 0)),
                  const((cw, d)), const((cw, d)), const((d, d))],
        out_specs=pl.BlockSpec((tm, d), lambda i: (i, 0)),
        out_shape=jax.ShapeDtypeStruct((m, d), F32),
        compiler_params=pltpu.CompilerParams(dimension_semantics=("arbitrary",),
                                             vmem_limit_bytes=VMEM_LIMIT),
        name="merge_outproj",
    )(y_rw, y_fox, z, z, x, w_up_rw, w_up_fox, w_out)


def _ple_kernel(x_ref, p_ref, ng_ref, fg_ref, wp_ref, wg_ref, o_ref):
    x = x_ref[...]
    hn = x * lax.rsqrt(jnp.mean(x * x, axis=-1, keepdims=True) + NORM_EPS) * ng_ref[...]
    gate = _sigmoid(_dot(hn.astype(BF16), wg_ref[...]))
    ple = _dot(p_ref[...].astype(BF16), wp_ref[...])
    x2 = x + ple * gate
    o_ref[...] = x2 * lax.rsqrt(jnp.mean(x2 * x2, axis=-1, keepdims=True) + NORM_EPS) * fg_ref[...]


def _ple(x1, p, ple_norm_g, final_norm_g, ple_proj, ple_gate_w, tm):
    m, d = x1.shape
    pd = p.shape[1]
    const = lambda shape: pl.BlockSpec(shape, lambda i: (0, 0), pipeline_mode=pl.Buffered(1))
    return pl.pallas_call(
        _ple_kernel,
        grid=(m // tm,),
        in_specs=[pl.BlockSpec((tm, d), lambda i: (i, 0)),
                  pl.BlockSpec((tm, pd), lambda i: (i, 0)),
                  const((1, d)), const((1, d)), const((pd, d)), const((d, d))],
        out_specs=pl.BlockSpec((tm, d), lambda i: (i, 0)),
        out_shape=jax.ShapeDtypeStruct((m, d), F32),
        compiler_params=pltpu.CompilerParams(dimension_semantics=("arbitrary",),
                                             vmem_limit_bytes=VMEM_LIMIT),
        name="ple_final_norm",
    )(x1, p, ple_norm_g.reshape(1, d), final_norm_g.reshape(1, d), ple_proj, ple_gate_w)


def _regroup_kernel(main_ref, lora_ref, fl_ref, o_ref, *, n_heads):
    last = pl.num_programs(0) - 1

    @pl.when(pl.program_id(0) < last)
    def _():
        o_ref[...] = main_ref[...].T.astype(BF16)

    @pl.when(pl.program_id(0) == last)
    def _():
        row = lax.broadcasted_iota(jnp.int32, (LANES, 1), 0)
        o_ref[:, 0:LANES] = lora_ref[...].T.astype(BF16)
        o_ref[:, LANES:2 * LANES] = jnp.where(row < n_heads, fl_ref[...], 0.0).T.astype(BF16)


def _regroup_weights(w_t, cw, lw, n_heads, tr):
    n_in, d = w_t.shape
    fl0 = 8 * cw + 2 * lw
    small0 = 8 * cw + 2 * d
    n_pad = small0 + 2 * LANES
    assert tr == 2 * LANES and (4 * cw) % tr == 0 and (2 * d) % tr == 0 and n_in == fl0 + n_heads + 2 * d
    assert (2 * lw) % 8 == 0 and n_heads % 8 == 0

    def main_rows(i):
        r = i * tr
        src = jnp.where(r < 4 * cw, r, jnp.where(r < 8 * cw, r + 2 * lw,
                                                 jnp.where(r < small0, r + 2 * lw + n_heads, 0)))
        return pl.multiple_of(src, 8), 0

    rows = lambda n: (pl.Element(n), pl.Element(d))
    return pl.pallas_call(
        functools.partial(_regroup_kernel, n_heads=n_heads),
        grid=(n_pad // tr,),
        in_specs=[pl.BlockSpec(rows(tr), main_rows),
                  pl.BlockSpec(rows(LANES), lambda i: (4 * cw, 0)),
                  pl.BlockSpec(rows(LANES), lambda i: (fl0, 0))],
        out_specs=pl.BlockSpec((d, tr), lambda i: (0, i)),
        out_shape=jax.ShapeDtypeStruct((d, n_pad), BF16),
        compiler_params=pltpu.CompilerParams(dimension_semantics=("arbitrary",),
                                             vmem_limit_bytes=VMEM_LIMIT),
        name="regroup_w_in",
    )(w_t, w_t, w_t)


def _layer(x2d, p2d, seq, norm_g, w_in, rw_shift_mu, rw_w0, rw_w_lora_up, rw_a0, rw_a_lora_up, rw_k_k, rw_k_a,
           rw_r_k, rw_ln_g, rw_ln_b, fox_b_f, w_up_rwkv, w_up_fox, w_out, ple_proj, ple_gate_w, ple_norm_g,
           final_norm_g, *, tiles):
    m, d = x2d.shape
    bsz = m // seq
    cw = w_up_rwkv.shape[0]
    n_heads = fox_b_f.shape[0]
    lw = rw_w_lora_up.shape[0]
    assert 2 * lw == LANES and cw % LANES == 0 and C_TERMS * (cw // LANES) <= HEAD_DIM

    fox0, gate0, small0 = 4 * cw, 8 * cw, 8 * cw + 2 * d
    n_pad = small0 + 2 * LANES
    w_cat = _regroup_weights(w_in.T, cw, lw, n_heads, tiles["regroup_tr"])
    mu_cat = jnp.concatenate([
        rw_shift_mu[0:4 * cw], jnp.zeros((small0 - 4 * cw,), F32), rw_shift_mu[4 * cw:],
        jnp.zeros((LANES,), F32)]).reshape(1, n_pad)

    z = _inproj(x2d, norm_g, w_cat, mu_cat, seq, tiles["in_tm"], tiles["in_tn"])
    z3 = z.reshape(bsz, seq, n_pad)
    lora_blk = small0 // (2 * LANES)

    row = lambda a: a.reshape(1, cw)
    wup = jnp.pad(rw_w_lora_up, ((0, lw), (0, 0))).astype(BF16)
    aup = jnp.pad(rw_a_lora_up, ((lw, 0), (0, 0))).astype(BF16)
    y_rw = _rwkv(z3, lora_blk, wup, aup, row(rw_w0), row(rw_a0), row(rw_k_k), row(rw_k_a), row(rw_r_k),
                 row(rw_ln_g), row(rw_ln_b), cw)

    bf_pad = jnp.pad(fox_b_f, (0, LANES - n_heads)).reshape(1, LANES)
    c_sp = _fcum(z3, lora_blk, bf_pad, n_heads, tiles["fc_tc"])
    y_fox = _fox(z3, c_sp, fox0, cw, tiles["fox_tq"])

    x1 = _merge(y_rw.reshape(m, cw), y_fox.reshape(m, cw), z, gate0, x2d, w_up_rwkv.astype(BF16),
                w_up_fox.astype(BF16), w_out.astype(BF16), tiles["merge_tm"])
    return _ple(x1, p2d, ple_norm_g, final_norm_g, ple_proj.astype(BF16), ple_gate_w.astype(BF16),
                tiles["ple_tm"])


_TILES = dict(regroup_tr=256, in_tm=512, in_tn=1792, fc_tc=256, fox_tq=512, merge_tm=256, ple_tm=256)


def kernel(x, p, norm_g, w_in, rw_shift_mu, rw_w0, rw_w_lora_up, rw_a0, rw_a_lora_up, rw_k_k, rw_k_a, rw_r_k,
           rw_ln_g, rw_ln_b, fox_b_f, w_up_rwkv, w_up_fox, w_out, ple_proj, ple_gate_w, ple_norm_g,
           final_norm_g):
    bsz, seq, d = x.shape
    depth = p.shape[0]
    assert depth == 1, "the final norm is fused into the (single) layer"
    out = _layer(x.reshape(bsz * seq, d), p[0].reshape(bsz * seq, -1), seq, norm_g[0], w_in[0], rw_shift_mu[0],
                 rw_w0[0], rw_w_lora_up[0], rw_a0[0], rw_a_lora_up[0], rw_k_k[0], rw_k_a[0],
                 rw_r_k[0].reshape(-1), rw_ln_g[0], rw_ln_b[0], fox_b_f[0], w_up_rwkv[0], w_up_fox[0], w_out[0],
                 ple_proj[0], ple_gate_w[0], ple_norm_g[0], final_norm_g, tiles=_TILES)
    return out.reshape(bsz, seq, d)
```

```python
import functools
import math

import jax
import jax.numpy as jnp
from jax import lax
from jax.experimental import pallas as pl
from jax.experimental.pallas import tpu as pltpu

F32 = jnp.float32
BF16 = jnp.bfloat16

HEAD_DIM = 64
LANES = 128
NORM_EPS = 1e-6
GN_EPS = 64e-5
Q_SCALE = HEAD_DIM ** -0.5
EXP_NEG_HALF = math.exp(-0.5)
LOG2_E = math.log2(math.e)
NEG_BIG = -1e30
VMEM_LIMIT = 56 * 1024 * 1024

SPLIT_TERMS = 2
NORM_TERMS = 1
RW_CHUNK = 64
RW_CHUNKS_PER_STEP = 4
C_TERMS = 3
VT_ROWS = HEAD_DIM + 16


def _dot(a, b):
    return jnp.dot(a, b, preferred_element_type=F32)


def _dot_nt(a, b):
    return lax.dot_general(a, b, (((1,), (1,)), ((), ())), preferred_element_type=F32)


def _dot_tn(a, b):
    return lax.dot_general(a, b, (((0,), (0,)), ((), ())), preferred_element_type=F32)


def _split_bf16(x, n):
    parts = []
    rem = x
    for i in range(n):
        p = rem.astype(BF16)
        parts.append(p)
        if i + 1 < n:
            rem = rem - p.astype(F32)
    return parts


def _sigmoid(x):
    return 1.0 / (1.0 + jnp.exp(-x))


def _inproj_kernel(x_ref, g_ref, w_ref, mu_ref, o_ref, carry_ref, *, tiles_per_seq):
    x = x_ref[...]
    rstd = lax.rsqrt(jnp.mean(x * x, axis=-1, keepdims=True) + NORM_EPS)
    z = _dot((x * g_ref[...]).astype(BF16), w_ref[...]) * rstd
    tm = z.shape[0]
    first = (pl.program_id(1) % tiles_per_seq) == 0
    prev_last = jnp.where(first, 0.0, carry_ref[7:8, :])
    row = lax.broadcasted_iota(jnp.int32, z.shape, 0)
    z_prev = jnp.where(row == 0, prev_last, pltpu.roll(z, 1, 0))
    o_ref[...] = z + (z_prev - z) * mu_ref[...]
    carry_ref[...] = z[tm - 8:tm, :]


def _inproj(x, g, w, mu, seq, tm, tn):
    m, d = x.shape
    n = w.shape[1]
    return pl.pallas_call(
        functools.partial(_inproj_kernel, tiles_per_seq=seq // tm),
        grid=(n // tn, m // tm),
        in_specs=[pl.BlockSpec((tm, d), lambda j, i: (i, 0)),
                  pl.BlockSpec((1, d), lambda j, i: (0, 0)),
                  pl.BlockSpec((d, tn), lambda j, i: (0, j)),
                  pl.BlockSpec((1, tn), lambda j, i: (0, j))],
        out_specs=pl.BlockSpec((tm, tn), lambda j, i: (i, j)),
        out_shape=jax.ShapeDtypeStruct((m, n), F32),
        scratch_shapes=[pltpu.VMEM((8, tn), F32)],
        compiler_params=pltpu.CompilerParams(dimension_semantics=("arbitrary", "arbitrary"),
                                             vmem_limit_bytes=VMEM_LIMIT),
        name="inproj",
    )(x, g.reshape(1, d), w, mu)


def _rwkv_chunk(r, k, v, gate, lora_b, lora_t, wup, aup, w0, a0, kkp, kap, rkp, lng, lnb, s_old, *, c):
    rows, width = r.shape
    n_pairs = width // LANES
    pairs = range(n_pairs)
    subs = range(rows // c)
    blocks = [(i, p) for i in subs for p in pairs]
    cols = lambda x: [x[i * c:(i + 1) * c, p * LANES:(p + 1) * LANES] for i, p in blocks]
    lane = lax.broadcasted_iota(jnp.int32, (1, LANES), 1)
    m0 = lane < HEAD_DIM

    ri = lax.broadcasted_iota(jnp.int32, (LANES, LANES), 0)
    ci = lax.broadcasted_iota(jnp.int32, (LANES, LANES), 1)
    same_head = (ri // HEAD_DIM) == (ci // HEAD_DIM)
    ones_bd = jnp.where(same_head, 1.0, 0.0).astype(BF16)

    def segsum(x, terms):
        xr = jnp.concatenate(cols(x), axis=0)
        sr = sum(_dot(part, ones_bd) for part in _split_bf16(xr, terms))
        return jnp.concatenate([jnp.concatenate([sr[(i * n_pairs + p) * c:(i * n_pairs + p + 1) * c] for p in pairs],
                                                axis=1) for i in subs], axis=0)

    w_raw = w0 + _dot(lora_t, wup)
    logw = -EXP_NEG_HALF * _sigmoid(w_raw)
    a = _sigmoid(a0 + _dot(lora_b, aup))
    kkr = k * kkp
    kk = kkr / jnp.maximum(jnp.sqrt(segsum(kkr * kkr, NORM_TERMS)), 1e-12)
    b = kk * a
    k2 = k * (1.0 + (a - 1.0) * kap)

    ti = lax.broadcasted_iota(jnp.int32, (rows, rows), 0)
    si = lax.broadcasted_iota(jnp.int32, (rows, rows), 1)
    tri = jnp.where((si <= ti) & (si // c == ti // c), 1.0, 0.0).astype(BF16)
    g = sum(_dot(tri, part) for part in _split_bf16(logw, SPLIT_TERMS))
    g_ends = [g[(i + 1) * c - 1:(i + 1) * c, :] for i in subs]
    g_last = jnp.concatenate([jnp.broadcast_to(ge, (c, width)) for ge in g_ends], axis=0)
    egd = jnp.exp(g_last - g)
    eng = jnp.exp(-g)
    rg = cols(r * jnp.exp(g))
    kkg = cols(kk * jnp.exp(g - logw))
    kn = cols(k2 * eng)
    bn = cols(b * eng)
    kd = cols(k2 * egd)
    bdec = cols(b * egd)
    vs = cols(v)
    decay_c = [jnp.exp(g_ends[i][:, p * LANES:(p + 1) * LANES]) for i, p in blocks]
    nb = range(len(blocks))

    def stack(x):
        return jnp.concatenate([jnp.where(m0, x, 0.0), jnp.where(m0, 0.0, x)], axis=0).astype(BF16)

    t_in = lax.broadcasted_iota(jnp.int32, (c, LANES), 0)
    s_in = lax.broadcasted_iota(jnp.int32, (c, LANES), 1) % HEAD_DIM
    strict = s_in < t_in
    incl = s_in <= t_in
    eye = jnp.where(s_in == t_in, 1.0, 0.0)

    lhs_kr = [jnp.concatenate([kkg[b], rg[b]], axis=0).astype(BF16) for b in nb]
    rhs_bk = [jnp.concatenate([stack(bn[b]), stack(kn[b])], axis=0) for b in nb]
    aa = [_dot_nt(lhs_kr[b], rhs_bk[b]) for b in nb]
    a_kb = [jnp.where(strict, aa[b][0:c, 0:LANES], 0.0) for b in nb]
    a_kk = [jnp.where(strict, aa[b][0:c, LANES:2 * LANES], 0.0).astype(BF16) for b in nb]
    a_rb = [jnp.where(incl, aa[b][c:2 * c, 0:LANES], 0.0) for b in nb]
    a_rk = [jnp.where(incl, aa[b][c:2 * c, LANES:2 * LANES], 0.0) for b in nb]
    a_r = [jnp.concatenate([a_rk[b], a_rb[b]], axis=1).astype(BF16) for b in nb]
    v_st = [stack(vs[b]) for b in nb]
    kb = [jnp.concatenate([kd[b], bdec[b]], axis=0).astype(BF16) for b in nb]
    av = [_dot(a_kk[b], v_st[b]) for b in nb]

    tinv = [eye - a_kb[b] for b in nb]
    lpow = [_dot(a_kb[b].astype(BF16), stack(a_kb[b])) for b in nb]
    n_sq = int(math.log2(c)) - 1
    for i in range(n_sq):
        last = i + 1 == n_sq
        w_pow = [stack(lpow[b]) for b in nb]
        lhs = [tinv[b].astype(BF16) if last else jnp.concatenate([tinv[b], lpow[b]], axis=0).astype(BF16)
               for b in nb]
        prod = [_dot(lhs[b], w_pow[b]) for b in nb]
        tinv = [tinv[b] + prod[b][0:c] for b in nb]
        if not last:
            lpow = [prod[b][c:2 * c] for b in nb]
    tinv = [t.astype(BF16) for t in tinv]

    s_cur = list(s_old)
    ys = []
    for i in subs:
        bi = [i * n_pairs + p for p in pairs]
        s_b = [s_cur[p].astype(BF16) for p in pairs]
        sh = [_dot_nt(lhs_kr[bi[p]], s_b[p]) for p in pairs]
        x = [sh[p][0:c] + av[bi[p]] for p in pairs]
        u = [_dot(tinv[bi[p]], stack(x[p])) for p in pairs]
        vu_st = [jnp.concatenate([v_st[bi[p]], stack(-u[p])], axis=0) for p in pairs]
        ys.append(jnp.concatenate([sh[p][c:2 * c] + _dot(a_r[bi[p]], vu_st[p]) for p in pairs], axis=1))
        vu = [jnp.concatenate([vs[bi[p]], -u[p]], axis=0).astype(BF16) for p in pairs]
        s_cur = [s_cur[p] * decay_c[bi[p]] + jnp.where(same_head, _dot_tn(vu[p], kb[bi[p]]), 0.0) for p in pairs]
    y = jnp.concatenate(ys, axis=0)
    s_new = s_cur

    inv_n = 1.0 / HEAD_DIM
    mean = segsum(y, NORM_TERMS) * inv_n
    dlt = y - mean
    var = segsum(dlt * dlt, NORM_TERMS) * inv_n
    yn = dlt * lax.rsqrt(var + GN_EPS) * lng + lnb
    bonus = segsum(r * k2 * rkp, SPLIT_TERMS) * v
    out = (yn + bonus) * (gate * _sigmoid(gate))
    return out, s_new


def _rwkv_kernel(r_ref, k_ref, v_ref, g_ref, lora_ref, wup_ref, aup_ref, w0_ref, a0_ref, kk_ref, ka_ref,
                 rk_ref, lng_ref, lnb_ref, o_ref, s_ref):
    @pl.when(pl.program_id(1) == 0)
    def _():
        s_ref[...] = jnp.zeros_like(s_ref)

    lora = lora_ref[0][:, 0:LANES]
    out, s_new = _rwkv_chunk(
        r_ref[0], k_ref[0], v_ref[0], g_ref[0], lora.astype(BF16), jnp.tanh(lora).astype(BF16),
        wup_ref[...], aup_ref[...], w0_ref[...], a0_ref[...], kk_ref[...], ka_ref[...], rk_ref[...],
        lng_ref[...], lnb_ref[...], [s_ref[p] for p in range(s_ref.shape[0])], c=RW_CHUNK)
    for p in range(s_ref.shape[0]):
        s_ref[p] = s_new[p]
    o_ref[0] = out.astype(o_ref.dtype)


def _rwkv(z3, lora_blk, wup, aup, w0, a0, k_k, k_a, r_k, ln_g, ln_b, width):
    bsz, seq, _ = z3.shape
    c = min(seq, RW_CHUNK * RW_CHUNKS_PER_STEP)
    assert width % LANES == 0
    zspec = lambda j: pl.BlockSpec((1, c, width), lambda b, t, j=j: (b, t, j))
    pspec = lambda rows: pl.BlockSpec((rows, width), lambda b, t: (0, 0))
    return pl.pallas_call(
        _rwkv_kernel,
        grid=(bsz, seq // c),
        in_specs=[zspec(0), zspec(1), zspec(2), zspec(3),
                  pl.BlockSpec((1, c, 2 * LANES), lambda b, t: (b, t, lora_blk)),
                  pspec(LANES), pspec(LANES)] + [pspec(1)] * 7,
        out_specs=pl.BlockSpec((1, c, width), lambda b, t: (b, t, 0)),
        out_shape=jax.ShapeDtypeStruct((bsz, seq, width), BF16),
        scratch_shapes=[pltpu.VMEM((width // LANES, LANES, LANES), F32)],
        compiler_params=pltpu.CompilerParams(dimension_semantics=("arbitrary", "arbitrary"),
                                             vmem_limit_bytes=VMEM_LIMIT),
        name="rwkv7_scan",
    )(z3, z3, z3, z3, z3, wup, aup, w0, a0, k_k, k_a, r_k, ln_g, ln_b)


def _c_lane(pair, h):
    return (1 - h) * HEAD_DIM + C_TERMS * pair


def _fcum_kernel(l_ref, bf_ref, o_ref, carry_ref, *, n_heads):
    @pl.when(pl.program_id(1) == 0)
    def _():
        carry_ref[...] = jnp.zeros_like(carry_ref)

    x = l_ref[0] + bf_ref[...]
    log_f = jnp.minimum(x, 0.0) - jnp.log(1.0 + jnp.exp(-jnp.abs(x)))
    tc = x.shape[0]
    ti = lax.broadcasted_iota(jnp.int32, (tc, tc), 0)
    si = lax.broadcasted_iota(jnp.int32, (tc, tc), 1)
    tril = jnp.where(si <= ti, 1.0, 0.0).astype(BF16)
    c = sum(_dot(tril, part) for part in _split_bf16(log_f, 3)) + carry_ref[0:1, :]
    carry_ref[...] = jnp.broadcast_to(c[tc - 1:tc, :], carry_ref.shape)
    lin = lax.broadcasted_iota(jnp.int32, (LANES, LANES), 0)
    lout = lax.broadcasted_iota(jnp.int32, (LANES, LANES), 1)
    terms = _split_bf16(-LOG2_E * c, C_TERMS)
    sel = [jnp.where((lin < n_heads) & (lout == _c_lane(lin // 2, lin % 2) + i), 1.0, 0.0).astype(BF16)
           for i in range(C_TERMS)]
    o_ref[0] = sum(_dot(terms[i], sel[i]) for i in range(C_TERMS)).astype(o_ref.dtype)


def _fcum(z3, lora_blk, bf_pad, n_heads, tc):
    bsz, seq, _ = z3.shape
    return pl.pallas_call(
        functools.partial(_fcum_kernel, n_heads=n_heads),
        grid=(bsz, seq // tc),
        in_specs=[pl.BlockSpec((1, tc, LANES), lambda b, t: (b, t, 2 * lora_blk + 1)),
                  pl.BlockSpec((1, LANES), lambda b, t: (0, 0))],
        out_specs=pl.BlockSpec((1, tc, LANES), lambda b, t: (b, t, 0)),
        out_shape=jax.ShapeDtypeStruct((bsz, seq, LANES), BF16),
        scratch_shapes=[pltpu.VMEM((8, LANES), F32)],
        compiler_params=pltpu.CompilerParams(dimension_semantics=("arbitrary", "arbitrary"),
                                             vmem_limit_bytes=VMEM_LIMIT),
        name="fox_cumsum",
    )(z3, bf_pad)


def _fox_kernel(q_ref, k_ref, v_ref, g_ref, c_ref, o_ref, vt_ref, qt_ref, m_ref, acc_ref, sa_ref, sb_ref, *, tq, tqs,
                t_chunk):
    pair = pl.program_id(1)
    seq = k_ref.shape[1]
    nq = seq // tq
    lane = lax.broadcasted_iota(jnp.int32, (1, LANES), 1)
    row = lax.broadcasted_iota(jnp.int32, (LANES, 1), 0)

    one_row = jnp.where(lax.broadcasted_iota(jnp.int32, (VT_ROWS - HEAD_DIM, t_chunk), 0) == 0, 1.0, 0.0)
    for t0 in range(0, seq, t_chunk):
        v_t = v_ref[0, t0:t0 + t_chunk, :].T
        q_t = (q_ref[0, t0:t0 + t_chunk, :] * (Q_SCALE * LOG2_E)).T
        for h in range(2):
            vt_ref[h, 0:HEAD_DIM, t0:t0 + t_chunk] = v_t[h * HEAD_DIM:(h + 1) * HEAD_DIM].astype(BF16)
            vt_ref[h, HEAD_DIM:VT_ROWS, t0:t0 + t_chunk] = one_row.astype(BF16)
            r0 = _c_lane(pair, h)
            ones = jnp.where((row >= r0) & (row < r0 + C_TERMS), 1.0, 0.0)
            qt_ref[h, :, t0:t0 + t_chunk] = jnp.where((row // HEAD_DIM) == h, q_t, ones).astype(BF16)
    acc_ref[...] = jnp.zeros_like(acc_ref)
    m_ref[...] = jnp.full_like(m_ref, NEG_BIG)

    chains = [(h, slice(q0, q0 + tqs)) for h in range(2) for q0 in range(0, tq, tqs)]

    def scores(qi, j, dst_ref):
        ks = pl.multiple_of(j * tq, tq)
        kb = k_ref[0, pl.ds(ks, tq), :].astype(BF16)
        cb = c_ref[0, pl.ds(ks, tq), :]
        ka = [jnp.where((lane // HEAD_DIM) == h, kb, cb) for h in range(2)]
        for i, (h, qs) in enumerate(chains):
            dst_ref[i] = _dot(ka[h], qt_ref[h, :, pl.ds(pl.multiple_of(qi * tq + qs.start, tqs), tqs)])

    def update(src_ref, qi, j, diagonal):
        ks = pl.multiple_of(j * tq, tq)
        nk = [qs.stop if diagonal else tq for h, qs in chains]
        s = [src_ref[i, 0:nk[i], :] for i in range(len(chains))]
        if diagonal:
            s = [jnp.where(lax.broadcasted_iota(jnp.int32, (nk[i], tqs), 0)
                           <= lax.broadcasted_iota(jnp.int32, (nk[i], tqs), 1) + qs.start, s[i], NEG_BIG)
                 for i, (h, qs) in enumerate(chains)]
        m_old = [jnp.where(j == 0, NEG_BIG, m_ref[h, :, qs]) for h, qs in chains]
        m_new = [jnp.maximum(m_old[i], jnp.max(s[i], axis=0, keepdims=True)) for i in range(len(chains))]
        alpha = [jnp.exp2(m_old[i] - m_new[i]) for i in range(len(chains))]
        pexp = [jnp.exp2(s[i] - m_new[i]).astype(BF16) for i in range(len(chains))]
        pv = [_dot(vt_ref[h, :, pl.ds(ks, nk[i])], pexp[i]) for i, (h, qs) in enumerate(chains)]
        for i, (h, qs) in enumerate(chains):
            acc_ref[h, :, qs] = alpha[i] * acc_ref[h, :, qs] + pv[i]
            m_ref[h, :, qs] = m_new[i]
        if diagonal:
            o_t = jnp.concatenate([acc_ref[h, 0:HEAD_DIM, :] / acc_ref[h, HEAD_DIM:HEAD_DIM + 1, :]
                                   for h in range(2)], axis=0)
            q0 = pl.multiple_of(qi * tq, tq)
            gate = g_ref[0, pl.ds(q0, tq), :]
            o_ref[0, pl.ds(q0, tq), :] = (o_t.T * (gate * _sigmoid(gate))).astype(o_ref.dtype)

    def following(qi, j):
        wrap = j == qi
        return jnp.where(wrap, qi + 1, qi), jnp.where(wrap, 0, j + 1)

    def half_step(cur_ref, nxt_ref, qi, j):
        qn, jn = following(qi, j)
        qn_c = jnp.minimum(qn, nq - 1)

        @pl.when(j < qi)
        def _():
            scores(qn_c, jn, nxt_ref)
            update(cur_ref, qi, j, False)

        @pl.when(j == qi)
        def _():
            scores(qn_c, jn, nxt_ref)
            update(cur_ref, qi, j, True)
        return qn, jn

    scores(0, 0, sa_ref)

    def body(n, carry):
        qi, j = carry
        qi, j = half_step(sa_ref, sb_ref, qi, j)
        return half_step(sb_ref, sa_ref, qi, j)

    lax.fori_loop(0, nq * (nq + 1) // 4, body, (jnp.int32(0), jnp.int32(0)))


def _fox(z3, c_sp, col0, width, tq):
    bsz, seq, _ = z3.shape
    n_pairs = width // LANES
    nq = seq // tq
    assert (nq * (nq + 1) // 2) % 2 == 0
    blk = lambda j: col0 // LANES + j * n_pairs
    tqs = min(tq, 2 * LANES)
    s_buf = pltpu.VMEM((2 * tq // tqs, tq, tqs), F32)
    zspec = lambda j: pl.BlockSpec((1, seq, LANES), lambda b, p, j=j: (b, 0, blk(j) + p))
    return pl.pallas_call(
        functools.partial(_fox_kernel, tq=tq, tqs=tqs, t_chunk=min(seq, 512)),
        grid=(bsz, n_pairs),
        in_specs=[zspec(0), zspec(1), zspec(2), zspec(3),
                  pl.BlockSpec((1, seq, LANES), lambda b, p: (b, 0, 0))],
        out_specs=pl.BlockSpec((1, seq, LANES), lambda b, p: (b, 0, p)),
        out_shape=jax.ShapeDtypeStruct((bsz, seq, width), BF16),
        scratch_shapes=[pltpu.VMEM((2, VT_ROWS, seq), BF16), pltpu.VMEM((2, LANES, seq), BF16),
                        pltpu.VMEM((2, 1, tq), F32), pltpu.VMEM((2, VT_ROWS, tq), F32), s_buf, s_buf],
        compiler_params=pltpu.CompilerParams(dimension_semantics=("arbitrary",) * 2,
                                             vmem_limit_bytes=VMEM_LIMIT),
        name="fox_attention",
    )(z3, z3, z3, z3, c_sp)


def _merge_kernel(yr_ref, yf_ref, gr_ref, gf_ref, x_ref, wr_ref, wf_ref, wo_ref, o_ref):
    u_rw = _dot(yr_ref[...], wr_ref[...])
    u_fox = _dot(yf_ref[...], wf_ref[...])
    merged = _sigmoid(gr_ref[...]) * u_rw + _sigmoid(gf_ref[...]) * u_fox
    o_ref[...] = x_ref[...] + _dot(merged.astype(BF16), wo_ref[...])


def _merge(y_rw, y_fox, z, gate_col0, x, w_up_rw, w_up_fox, w_out, tm):
    m, d = x.shape
    cw = y_rw.shape[1]
    gblk = gate_col0 // d
    const = lambda shape: pl.BlockSpec(shape, lambda i: (0, 0), pipeline_mode=pl.Buffered(1))
    return pl.pallas_call(
        _merge_kernel,
        grid=(m // tm,),
        in_specs=[pl.BlockSpec((tm, cw), lambda i: (i, 0)),
                  pl.BlockSpec((tm, cw), lambda i: (i, 0)),
                  pl.BlockSpec((tm, d), lambda i: (i, gblk)),
                  pl.BlockSpec((tm, d), lambda i: (i, gblk + 1)),
                  pl.BlockSpec((tm, d), lambda i: (i, 0)),
                  const((cw, d)), const((cw, d)), const((d, d))],
        out_specs=pl.BlockSpec((tm, d), lambda i: (i, 0)),
        out_shape=jax.ShapeDtypeStruct((m, d), F32),
        compiler_params=pltpu.CompilerParams(dimension_semantics=("arbitrary",),
                                             vmem_limit_bytes=VMEM_LIMIT),
        name="merge_outproj",
    )(y_rw, y_fox, z, z, x, w_up_rw, w_up_fox, w_out)


def _ple_kernel(x_ref, p_ref, ng_ref, fg_ref, wp_ref, wg_ref, o_ref):
    x = x_ref[...]
    hn = x * lax.rsqrt(jnp.mean(x * x, axis=-1, keepdims=True) + NORM_EPS) * ng_ref[...]
    gate = _sigmoid(_dot(hn.astype(BF16), wg_ref[...]))
    ple = _dot(p_ref[...].astype(BF16), wp_ref[...])
    x2 = x + ple * gate
    o_ref[...] = x2 * lax.rsqrt(jnp.mean(x2 * x2, axis=-1, keepdims=True) + NORM_EPS) * fg_ref[...]


def _ple(x1, p, ple_norm_g, final_norm_g, ple_proj, ple_gate_w, tm):
    m, d = x1.shape
    pd = p.shape[1]
    const = lambda shape: pl.BlockSpec(shape, lambda i: (0, 0), pipeline_mode=pl.Buffered(1))
    return pl.pallas_call(
        _ple_kernel,
        grid=(m // tm,),
        in_specs=[pl.BlockSpec((tm, d), lambda i: (i, 0)),
                  pl.BlockSpec((tm, pd), lambda i: (i, 0)),
                  const((1, d)), const((1, d)), const((pd, d)), const((d, d))],
        out_specs=pl.BlockSpec((tm, d), lambda i: (i, 0)),
        out_shape=jax.ShapeDtypeStruct((m, d), F32),
        compiler_params=pltpu.CompilerParams(dimension_semantics=("arbitrary",),
                                             vmem_limit_bytes=VMEM_LIMIT),
        name="ple_final_norm",
    )(x1, p, ple_norm_g.reshape(1, d), final_norm_g.reshape(1, d), ple_proj, ple_gate_w)


def _regroup_kernel(main_ref, lora_ref, fl_ref, o_ref, *, n_heads):
    last = pl.num_programs(0) - 1

    @pl.when(pl.program_id(0) < last)
    def _():
        o_ref[...] = main_ref[...].T.astype(BF16)

    @pl.when(pl.program_id(0) == last)
    def _():
        row = lax.broadcasted_iota(jnp.int32, (LANES, 1), 0)
        o_ref[:, 0:LANES] = lora_ref[...].T.astype(BF16)
        o_ref[:, LANES:2 * LANES] = jnp.where(row < n_heads, fl_ref[...], 0.0).T.astype(BF16)


def _regroup_weights(w_t, cw, lw, n_heads, tr):
    n_in, d = w_t.shape
    fl0 = 8 * cw + 2 * lw
    small0 = 8 * cw + 2 * d
    n_pad = small0 + 2 * LANES
    assert tr == 2 * LANES and (4 * cw) % tr == 0 and (2 * d) % tr == 0 and n_in == fl0 + n_heads + 2 * d
    assert (2 * lw) % 8 == 0 and n_heads % 8 == 0

    def main_rows(i):
        r = i * tr
        src = jnp.where(r < 4 * cw, r, jnp.where(r < 8 * cw, r + 2 * lw,
                                                 jnp.where(r < small0, r + 2 * lw + n_heads, 0)))
        return pl.multiple_of(src, 8), 0

    rows = lambda n: (pl.Element(n), pl.Element(d))
    return pl.pallas_call(
        functools.partial(_regroup_kernel, n_heads=n_heads),
        grid=(n_pad // tr,),
        in_specs=[pl.BlockSpec(rows(tr), main_rows),
                  pl.BlockSpec(rows(LANES), lambda i: (4 * cw, 0)),
                  pl.BlockSpec(rows(LANES), lambda i: (fl0, 0))],
        out_specs=pl.BlockSpec((d, tr), lambda i: (0, i)),
        out_shape=jax.ShapeDtypeStruct((d, n_pad), BF16),
        compiler_params=pltpu.CompilerParams(dimension_semantics=("arbitrary",),
                                             vmem_limit_bytes=VMEM_LIMIT),
        name="regroup_w_in",
    )(w_t, w_t, w_t)


def _layer(x2d, p2d, seq, norm_g, w_in, rw_shift_mu, rw_w0, rw_w_lora_up, rw_a0, rw_a_lora_up, rw_k_k, rw_k_a,
           rw_r_k, rw_ln_g, rw_ln_b, fox_b_f, w_up_rwkv, w_up_fox, w_out, ple_proj, ple_gate_w, ple_norm_g,
           final_norm_g, *, tiles):
    m, d = x2d.shape
    bsz = m // seq
    cw = w_up_rwkv.shape[0]
    n_heads = fox_b_f.shape[0]
    lw = rw_w_lora_up.shape[0]
    assert 2 * lw == LANES and cw % LANES == 0 and C_TERMS * (cw // LANES) <= HEAD_DIM

    fox0, gate0, small0 = 4 * cw, 8 * cw, 8 * cw + 2 * d
    n_pad = small0 + 2 * LANES
    w_cat = _regroup_weights(w_in.T, cw, lw, n_heads, tiles["regroup_tr"])
    mu_cat = jnp.concatenate([
        rw_shift_mu[0:4 * cw], jnp.zeros((small0 - 4 * cw,), F32), rw_shift_mu[4 * cw:],
        jnp.zeros((LANES,), F32)]).reshape(1, n_pad)

    z = _inproj(x2d, norm_g, w_cat, mu_cat, seq, tiles["in_tm"], tiles["in_tn"])
    z3 = z.reshape(bsz, seq, n_pad)
    lora_blk = small0 // (2 * LANES)

    row = lambda a: a.reshape(1, cw)
    wup = jnp.pad(rw_w_lora_up, ((0, lw), (0, 0))).astype(BF16)
    aup = jnp.pad(rw_a_lora_up, ((lw, 0), (0, 0))).astype(BF16)
    y_rw = _rwkv(z3, lora_blk, wup, aup, row(rw_w0), row(rw_a0), row(rw_k_k), row(rw_k_a), row(rw_r_k),
                 row(rw_ln_g), row(rw_ln_b), cw)

    bf_pad = jnp.pad(fox_b_f, (0, LANES - n_heads)).reshape(1, LANES)
    c_sp = _fcum(z3, lora_blk, bf_pad, n_heads, tiles["fc_tc"])
    y_fox = _fox(z3, c_sp, fox0, cw, tiles["fox_tq"])

    x1 = _merge(y_rw.reshape(m, cw), y_fox.reshape(m, cw), z, gate0, x2d, w_up_rwkv.astype(BF16),
                w_up_fox.astype(BF16), w_out.astype(BF16), tiles["merge_tm"])
    return _ple(x1, p2d, ple_norm_g, final_norm_g, ple_proj.astype(BF16), ple_gate_w.astype(BF16),
                tiles["ple_tm"])


_TILES = dict(regroup_tr=256, in_tm=512, in_tn=1792, fc_tc=512, fox_tq=1024, merge_tm=256, ple_tm=256)


def kernel(x, p, norm_g, w_in, rw_shift_mu, rw_w0, rw_w_lora_up, rw_a0, rw_a_lora_up, rw_k_k, rw_k_a, rw_r_k,
           rw_ln_g, rw_ln_b, fox_b_f, w_up_rwkv, w_up_fox, w_out, ple_proj, ple_gate_w, ple_norm_g,
           final_norm_g):
    bsz, seq, d = x.shape
    depth = p.shape[0]
    assert depth == 1, "the final norm is fused into the (single) layer"
    out = _layer(x.reshape(bsz * seq, d), p[0].reshape(bsz * seq, -1), seq, norm_g[0], w_in[0], rw_shift_mu[0],
                 rw_w0[0], rw_w_lora_up[0], rw_a0[0], rw_a_lora_up[0], rw_k_k[0], rw_k_a[0],
                 rw_r_k[0].reshape(-1), rw_ln_g[0], rw_ln_b[0], fox_b_f[0], w_up_rwkv[0], w_up_fox[0], w_out[0],
                 ple_proj[0], ple_gate_w[0], ple_norm_g[0], final_norm_g, tiles=_TILES)
    return out.reshape(bsz, seq, d)
```

```python
import functools
import math

import jax
import jax.numpy as jnp
from jax import lax
from jax.experimental import pallas as pl
from jax.experimental.pallas import tpu as pltpu

F32 = jnp.float32
BF16 = jnp.bfloat16

HEAD_DIM = 64
LANES = 128
NORM_EPS = 1e-6
GN_EPS = 64e-5
Q_SCALE = HEAD_DIM ** -0.5
EXP_NEG_HALF = math.exp(-0.5)
LOG2_E = math.log2(math.e)
NEG_BIG = -1e30
VMEM_LIMIT = 56 * 1024 * 1024

SPLIT_TERMS = 2
NORM_TERMS = 1
RW_CHUNK = 64
RW_CHUNKS_PER_STEP = 4
C_TERMS = 3
VT_ROWS = HEAD_DIM + 16


def _dot(a, b):
    return jnp.dot(a, b, preferred_element_type=F32)


def _dot_nt(a, b):
    return lax.dot_general(a, b, (((1,), (1,)), ((), ())), preferred_element_type=F32)


def _dot_tn(a, b):
    return lax.dot_general(a, b, (((0,), (0,)), ((), ())), preferred_element_type=F32)


def _split_bf16(x, n):
    parts = []
    rem = x
    for i in range(n):
        p = rem.astype(BF16)
        parts.append(p)
        if i + 1 < n:
            rem = rem - p.astype(F32)
    return parts


def _sigmoid(x):
    return 1.0 / (1.0 + jnp.exp(-x))


def _inproj_kernel(x_ref, g_ref, w_ref, mu_ref, o_ref, carry_ref, *, tiles_per_seq):
    x = x_ref[...]
    rstd = lax.rsqrt(jnp.mean(x * x, axis=-1, keepdims=True) + NORM_EPS)
    z = _dot((x * g_ref[...]).astype(BF16), w_ref[...]) * rstd
    tm = z.shape[0]
    first = (pl.program_id(1) % tiles_per_seq) == 0
    prev_last = jnp.where(first, 0.0, carry_ref[7:8, :])
    row = lax.broadcasted_iota(jnp.int32, z.shape, 0)
    z_prev = jnp.where(row == 0, prev_last, pltpu.roll(z, 1, 0))
    o_ref[...] = z + (z_prev - z) * mu_ref[...]
    carry_ref[...] = z[tm - 8:tm, :]


def _inproj(x, g, w, mu, seq, tm, tn):
    m, d = x.shape
    n = w.shape[1]
    return pl.pallas_call(
        functools.partial(_inproj_kernel, tiles_per_seq=seq // tm),
        grid=(n // tn, m // tm),
        in_specs=[pl.BlockSpec((tm, d), lambda j, i: (i, 0)),
                  pl.BlockSpec((1, d), lambda j, i: (0, 0)),
                  pl.BlockSpec((d, tn), lambda j, i: (0, j)),
                  pl.BlockSpec((1, tn), lambda j, i: (0, j))],
        out_specs=pl.BlockSpec((tm, tn), lambda j, i: (i, j)),
        out_shape=jax.ShapeDtypeStruct((m, n), F32),
        scratch_shapes=[pltpu.VMEM((8, tn), F32)],
        compiler_params=pltpu.CompilerParams(dimension_semantics=("arbitrary", "arbitrary"),
                                             vmem_limit_bytes=VMEM_LIMIT),
        name="inproj",
    )(x, g.reshape(1, d), w, mu)


def _rwkv_chunk(r, k, v, gate, lora_b, lora_t, wup, aup, w0, a0, kkp, kap, rkp, lng, lnb, s_old, *, c):
    rows, width = r.shape
    n_pairs = width // LANES
    pairs = range(n_pairs)
    subs = range(rows // c)
    blocks = [(i, p) for i in subs for p in pairs]
    cols = lambda x: [x[i * c:(i + 1) * c, p * LANES:(p + 1) * LANES] for i, p in blocks]
    lane = lax.broadcasted_iota(jnp.int32, (1, LANES), 1)
    m0 = lane < HEAD_DIM

    ri = lax.broadcasted_iota(jnp.int32, (LANES, LANES), 0)
    ci = lax.broadcasted_iota(jnp.int32, (LANES, LANES), 1)
    same_head = (ri // HEAD_DIM) == (ci // HEAD_DIM)
    ones_bd = jnp.where(same_head, 1.0, 0.0).astype(BF16)

    def segsum(x, terms):
        xr = jnp.concatenate(cols(x), axis=0)
        sr = sum(_dot(part, ones_bd) for part in _split_bf16(xr, terms))
        return jnp.concatenate([jnp.concatenate([sr[(i * n_pairs + p) * c:(i * n_pairs + p + 1) * c] for p in pairs],
                                                axis=1) for i in subs], axis=0)

    w_raw = w0 + _dot(lora_t, wup)
    logw = -EXP_NEG_HALF * _sigmoid(w_raw)
    a = _sigmoid(a0 + _dot(lora_b, aup))
    kkr = k * kkp
    kk = kkr / jnp.maximum(jnp.sqrt(segsum(kkr * kkr, NORM_TERMS)), 1e-12)
    b = kk * a
    k2 = k * (1.0 + (a - 1.0) * kap)

    ti = lax.broadcasted_iota(jnp.int32, (rows, rows), 0)
    si = lax.broadcasted_iota(jnp.int32, (rows, rows), 1)
    tri = jnp.where((si <= ti) & (si // c == ti // c), 1.0, 0.0).astype(BF16)
    g = sum(_dot(tri, part) for part in _split_bf16(logw, SPLIT_TERMS))
    g_ends = [g[(i + 1) * c - 1:(i + 1) * c, :] for i in subs]
    g_last = jnp.concatenate([jnp.broadcast_to(ge, (c, width)) for ge in g_ends], axis=0)
    egd = jnp.exp(g_last - g)
    eng = jnp.exp(-g)
    rg = cols(r * jnp.exp(g))
    kkg = cols(kk * jnp.exp(g - logw))
    kn = cols(k2 * eng)
    bn = cols(b * eng)
    kd = cols(k2 * egd)
    bdec = cols(b * egd)
    vs = cols(v)
    decay_c = [jnp.exp(g_ends[i][:, p * LANES:(p + 1) * LANES]) for i, p in blocks]
    nb = range(len(blocks))

    def stack(x):
        return jnp.concatenate([jnp.where(m0, x, 0.0), jnp.where(m0, 0.0, x)], axis=0).astype(BF16)

    t_in = lax.broadcasted_iota(jnp.int32, (c, LANES), 0)
    s_in = lax.broadcasted_iota(jnp.int32, (c, LANES), 1) % HEAD_DIM
    strict = s_in < t_in
    incl = s_in <= t_in
    eye = jnp.where(s_in == t_in, 1.0, 0.0)

    lhs_kr = [jnp.concatenate([kkg[b], rg[b]], axis=0).astype(BF16) for b in nb]
    rhs_bk = [jnp.concatenate([stack(bn[b]), stack(kn[b])], axis=0) for b in nb]
    aa = [_dot_nt(lhs_kr[b], rhs_bk[b]) for b in nb]
    a_kb = [jnp.where(strict, aa[b][0:c, 0:LANES], 0.0) for b in nb]
    a_kk = [jnp.where(strict, aa[b][0:c, LANES:2 * LANES], 0.0).astype(BF16) for b in nb]
    a_rb = [jnp.where(incl, aa[b][c:2 * c, 0:LANES], 0.0) for b in nb]
    a_rk = [jnp.where(incl, aa[b][c:2 * c, LANES:2 * LANES], 0.0) for b in nb]
    a_r = [jnp.concatenate([a_rk[b], a_rb[b]], axis=1).astype(BF16) for b in nb]
    v_st = [stack(vs[b]) for b in nb]
    kb = [jnp.concatenate([kd[b], bdec[b]], axis=0).astype(BF16) for b in nb]
    av = [_dot(a_kk[b], v_st[b]) for b in nb]

    tinv = [eye - a_kb[b] for b in nb]
    lpow = [_dot(a_kb[b].astype(BF16), stack(a_kb[b])) for b in nb]
    n_sq = int(math.log2(c)) - 1
    for i in range(n_sq):
        last = i + 1 == n_sq
        w_pow = [stack(lpow[b]) for b in nb]
        lhs = [tinv[b].astype(BF16) if last else jnp.concatenate([tinv[b], lpow[b]], axis=0).astype(BF16)
               for b in nb]
        prod = [_dot(lhs[b], w_pow[b]) for b in nb]
        tinv = [tinv[b] + prod[b][0:c] for b in nb]
        if not last:
            lpow = [prod[b][c:2 * c] for b in nb]
    tinv = [t.astype(BF16) for t in tinv]

    s_cur = list(s_old)
    ys = []
    for i in subs:
        bi = [i * n_pairs + p for p in pairs]
        s_b = [s_cur[p].astype(BF16) for p in pairs]
        sh = [_dot_nt(lhs_kr[bi[p]], s_b[p]) for p in pairs]
        x = [sh[p][0:c] + av[bi[p]] for p in pairs]
        u = [_dot(tinv[bi[p]], stack(x[p])) for p in pairs]
        vu_st = [jnp.concatenate([v_st[bi[p]], stack(-u[p])], axis=0) for p in pairs]
        ys.append(jnp.concatenate([sh[p][c:2 * c] + _dot(a_r[bi[p]], vu_st[p]) for p in pairs], axis=1))
        vu = [jnp.concatenate([vs[bi[p]], -u[p]], axis=0).astype(BF16) for p in pairs]
        s_cur = [s_cur[p] * decay_c[bi[p]] + jnp.where(same_head, _dot_tn(vu[p], kb[bi[p]]), 0.0) for p in pairs]
    y = jnp.concatenate(ys, axis=0)
    s_new = s_cur

    inv_n = 1.0 / HEAD_DIM
    mean = segsum(y, NORM_TERMS) * inv_n
    dlt = y - mean
    var = segsum(dlt * dlt, NORM_TERMS) * inv_n
    yn = dlt * lax.rsqrt(var + GN_EPS) * lng + lnb
    bonus = segsum(r * k2 * rkp, SPLIT_TERMS) * v
    out = (yn + bonus) * (gate * _sigmoid(gate))
    return out, s_new


def _rwkv_kernel(r_ref, k_ref, v_ref, g_ref, lora_ref, wup_ref, aup_ref, w0_ref, a0_ref, kk_ref, ka_ref,
                 rk_ref, lng_ref, lnb_ref, o_ref, s_ref):
    @pl.when(pl.program_id(1) == 0)
    def _():
        s_ref[...] = jnp.zeros_like(s_ref)

    lora = lora_ref[0][:, 0:LANES]
    out, s_new = _rwkv_chunk(
        r_ref[0], k_ref[0], v_ref[0], g_ref[0], lora.astype(BF16), jnp.tanh(lora).astype(BF16),
        wup_ref[...], aup_ref[...], w0_ref[...], a0_ref[...], kk_ref[...], ka_ref[...], rk_ref[...],
        lng_ref[...], lnb_ref[...], [s_ref[p] for p in range(s_ref.shape[0])], c=RW_CHUNK)
    for p in range(s_ref.shape[0]):
        s_ref[p] = s_new[p]
    o_ref[0] = out.astype(o_ref.dtype)


def _rwkv(z3, lora_blk, wup, aup, w0, a0, k_k, k_a, r_k, ln_g, ln_b, width):
    bsz, seq, _ = z3.shape
    c = min(seq, RW_CHUNK * RW_CHUNKS_PER_STEP)
    assert width % LANES == 0
    zspec = lambda j: pl.BlockSpec((1, c, width), lambda b, t, j=j: (b, t, j))
    pspec = lambda rows: pl.BlockSpec((rows, width), lambda b, t: (0, 0))
    return pl.pallas_call(
        _rwkv_kernel,
        grid=(bsz, seq // c),
        in_specs=[zspec(0), zspec(1), zspec(2), zspec(3),
                  pl.BlockSpec((1, c, 2 * LANES), lambda b, t: (b, t, lora_blk)),
                  pspec(LANES), pspec(LANES)] + [pspec(1)] * 7,
        out_specs=pl.BlockSpec((1, c, width), lambda b, t: (b, t, 0)),
        out_shape=jax.ShapeDtypeStruct((bsz, seq, width), BF16),
        scratch_shapes=[pltpu.VMEM((width // LANES, LANES, LANES), F32)],
        compiler_params=pltpu.CompilerParams(dimension_semantics=("arbitrary", "arbitrary"),
                                             vmem_limit_bytes=VMEM_LIMIT),
        name="rwkv7_scan",
    )(z3, z3, z3, z3, z3, wup, aup, w0, a0, k_k, k_a, r_k, ln_g, ln_b)


def _c_lane(pair, h):
    return (1 - h) * HEAD_DIM + C_TERMS * pair


def _fcum_kernel(l_ref, bf_ref, o_ref, carry_ref, *, n_heads):
    @pl.when(pl.program_id(1) == 0)
    def _():
        carry_ref[...] = jnp.zeros_like(carry_ref)

    x = l_ref[0] + bf_ref[...]
    log_f = jnp.minimum(x, 0.0) - jnp.log(1.0 + jnp.exp(-jnp.abs(x)))
    tc = x.shape[0]
    ti = lax.broadcasted_iota(jnp.int32, (tc, tc), 0)
    si = lax.broadcasted_iota(jnp.int32, (tc, tc), 1)
    tril = jnp.where(si <= ti, 1.0, 0.0).astype(BF16)
    c = sum(_dot(tril, part) for part in _split_bf16(log_f, 3)) + carry_ref[0:1, :]
    carry_ref[...] = jnp.broadcast_to(c[tc - 1:tc, :], carry_ref.shape)
    lin = lax.broadcasted_iota(jnp.int32, (LANES, LANES), 0)
    lout = lax.broadcasted_iota(jnp.int32, (LANES, LANES), 1)
    terms = _split_bf16(-LOG2_E * c, C_TERMS)
    sel = [jnp.where((lin < n_heads) & (lout == _c_lane(lin // 2, lin % 2) + i), 1.0, 0.0).astype(BF16)
           for i in range(C_TERMS)]
    o_ref[0] = sum(_dot(terms[i], sel[i]) for i in range(C_TERMS)).astype(o_ref.dtype)


def _fcum(z3, lora_blk, bf_pad, n_heads, tc):
    bsz, seq, _ = z3.shape
    return pl.pallas_call(
        functools.partial(_fcum_kernel, n_heads=n_heads),
        grid=(bsz, seq // tc),
        in_specs=[pl.BlockSpec((1, tc, LANES), lambda b, t: (b, t, 2 * lora_blk + 1)),
                  pl.BlockSpec((1, LANES), lambda b, t: (0, 0))],
        out_specs=pl.BlockSpec((1, tc, LANES), lambda b, t: (b, t, 0)),
        out_shape=jax.ShapeDtypeStruct((bsz, seq, LANES), BF16),
        scratch_shapes=[pltpu.VMEM((8, LANES), F32)],
        compiler_params=pltpu.CompilerParams(dimension_semantics=("arbitrary", "arbitrary"),
                                             vmem_limit_bytes=VMEM_LIMIT),
        name="fox_cumsum",
    )(z3, bf_pad)


def _fox_kernel(q_ref, k_ref, v_ref, g_ref, c_ref, o_ref, vt_ref, qt_ref, m_ref, acc_ref, sa_ref, sb_ref, *, tq, tqs,
                t_chunk):
    pair = pl.program_id(1)
    seq = k_ref.shape[1]
    nq = seq // tq
    lane = lax.broadcasted_iota(jnp.int32, (1, LANES), 1)
    row = lax.broadcasted_iota(jnp.int32, (LANES, 1), 0)

    one_row = jnp.where(lax.broadcasted_iota(jnp.int32, (VT_ROWS - HEAD_DIM, t_chunk), 0) == 0, 1.0, 0.0)
    for t0 in range(0, seq, t_chunk):
        v_t = v_ref[0, t0:t0 + t_chunk, :].T
        q_t = (q_ref[0, t0:t0 + t_chunk, :] * (Q_SCALE * LOG2_E)).T
        for h in range(2):
            vt_ref[h, 0:HEAD_DIM, t0:t0 + t_chunk] = v_t[h * HEAD_DIM:(h + 1) * HEAD_DIM].astype(BF16)
            vt_ref[h, HEAD_DIM:VT_ROWS, t0:t0 + t_chunk] = one_row.astype(BF16)
            r0 = _c_lane(pair, h)
            ones = jnp.where((row >= r0) & (row < r0 + C_TERMS), 1.0, 0.0)
            qt_ref[h, :, t0:t0 + t_chunk] = jnp.where((row // HEAD_DIM) == h, q_t, ones).astype(BF16)
    acc_ref[...] = jnp.zeros_like(acc_ref)
    m_ref[...] = jnp.full_like(m_ref, NEG_BIG)

    chains = [(h, slice(q0, q0 + tqs)) for h in range(2) for q0 in range(0, tq, tqs)]

    def scores(qi, j, dst_ref):
        ks = pl.multiple_of(j * tq, tq)
        kb = k_ref[0, pl.ds(ks, tq), :].astype(BF16)
        cb = c_ref[0, pl.ds(ks, tq), :]
        ka = [jnp.where((lane // HEAD_DIM) == h, kb, cb) for h in range(2)]
        for i, (h, qs) in enumerate(chains):
            dst_ref[i] = _dot(ka[h], qt_ref[h, :, pl.ds(pl.multiple_of(qi * tq + qs.start, tqs), tqs)])

    def update(src_ref, qi, j, diagonal):
        ks = pl.multiple_of(j * tq, tq)
        nk = [qs.stop if diagonal else tq for h, qs in chains]
        s = [src_ref[i, 0:nk[i], :] for i in range(len(chains))]
        if diagonal:
            s = [jnp.where(lax.broadcasted_iota(jnp.int32, (nk[i], tqs), 0)
                           <= lax.broadcasted_iota(jnp.int32, (nk[i], tqs), 1) + qs.start, s[i], NEG_BIG)
                 for i, (h, qs) in enumerate(chains)]
        m_old = [jnp.where(j == 0, NEG_BIG, m_ref[h, :, qs]) for h, qs in chains]
        m_new = [jnp.maximum(m_old[i], jnp.max(s[i], axis=0, keepdims=True)) for i in range(len(chains))]
        alpha = [jnp.exp2(m_old[i] - m_new[i]) for i in range(len(chains))]
        pexp = [jnp.exp2(s[i] - m_new[i]).astype(BF16) for i in range(len(chains))]
        pv = [_dot(vt_ref[h, :, pl.ds(ks, nk[i])], pexp[i]) for i, (h, qs) in enumerate(chains)]
        for i, (h, qs) in enumerate(chains):
            acc_ref[h, :, qs] = alpha[i] * acc_ref[h, :, qs] + pv[i]
            m_ref[h, :, qs] = m_new[i]
        if diagonal:
            o_t = jnp.concatenate([acc_ref[h, 0:HEAD_DIM, :] / acc_ref[h, HEAD_DIM:HEAD_DIM + 1, :]
                                   for h in range(2)], axis=0)
            q0 = pl.multiple_of(qi * tq, tq)
            gate = g_ref[0, pl.ds(q0, tq), :]
            o_ref[0, pl.ds(q0, tq), :] = (o_t.T * (gate * _sigmoid(gate))).astype(o_ref.dtype)

    def following(qi, j):
        wrap = j == qi
        return jnp.where(wrap, qi + 1, qi), jnp.where(wrap, 0, j + 1)

    def half_step(cur_ref, nxt_ref, qi, j):
        qn, jn = following(qi, j)
        qn_c = jnp.minimum(qn, nq - 1)

        @pl.when(j < qi)
        def _():
            scores(qn_c, jn, nxt_ref)
            update(cur_ref, qi, j, False)

        @pl.when(j == qi)
        def _():
            scores(qn_c, jn, nxt_ref)
            update(cur_ref, qi, j, True)
        return qn, jn

    scores(0, 0, sa_ref)

    def body(n, carry):
        qi, j = carry
        qi, j = half_step(sa_ref, sb_ref, qi, j)
        return half_step(sb_ref, sa_ref, qi, j)

    lax.fori_loop(0, nq * (nq + 1) // 4, body, (jnp.int32(0), jnp.int32(0)))


def _fox(z3, c_sp, col0, width, tq):
    bsz, seq, _ = z3.shape
    n_pairs = width // LANES
    nq = seq // tq
    assert (nq * (nq + 1) // 2) % 2 == 0
    blk = lambda j: col0 // LANES + j * n_pairs
    tqs = min(tq, 2 * LANES)
    s_buf = pltpu.VMEM((2 * tq // tqs, tq, tqs), F32)
    zspec = lambda j: pl.BlockSpec((1, seq, LANES), lambda b, p, j=j: (b, 0, blk(j) + p))
    return pl.pallas_call(
        functools.partial(_fox_kernel, tq=tq, tqs=tqs, t_chunk=min(seq, 512)),
        grid=(bsz, n_pairs),
        in_specs=[zspec(0), zspec(1), zspec(2), zspec(3),
                  pl.BlockSpec((1, seq, LANES), lambda b, p: (b, 0, 0))],
        out_specs=pl.BlockSpec((1, seq, LANES), lambda b, p: (b, 0, p)),
        out_shape=jax.ShapeDtypeStruct((bsz, seq, width), BF16),
        scratch_shapes=[pltpu.VMEM((2, VT_ROWS, seq), BF16), pltpu.VMEM((2, LANES, seq), BF16),
                        pltpu.VMEM((2, 1, tq), F32), pltpu.VMEM((2, VT_ROWS, tq), F32), s_buf, s_buf],
        compiler_params=pltpu.CompilerParams(dimension_semantics=("arbitrary",) * 2,
                                             vmem_limit_bytes=VMEM_LIMIT),
        name="fox_attention",
    )(z3, z3, z3, z3, c_sp)


def _tail_kernel(yr_ref, yf_ref, gr_ref, gf_ref, x_ref, p_ref, ng_ref, fg_ref, wr_ref, wf_ref, wo_ref, wp_ref, wg_ref,
                 o_ref):
    u_rw = _dot(yr_ref[...], wr_ref[...])
    u_fox = _dot(yf_ref[...], wf_ref[...])
    merged = _sigmoid(gr_ref[...]) * u_rw + _sigmoid(gf_ref[...]) * u_fox
    x1 = x_ref[...] + _dot(merged.astype(BF16), wo_ref[...])
    hn = x1 * lax.rsqrt(jnp.mean(x1 * x1, axis=-1, keepdims=True) + NORM_EPS) * ng_ref[...]
    gate = _sigmoid(_dot(hn.astype(BF16), wg_ref[...]))
    ple = _dot(p_ref[...].astype(BF16), wp_ref[...])
    x2 = x1 + ple * gate
    o_ref[...] = x2 * lax.rsqrt(jnp.mean(x2 * x2, axis=-1, keepdims=True) + NORM_EPS) * fg_ref[...]


def _tail(y_rw, y_fox, z, gate_col0, x, p, ple_norm_g, final_norm_g, w_up_rw, w_up_fox, w_out, ple_proj, ple_gate_w,
          tm):
    m, d = x.shape
    cw = y_rw.shape[1]
    pd = p.shape[1]
    gblk = gate_col0 // d
    rows = lambda n, j=0: pl.BlockSpec((tm, n), lambda i, j=j: (i, j))
    const = lambda shape: pl.BlockSpec(shape, lambda i: (0, 0), pipeline_mode=pl.Buffered(1))
    return pl.pallas_call(
        _tail_kernel,
        grid=(m // tm,),
        in_specs=[rows(cw), rows(cw), rows(d, gblk), rows(d, gblk + 1), rows(d), rows(pd),
                  const((1, d)), const((1, d)), const((cw, d)), const((cw, d)), const((d, d)), const((pd, d)),
                  const((d, d))],
        out_specs=rows(d),
        out_shape=jax.ShapeDtypeStruct((m, d), F32),
        compiler_params=pltpu.CompilerParams(dimension_semantics=("arbitrary",),
                                             vmem_limit_bytes=VMEM_LIMIT),
        name="merge_ple_final_norm",
    )(y_rw, y_fox, z, z, x, p, ple_norm_g.reshape(1, d), final_norm_g.reshape(1, d), w_up_rw, w_up_fox, w_out,
      ple_proj, ple_gate_w)


def _regroup_kernel(main_ref, lora_ref, fl_ref, o_ref, *, n_heads):
    last = pl.num_programs(0) - 1

    @pl.when(pl.program_id(0) < last)
    def _():
        o_ref[...] = main_ref[...].T.astype(BF16)

    @pl.when(pl.program_id(0) == last)
    def _():
        row = lax.broadcasted_iota(jnp.int32, (LANES, 1), 0)
        o_ref[:, 0:LANES] = lora_ref[...].T.astype(BF16)
        o_ref[:, LANES:2 * LANES] = jnp.where(row < n_heads, fl_ref[...], 0.0).T.astype(BF16)


def _regroup_weights(w_t, cw, lw, n_heads, tr):
    n_in, d = w_t.shape
    fl0 = 8 * cw + 2 * lw
    small0 = 8 * cw + 2 * d
    n_pad = small0 + 2 * LANES
    assert tr == 2 * LANES and (4 * cw) % tr == 0 and (2 * d) % tr == 0 and n_in == fl0 + n_heads + 2 * d
    assert (2 * lw) % 8 == 0 and n_heads % 8 == 0

    def main_rows(i):
        r = i * tr
        src = jnp.where(r < 4 * cw, r, jnp.where(r < 8 * cw, r + 2 * lw,
                                                 jnp.where(r < small0, r + 2 * lw + n_heads, 0)))
        return pl.multiple_of(src, 8), 0

    rows = lambda n: (pl.Element(n), pl.Element(d))
    return pl.pallas_call(
        functools.partial(_regroup_kernel, n_heads=n_heads),
        grid=(n_pad // tr,),
        in_specs=[pl.BlockSpec(rows(tr), main_rows),
                  pl.BlockSpec(rows(LANES), lambda i: (4 * cw, 0)),
                  pl.BlockSpec(rows(LANES), lambda i: (fl0, 0))],
        out_specs=pl.BlockSpec((d, tr), lambda i: (0, i)),
        out_shape=jax.ShapeDtypeStruct((d, n_pad), BF16),
        compiler_params=pltpu.CompilerParams(dimension_semantics=("arbitrary",),
                                             vmem_limit_bytes=VMEM_LIMIT),
        name="regroup_w_in",
    )(w_t, w_t, w_t)


def _layer(x2d, p2d, seq, norm_g, w_in, rw_shift_mu, rw_w0, rw_w_lora_up, rw_a0, rw_a_lora_up, rw_k_k, rw_k_a,
           rw_r_k, rw_ln_g, rw_ln_b, fox_b_f, w_up_rwkv, w_up_fox, w_out, ple_proj, ple_gate_w, ple_norm_g,
           final_norm_g, *, tiles):
    m, d = x2d.shape
    bsz = m // seq
    cw = w_up_rwkv.shape[0]
    n_heads = fox_b_f.shape[0]
    lw = rw_w_lora_up.shape[0]
    assert 2 * lw == LANES and cw % LANES == 0 and C_TERMS * (cw // LANES) <= HEAD_DIM

    fox0, gate0, small0 = 4 * cw, 8 * cw, 8 * cw + 2 * d
    n_pad = small0 + 2 * LANES
    w_cat = _regroup_weights(w_in.T, cw, lw, n_heads, tiles["regroup_tr"])
    mu_cat = jnp.concatenate([
        rw_shift_mu[0:4 * cw], jnp.zeros((small0 - 4 * cw,), F32), rw_shift_mu[4 * cw:],
        jnp.zeros((LANES,), F32)]).reshape(1, n_pad)

    z = _inproj(x2d, norm_g, w_cat, mu_cat, seq, tiles["in_tm"], tiles["in_tn"])
    z3 = z.reshape(bsz, seq, n_pad)
    lora_blk = small0 // (2 * LANES)

    row = lambda a: a.reshape(1, cw)
    wup = jnp.pad(rw_w_lora_up, ((0, lw), (0, 0))).astype(BF16)
    aup = jnp.pad(rw_a_lora_up, ((lw, 0), (0, 0))).astype(BF16)
    y_rw = _rwkv(z3, lora_blk, wup, aup, row(rw_w0), row(rw_a0), row(rw_k_k), row(rw_k_a), row(rw_r_k),
                 row(rw_ln_g), row(rw_ln_b), cw)

    bf_pad = jnp.pad(fox_b_f, (0, LANES - n_heads)).reshape(1, LANES)
    c_sp = _fcum(z3, lora_blk, bf_pad, n_heads, tiles["fc_tc"])
    y_fox = _fox(z3, c_sp, fox0, cw, tiles["fox_tq"])

    return _tail(y_rw.reshape(m, cw), y_fox.reshape(m, cw), z, gate0, x2d, p2d, ple_norm_g, final_norm_g,
                 w_up_rwkv.astype(BF16), w_up_fox.astype(BF16), w_out.astype(BF16), ple_proj.astype(BF16),
                 ple_gate_w.astype(BF16), tiles["tail_tm"])


_TILES = dict(regroup_tr=256, in_tm=1024, in_tn=1792, fc_tc=512, fox_tq=1024, tail_tm=256)


def kernel(x, p, norm_g, w_in, rw_shift_mu, rw_w0, rw_w_lora_up, rw_a0, rw_a_lora_up, rw_k_k, rw_k_a, rw_r_k,
           rw_ln_g, rw_ln_b, fox_b_f, w_up_rwkv, w_up_fox, w_out, ple_proj, ple_gate_w, ple_norm_g,
           final_norm_g):
    bsz, seq, d = x.shape
    depth = p.shape[0]
    assert depth == 1, "the final norm is fused into the (single) layer"
    out = _layer(x.reshape(bsz * seq, d), p[0].reshape(bsz * seq, -1), seq, norm_g[0], w_in[0], rw_shift_mu[0],
                 rw_w0[0], rw_w_lora_up[0], rw_a0[0], rw_a_lora_up[0], rw_k_k[0], rw_k_a[0],
                 rw_r_k[0].reshape(-1), rw_ln_g[0], rw_ln_b[0], fox_b_f[0], w_up_rwkv[0], w_up_fox[0], w_out[0],
                 ple_proj[0], ple_gate_w[0], ple_norm_g[0], final_norm_g, tiles=_TILES)
    return out.reshape(bsz, seq, d)
```

```python
import functools
import math

import jax
import jax.numpy as jnp
from jax import lax
from jax.experimental import pallas as pl
from jax.experimental.pallas import tpu as pltpu

F32 = jnp.float32
BF16 = jnp.bfloat16

HEAD_DIM = 64
LANES = 128
NORM_EPS = 1e-6
GN_EPS = 64e-5
Q_SCALE = HEAD_DIM ** -0.5
EXP_NEG_HALF = math.exp(-0.5)
LOG2_E = math.log2(math.e)
NEG_BIG = -1e30
VMEM_LIMIT = 56 * 1024 * 1024

SPLIT_TERMS = 2
NORM_TERMS = 1
RW_CHUNK = 64
RW_CHUNKS_PER_STEP = 4
C_TERMS = 3
VT_ROWS = HEAD_DIM + 16


def _dot(a, b):
    return jnp.dot(a, b, preferred_element_type=F32)


def _dot_nt(a, b):
    return lax.dot_general(a, b, (((1,), (1,)), ((), ())), preferred_element_type=F32)


def _dot_tn(a, b):
    return lax.dot_general(a, b, (((0,), (0,)), ((), ())), preferred_element_type=F32)


def _split_bf16(x, n):
    parts = []
    rem = x
    for i in range(n):
        p = rem.astype(BF16)
        parts.append(p)
        if i + 1 < n:
            rem = rem - p.astype(F32)
    return parts


def _sigmoid(x):
    return 1.0 / (1.0 + jnp.exp(-x))


def _inproj_kernel(x_ref, g_ref, w_ref, mu_ref, o_ref, carry_ref, *, tiles_per_seq):
    x = x_ref[...]
    rstd = lax.rsqrt(jnp.mean(x * x, axis=-1, keepdims=True) + NORM_EPS)
    z = _dot((x * g_ref[...]).astype(BF16), w_ref[...]) * rstd
    tm = z.shape[0]
    first = (pl.program_id(1) % tiles_per_seq) == 0
    prev_last = jnp.where(first, 0.0, carry_ref[7:8, :])
    row = lax.broadcasted_iota(jnp.int32, z.shape, 0)
    z_prev = jnp.where(row == 0, prev_last, pltpu.roll(z, 1, 0))
    o_ref[...] = z + (z_prev - z) * mu_ref[...]
    carry_ref[...] = z[tm - 8:tm, :]


def _inproj(x, g, w, mu, seq, tm, tn):
    m, d = x.shape
    n = w.shape[1]
    return pl.pallas_call(
        functools.partial(_inproj_kernel, tiles_per_seq=seq // tm),
        grid=(n // tn, m // tm),
        in_specs=[pl.BlockSpec((tm, d), lambda j, i: (i, 0)),
                  pl.BlockSpec((1, d), lambda j, i: (0, 0)),
                  pl.BlockSpec((d, tn), lambda j, i: (0, j)),
                  pl.BlockSpec((1, tn), lambda j, i: (0, j))],
        out_specs=pl.BlockSpec((tm, tn), lambda j, i: (i, j)),
        out_shape=jax.ShapeDtypeStruct((m, n), F32),
        scratch_shapes=[pltpu.VMEM((8, tn), F32)],
        compiler_params=pltpu.CompilerParams(dimension_semantics=("arbitrary", "arbitrary"),
                                             vmem_limit_bytes=VMEM_LIMIT),
        name="inproj",
    )(x, g.reshape(1, d), w, mu)


def _rwkv_chunk(r, k, v, gate, lora_b, lora_t, wup, aup, w0, a0, kkp, kap, rkp, lng, lnb, s_old, *, c):
    rows, width = r.shape
    n_pairs = width // LANES
    pairs = range(n_pairs)
    subs = range(rows // c)
    blocks = [(i, p) for i in subs for p in pairs]
    cols = lambda x: [x[i * c:(i + 1) * c, p * LANES:(p + 1) * LANES] for i, p in blocks]
    lane = lax.broadcasted_iota(jnp.int32, (1, LANES), 1)
    m0 = lane < HEAD_DIM

    ri = lax.broadcasted_iota(jnp.int32, (LANES, LANES), 0)
    ci = lax.broadcasted_iota(jnp.int32, (LANES, LANES), 1)
    same_head = (ri // HEAD_DIM) == (ci // HEAD_DIM)
    ones_bd = jnp.where(same_head, 1.0, 0.0).astype(BF16)

    def segsum(x, terms):
        xr = jnp.concatenate(cols(x), axis=0)
        sr = sum(_dot(part, ones_bd) for part in _split_bf16(xr, terms))
        return jnp.concatenate([jnp.concatenate([sr[(i * n_pairs + p) * c:(i * n_pairs + p + 1) * c] for p in pairs],
                                                axis=1) for i in subs], axis=0)

    w_raw = w0 + _dot(lora_t, wup)
    logw = -EXP_NEG_HALF * _sigmoid(w_raw)
    a = _sigmoid(a0 + _dot(lora_b, aup))
    kkr = k * kkp
    kk = kkr / jnp.maximum(jnp.sqrt(segsum(kkr * kkr, NORM_TERMS)), 1e-12)
    b = kk * a
    k2 = k * (1.0 + (a - 1.0) * kap)

    ti = lax.broadcasted_iota(jnp.int32, (rows, rows), 0)
    si = lax.broadcasted_iota(jnp.int32, (rows, rows), 1)
    tri = jnp.where((si <= ti) & (si // c == ti // c), 1.0, 0.0).astype(BF16)
    g = sum(_dot(tri, part) for part in _split_bf16(logw, SPLIT_TERMS))
    g_ends = [g[(i + 1) * c - 1:(i + 1) * c, :] for i in subs]
    g_last = jnp.concatenate([jnp.broadcast_to(ge, (c, width)) for ge in g_ends], axis=0)
    egd = jnp.exp(g_last - g)
    eng = jnp.exp(-g)
    rg = cols(r * jnp.exp(g))
    kkg = cols(kk * jnp.exp(g - logw))
    kn = cols(k2 * eng)
    bn = cols(b * eng)
    kd = cols(k2 * egd)
    bdec = cols(b * egd)
    vs = cols(v)
    decay_c = [jnp.exp(g_ends[i][:, p * LANES:(p + 1) * LANES]) for i, p in blocks]
    nb = range(len(blocks))

    def stack(x):
        return jnp.concatenate([jnp.where(m0, x, 0.0), jnp.where(m0, 0.0, x)], axis=0).astype(BF16)

    t_in = lax.broadcasted_iota(jnp.int32, (c, LANES), 0)
    s_in = lax.broadcasted_iota(jnp.int32, (c, LANES), 1) % HEAD_DIM
    strict = s_in < t_in
    incl = s_in <= t_in
    eye = jnp.where(s_in == t_in, 1.0, 0.0)

    lhs_kr = [jnp.concatenate([kkg[b], rg[b]], axis=0).astype(BF16) for b in nb]
    rhs_bk = [jnp.concatenate([stack(bn[b]), stack(kn[b])], axis=0) for b in nb]
    aa = [_dot_nt(lhs_kr[b], rhs_bk[b]) for b in nb]
    a_kb = [jnp.where(strict, aa[b][0:c, 0:LANES], 0.0) for b in nb]
    a_kk = [jnp.where(strict, aa[b][0:c, LANES:2 * LANES], 0.0).astype(BF16) for b in nb]
    a_rb = [jnp.where(incl, aa[b][c:2 * c, 0:LANES], 0.0) for b in nb]
    a_rk = [jnp.where(incl, aa[b][c:2 * c, LANES:2 * LANES], 0.0) for b in nb]
    a_r = [jnp.concatenate([a_rk[b], a_rb[b]], axis=1).astype(BF16) for b in nb]
    v_st = [stack(vs[b]) for b in nb]
    kb = [jnp.concatenate([kd[b], bdec[b]], axis=0).astype(BF16) for b in nb]
    av = [_dot(a_kk[b], v_st[b]) for b in nb]

    tinv = [eye - a_kb[b] for b in nb]
    lpow = [_dot(a_kb[b].astype(BF16), stack(a_kb[b])) for b in nb]
    n_sq = int(math.log2(c)) - 1
    for i in range(n_sq):
        last = i + 1 == n_sq
        w_pow = [stack(lpow[b]) for b in nb]
        lhs = [tinv[b].astype(BF16) if last else jnp.concatenate([tinv[b], lpow[b]], axis=0).astype(BF16)
               for b in nb]
        prod = [_dot(lhs[b], w_pow[b]) for b in nb]
        tinv = [tinv[b] + prod[b][0:c] for b in nb]
        if not last:
            lpow = [prod[b][c:2 * c] for b in nb]
    tinv = [t.astype(BF16) for t in tinv]

    s_cur = list(s_old)
    ys = []
    for i in subs:
        bi = [i * n_pairs + p for p in pairs]
        s_b = [s_cur[p].astype(BF16) for p in pairs]
        sh = [_dot_nt(lhs_kr[bi[p]], s_b[p]) for p in pairs]
        x = [sh[p][0:c] + av[bi[p]] for p in pairs]
        u = [_dot(tinv[bi[p]], stack(x[p])) for p in pairs]
        vu_st = [jnp.concatenate([v_st[bi[p]], stack(-u[p])], axis=0) for p in pairs]
        ys.append(jnp.concatenate([sh[p][c:2 * c] + _dot(a_r[bi[p]], vu_st[p]) for p in pairs], axis=1))
        vu = [jnp.concatenate([vs[bi[p]], -u[p]], axis=0).astype(BF16) for p in pairs]
        s_cur = [s_cur[p] * decay_c[bi[p]] + jnp.where(same_head, _dot_tn(vu[p], kb[bi[p]]), 0.0) for p in pairs]
    y = jnp.concatenate(ys, axis=0)
    s_new = s_cur

    inv_n = 1.0 / HEAD_DIM
    mean = segsum(y, NORM_TERMS) * inv_n
    dlt = y - mean
    var = segsum(dlt * dlt, NORM_TERMS) * inv_n
    yn = dlt * lax.rsqrt(var + GN_EPS) * lng + lnb
    bonus = segsum(r * k2 * rkp, SPLIT_TERMS) * v
    out = (yn + bonus) * (gate * _sigmoid(gate))
    return out, s_new


def _rwkv_kernel(r_ref, k_ref, v_ref, g_ref, lora_ref, wup_ref, aup_ref, w0_ref, a0_ref, kk_ref, ka_ref,
                 rk_ref, lng_ref, lnb_ref, o_ref, s_ref):
    @pl.when(pl.program_id(1) == 0)
    def _():
        s_ref[...] = jnp.zeros_like(s_ref)

    lora = lora_ref[0][:, 0:LANES]
    out, s_new = _rwkv_chunk(
        r_ref[0], k_ref[0], v_ref[0], g_ref[0], lora.astype(BF16), jnp.tanh(lora).astype(BF16),
        wup_ref[...], aup_ref[...], w0_ref[...], a0_ref[...], kk_ref[...], ka_ref[...], rk_ref[...],
        lng_ref[...], lnb_ref[...], [s_ref[p] for p in range(s_ref.shape[0])], c=RW_CHUNK)
    for p in range(s_ref.shape[0]):
        s_ref[p] = s_new[p]
    o_ref[0] = out.astype(o_ref.dtype)


def _rwkv(z3, lora_blk, wup, aup, w0, a0, k_k, k_a, r_k, ln_g, ln_b, width):
    bsz, seq, _ = z3.shape
    c = min(seq, RW_CHUNK * RW_CHUNKS_PER_STEP)
    assert width % LANES == 0
    zspec = lambda j: pl.BlockSpec((1, c, width), lambda b, t, j=j: (b, t, j))
    pspec = lambda rows: pl.BlockSpec((rows, width), lambda b, t: (0, 0))
    return pl.pallas_call(
        _rwkv_kernel,
        grid=(bsz, seq // c),
        in_specs=[zspec(0), zspec(1), zspec(2), zspec(3),
                  pl.BlockSpec((1, c, 2 * LANES), lambda b, t: (b, t, lora_blk)),
                  pspec(LANES), pspec(LANES)] + [pspec(1)] * 7,
        out_specs=pl.BlockSpec((1, c, width), lambda b, t: (b, t, 0)),
        out_shape=jax.ShapeDtypeStruct((bsz, seq, width), BF16),
        scratch_shapes=[pltpu.VMEM((width // LANES, LANES, LANES), F32)],
        compiler_params=pltpu.CompilerParams(dimension_semantics=("arbitrary", "arbitrary"),
                                             vmem_limit_bytes=VMEM_LIMIT),
        name="rwkv7_scan",
    )(z3, z3, z3, z3, z3, wup, aup, w0, a0, k_k, k_a, r_k, ln_g, ln_b)


def _c_lane(pair, h):
    return (1 - h) * HEAD_DIM + C_TERMS * pair


def _fcum_kernel(l_ref, bf_ref, o_ref, carry_ref, *, n_heads):
    @pl.when(pl.program_id(1) == 0)
    def _():
        carry_ref[...] = jnp.zeros_like(carry_ref)

    x = l_ref[0] + bf_ref[...]
    log_f = jnp.minimum(x, 0.0) - jnp.log(1.0 + jnp.exp(-jnp.abs(x)))
    tc = x.shape[0]
    ti = lax.broadcasted_iota(jnp.int32, (tc, tc), 0)
    si = lax.broadcasted_iota(jnp.int32, (tc, tc), 1)
    tril = jnp.where(si <= ti, 1.0, 0.0).astype(BF16)
    c = sum(_dot(tril, part) for part in _split_bf16(log_f, 3)) + carry_ref[0:1, :]
    carry_ref[...] = jnp.broadcast_to(c[tc - 1:tc, :], carry_ref.shape)
    lin = lax.broadcasted_iota(jnp.int32, (LANES, LANES), 0)
    lout = lax.broadcasted_iota(jnp.int32, (LANES, LANES), 1)
    terms = _split_bf16(-LOG2_E * c, C_TERMS)
    sel = [jnp.where((lin < n_heads) & (lout == _c_lane(lin // 2, lin % 2) + i), 1.0, 0.0).astype(BF16)
           for i in range(C_TERMS)]
    o_ref[0] = sum(_dot(terms[i], sel[i]) for i in range(C_TERMS)).astype(o_ref.dtype)


def _fcum(z3, lora_blk, bf_pad, n_heads, tc):
    bsz, seq, _ = z3.shape
    return pl.pallas_call(
        functools.partial(_fcum_kernel, n_heads=n_heads),
        grid=(bsz, seq // tc),
        in_specs=[pl.BlockSpec((1, tc, LANES), lambda b, t: (b, t, 2 * lora_blk + 1)),
                  pl.BlockSpec((1, LANES), lambda b, t: (0, 0))],
        out_specs=pl.BlockSpec((1, tc, LANES), lambda b, t: (b, t, 0)),
        out_shape=jax.ShapeDtypeStruct((bsz, seq, LANES), BF16),
        scratch_shapes=[pltpu.VMEM((8, LANES), F32)],
        compiler_params=pltpu.CompilerParams(dimension_semantics=("arbitrary", "arbitrary"),
                                             vmem_limit_bytes=VMEM_LIMIT),
        name="fox_cumsum",
    )(z3, bf_pad)


def _fox_kernel(q_ref, k_ref, v_ref, g_ref, c_ref, o_ref, vt_ref, qt_ref, m_ref, acc_ref, sa_ref, sb_ref, *, tq, tqs,
                t_chunk):
    pair = pl.program_id(1)
    seq = k_ref.shape[1]
    nq = seq // tq
    lane = lax.broadcasted_iota(jnp.int32, (1, LANES), 1)
    row = lax.broadcasted_iota(jnp.int32, (LANES, 1), 0)

    one_row = jnp.where(lax.broadcasted_iota(jnp.int32, (VT_ROWS - HEAD_DIM, t_chunk), 0) == 0, 1.0, 0.0)
    for t0 in range(0, seq, t_chunk):
        v_t = v_ref[0, t0:t0 + t_chunk, :].T
        q_t = (q_ref[0, t0:t0 + t_chunk, :] * (Q_SCALE * LOG2_E)).T
        for h in range(2):
            vt_ref[h, 0:HEAD_DIM, t0:t0 + t_chunk] = v_t[h * HEAD_DIM:(h + 1) * HEAD_DIM].astype(BF16)
            vt_ref[h, HEAD_DIM:VT_ROWS, t0:t0 + t_chunk] = one_row.astype(BF16)
            r0 = _c_lane(pair, h)
            ones = jnp.where((row >= r0) & (row < r0 + C_TERMS), 1.0, 0.0)
            qt_ref[h, :, t0:t0 + t_chunk] = jnp.where((row // HEAD_DIM) == h, q_t, ones).astype(BF16)
    acc_ref[...] = jnp.zeros_like(acc_ref)
    m_ref[...] = jnp.full_like(m_ref, NEG_BIG)

    chains = [(h, slice(q0, q0 + tqs)) for h in range(2) for q0 in range(0, tq, tqs)]

    def keys_seen(qs, diagonal):
        return qs.stop if diagonal else tq

    def scores(qi, j, dst_ref, diagonal):
        ks = pl.multiple_of(j * tq, tq)
        kb = k_ref[0, pl.ds(ks, tq), :].astype(BF16)
        cb = c_ref[0, pl.ds(ks, tq), :]
        ka = [jnp.where((lane // HEAD_DIM) == h, kb, cb) for h in range(2)]
        for i, (h, qs) in enumerate(chains):
            nk = keys_seen(qs, diagonal)
            dst_ref[i, 0:nk, :] = _dot(ka[h][0:nk], qt_ref[h, :, pl.ds(pl.multiple_of(qi * tq + qs.start, tqs), tqs)])

    def update(src_ref, qi, j, diagonal):
        ks = pl.multiple_of(j * tq, tq)
        nk = [keys_seen(qs, diagonal) for h, qs in chains]
        s = [src_ref[i, 0:nk[i], :] for i in range(len(chains))]
        if diagonal:
            s = [jnp.where(lax.broadcasted_iota(jnp.int32, (nk[i], tqs), 0)
                           <= lax.broadcasted_iota(jnp.int32, (nk[i], tqs), 1) + qs.start, s[i], NEG_BIG)
                 for i, (h, qs) in enumerate(chains)]
        m_old = [jnp.where(j == 0, NEG_BIG, m_ref[h, :, qs]) for h, qs in chains]
        m_new = [jnp.maximum(m_old[i], jnp.max(s[i], axis=0, keepdims=True)) for i in range(len(chains))]
        alpha = [jnp.exp2(m_old[i] - m_new[i]) for i in range(len(chains))]
        pexp = [jnp.exp2(s[i] - m_new[i]).astype(BF16) for i in range(len(chains))]
        pv = [_dot(vt_ref[h, :, pl.ds(ks, nk[i])], pexp[i]) for i, (h, qs) in enumerate(chains)]
        for i, (h, qs) in enumerate(chains):
            acc_ref[h, :, qs] = alpha[i] * acc_ref[h, :, qs] + pv[i]
            m_ref[h, :, qs] = m_new[i]
        if diagonal:
            o_t = jnp.concatenate([acc_ref[h, 0:HEAD_DIM, :] / acc_ref[h, HEAD_DIM:HEAD_DIM + 1, :]
                                   for h in range(2)], axis=0)
            q0 = pl.multiple_of(qi * tq, tq)
            gate = g_ref[0, pl.ds(q0, tq), :]
            o_ref[0, pl.ds(q0, tq), :] = (o_t.T * (gate * _sigmoid(gate))).astype(o_ref.dtype)

    def following(qi, j):
        wrap = j == qi
        return jnp.where(wrap, qi + 1, qi), jnp.where(wrap, 0, j + 1)

    def half_step(cur_ref, nxt_ref, qi, j):
        qn, jn = following(qi, j)
        qn_c = jnp.minimum(qn, nq - 1)

        @pl.when(j + 1 < qi)
        def _():
            scores(qn_c, jn, nxt_ref, False)
            update(cur_ref, qi, j, False)

        @pl.when(j + 1 == qi)
        def _():
            scores(qn_c, jn, nxt_ref, True)
            update(cur_ref, qi, j, False)

        @pl.when(j == qi)
        def _():
            scores(qn_c, jn, nxt_ref, False)
            update(cur_ref, qi, j, True)
        return qn, jn

    scores(0, 0, sa_ref, True)

    def body(n, carry):
        qi, j = carry
        qi, j = half_step(sa_ref, sb_ref, qi, j)
        return half_step(sb_ref, sa_ref, qi, j)

    lax.fori_loop(0, nq * (nq + 1) // 4, body, (jnp.int32(0), jnp.int32(0)))


def _fox(z3, c_sp, col0, width, tq):
    bsz, seq, _ = z3.shape
    n_pairs = width // LANES
    nq = seq // tq
    assert (nq * (nq + 1) // 2) % 2 == 0
    blk = lambda j: col0 // LANES + j * n_pairs
    tqs = min(tq, 2 * LANES)
    s_buf = pltpu.VMEM((2 * tq // tqs, tq, tqs), F32)
    zspec = lambda j: pl.BlockSpec((1, seq, LANES), lambda b, p, j=j: (b, 0, blk(j) + p))
    return pl.pallas_call(
        functools.partial(_fox_kernel, tq=tq, tqs=tqs, t_chunk=min(seq, 512)),
        grid=(bsz, n_pairs),
        in_specs=[zspec(0), zspec(1), zspec(2), zspec(3),
                  pl.BlockSpec((1, seq, LANES), lambda b, p: (b, 0, 0))],
        out_specs=pl.BlockSpec((1, seq, LANES), lambda b, p: (b, 0, p)),
        out_shape=jax.ShapeDtypeStruct((bsz, seq, width), BF16),
        scratch_shapes=[pltpu.VMEM((2, VT_ROWS, seq), BF16), pltpu.VMEM((2, LANES, seq), BF16),
                        pltpu.VMEM((2, 1, tq), F32), pltpu.VMEM((2, VT_ROWS, tq), F32), s_buf, s_buf],
        compiler_params=pltpu.CompilerParams(dimension_semantics=("arbitrary",) * 2,
                                             vmem_limit_bytes=VMEM_LIMIT),
        name="fox_attention",
    )(z3, z3, z3, z3, c_sp)


def _tail_kernel(yr_ref, yf_ref, gr_ref, gf_ref, x_ref, p_ref, ng_ref, fg_ref, wr_ref, wf_ref, wo_ref, wp_ref, wg_ref,
                 o_ref):
    u_rw = _dot(yr_ref[...], wr_ref[...])
    u_fox = _dot(yf_ref[...], wf_ref[...])
    merged = _sigmoid(gr_ref[...]) * u_rw + _sigmoid(gf_ref[...]) * u_fox
    x1 = x_ref[...] + _dot(merged.astype(BF16), wo_ref[...])
    hn = x1 * lax.rsqrt(jnp.mean(x1 * x1, axis=-1, keepdims=True) + NORM_EPS) * ng_ref[...]
    gate = _sigmoid(_dot(hn.astype(BF16), wg_ref[...]))
    ple = _dot(p_ref[...].astype(BF16), wp_ref[...])
    x2 = x1 + ple * gate
    o_ref[...] = x2 * lax.rsqrt(jnp.mean(x2 * x2, axis=-1, keepdims=True) + NORM_EPS) * fg_ref[...]


def _tail(y_rw, y_fox, z, gate_col0, x, p, ple_norm_g, final_norm_g, w_up_rw, w_up_fox, w_out, ple_proj, ple_gate_w,
          tm):
    m, d = x.shape
    cw = y_rw.shape[1]
    pd = p.shape[1]
    gblk = gate_col0 // d
    rows = lambda n, j=0: pl.BlockSpec((tm, n), lambda i, j=j: (i, j))
    const = lambda shape: pl.BlockSpec(shape, lambda i: (0, 0), pipeline_mode=pl.Buffered(1))
    return pl.pallas_call(
        _tail_kernel,
        grid=(m // tm,),
        in_specs=[rows(cw), rows(cw), rows(d, gblk), rows(d, gblk + 1), rows(d), rows(pd),
                  const((1, d)), const((1, d)), const((cw, d)), const((cw, d)), const((d, d)), const((pd, d)),
                  const((d, d))],
        out_specs=rows(d),
        out_shape=jax.ShapeDtypeStruct((m, d), F32),
        compiler_params=pltpu.CompilerParams(dimension_semantics=("arbitrary",),
                                             vmem_limit_bytes=VMEM_LIMIT),
        name="merge_ple_final_norm",
    )(y_rw, y_fox, z, z, x, p, ple_norm_g.reshape(1, d), final_norm_g.reshape(1, d), w_up_rw, w_up_fox, w_out,
      ple_proj, ple_gate_w)


def _regroup_kernel(main_ref, lora_ref, fl_ref, o_ref, *, n_heads):
    last = pl.num_programs(0) - 1

    @pl.when(pl.program_id(0) < last)
    def _():
        o_ref[...] = main_ref[...].T.astype(BF16)

    @pl.when(pl.program_id(0) == last)
    def _():
        row = lax.broadcasted_iota(jnp.int32, (LANES, 1), 0)
        o_ref[:, 0:LANES] = lora_ref[...].T.astype(BF16)
        o_ref[:, LANES:2 * LANES] = jnp.where(row < n_heads, fl_ref[...], 0.0).T.astype(BF16)


def _regroup_weights(w_t, cw, lw, n_heads, tr):
    n_in, d = w_t.shape
    fl0 = 8 * cw + 2 * lw
    small0 = 8 * cw + 2 * d
    n_pad = small0 + 2 * LANES
    assert tr == 2 * LANES and (4 * cw) % tr == 0 and (2 * d) % tr == 0 and n_in == fl0 + n_heads + 2 * d
    assert (2 * lw) % 8 == 0 and n_heads % 8 == 0

    def main_rows(i):
        r = i * tr
        src = jnp.where(r < 4 * cw, r, jnp.where(r < 8 * cw, r + 2 * lw,
                                                 jnp.where(r < small0, r + 2 * lw + n_heads, 0)))
        return pl.multiple_of(src, 8), 0

    rows = lambda n: (pl.Element(n), pl.Element(d))
    return pl.pallas_call(
        functools.partial(_regroup_kernel, n_heads=n_heads),
        grid=(n_pad // tr,),
        in_specs=[pl.BlockSpec(rows(tr), main_rows),
                  pl.BlockSpec(rows(LANES), lambda i: (4 * cw, 0)),
                  pl.BlockSpec(rows(LANES), lambda i: (fl0, 0))],
        out_specs=pl.BlockSpec((d, tr), lambda i: (0, i)),
        out_shape=jax.ShapeDtypeStruct((d, n_pad), BF16),
        compiler_params=pltpu.CompilerParams(dimension_semantics=("arbitrary",),
                                             vmem_limit_bytes=VMEM_LIMIT),
        name="regroup_w_in",
    )(w_t, w_t, w_t)


def _layer(x2d, p2d, seq, norm_g, w_in, rw_shift_mu, rw_w0, rw_w_lora_up, rw_a0, rw_a_lora_up, rw_k_k, rw_k_a,
           rw_r_k, rw_ln_g, rw_ln_b, fox_b_f, w_up_rwkv, w_up_fox, w_out, ple_proj, ple_gate_w, ple_norm_g,
           final_norm_g, *, tiles):
    m, d = x2d.shape
    bsz = m // seq
    cw = w_up_rwkv.shape[0]
    n_heads = fox_b_f.shape[0]
    lw = rw_w_lora_up.shape[0]
    assert 2 * lw == LANES and cw % LANES == 0 and C_TERMS * (cw // LANES) <= HEAD_DIM

    fox0, gate0, small0 = 4 * cw, 8 * cw, 8 * cw + 2 * d
    n_pad = small0 + 2 * LANES
    w_cat = _regroup_weights(w_in.T, cw, lw, n_heads, tiles["regroup_tr"])
    mu_cat = jnp.concatenate([
        rw_shift_mu[0:4 * cw], jnp.zeros((small0 - 4 * cw,), F32), rw_shift_mu[4 * cw:],
        jnp.zeros((LANES,), F32)]).reshape(1, n_pad)

    z = _inproj(x2d, norm_g, w_cat, mu_cat, seq, tiles["in_tm"], tiles["in_tn"])
    z3 = z.reshape(bsz, seq, n_pad)
    lora_blk = small0 // (2 * LANES)

    row = lambda a: a.reshape(1, cw)
    wup = jnp.pad(rw_w_lora_up, ((0, lw), (0, 0))).astype(BF16)
    aup = jnp.pad(rw_a_lora_up, ((lw, 0), (0, 0))).astype(BF16)
    y_rw = _rwkv(z3, lora_blk, wup, aup, row(rw_w0), row(rw_a0), row(rw_k_k), row(rw_k_a), row(rw_r_k),
                 row(rw_ln_g), row(rw_ln_b), cw)

    bf_pad = jnp.pad(fox_b_f, (0, LANES - n_heads)).reshape(1, LANES)
    c_sp = _fcum(z3, lora_blk, bf_pad, n_heads, tiles["fc_tc"])
    y_fox = _fox(z3, c_sp, fox0, cw, tiles["fox_tq"])

    return _tail(y_rw.reshape(m, cw), y_fox.reshape(m, cw), z, gate0, x2d, p2d, ple_norm_g, final_norm_g,
                 w_up_rwkv.astype(BF16), w_up_fox.astype(BF16), w_out.astype(BF16), ple_proj.astype(BF16),
                 ple_gate_w.astype(BF16), tiles["tail_tm"])


_TILES = dict(regroup_tr=256, in_tm=1024, in_tn=1792, fc_tc=512, fox_tq=1024, tail_tm=256)


def kernel(x, p, norm_g, w_in, rw_shift_mu, rw_w0, rw_w_lora_up, rw_a0, rw_a_lora_up, rw_k_k, rw_k_a, rw_r_k,
           rw_ln_g, rw_ln_b, fox_b_f, w_up_rwkv, w_up_fox, w_out, ple_proj, ple_gate_w, ple_norm_g,
           final_norm_g):
    bsz, seq, d = x.shape
    depth = p.shape[0]
    assert depth == 1, "the final norm is fused into the (single) layer"
    out = _layer(x.reshape(bsz * seq, d), p[0].reshape(bsz * seq, -1), seq, norm_g[0], w_in[0], rw_shift_mu[0],
                 rw_w0[0], rw_w_lora_up[0], rw_a0[0], rw_a_lora_up[0], rw_k_k[0], rw_k_a[0],
                 rw_r_k[0].reshape(-1), rw_ln_g[0], rw_ln_b[0], fox_b_f[0], w_up_rwkv[0], w_up_fox[0], w_out[0],
                 ple_proj[0], ple_gate_w[0], ple_norm_g[0], final_norm_g, tiles=_TILES)
    return out.reshape(bsz, seq, d)
```

```python
import functools
import math

import jax
import jax.numpy as jnp
from jax import lax
from jax.experimental import pallas as pl
from jax.experimental.pallas import tpu as pltpu

F32 = jnp.float32
BF16 = jnp.bfloat16

HEAD_DIM = 64
LANES = 128
SUBLANES = 8
NORM_EPS = 1e-6
GN_EPS = 64e-5
Q_SCALE = HEAD_DIM ** -0.5
EXP_NEG_HALF = math.exp(-0.5)
LOG2_E = math.log2(math.e)
NEG_BIG = -1e30
VMEM_LIMIT = 56 * 1024 * 1024

SPLIT_TERMS = 2
NORM_TERMS = 1
RW_CHUNK = 64
RW_CHUNKS_PER_STEP = 4
C_TERMS = 3
VT_ROWS = HEAD_DIM + 2 * SUBLANES


def _dot(a, b):
    return jnp.dot(a, b, preferred_element_type=F32)


def _dot_nt(a, b):
    return lax.dot_general(a, b, (((1,), (1,)), ((), ())), preferred_element_type=F32)


def _dot_tn(a, b):
    return lax.dot_general(a, b, (((0,), (0,)), ((), ())), preferred_element_type=F32)


def _split_bf16(x, n):
    parts = []
    rem = x
    for i in range(n):
        p = rem.astype(BF16)
        parts.append(p)
        if i + 1 < n:
            rem = rem - p.astype(F32)
    return parts


def _sigmoid(x):
    return 1.0 / (1.0 + jnp.exp(-x))


def _inproj_kernel(x_ref, g_ref, w_ref, mu_ref, o_ref, carry_ref, *, tiles_per_seq):
    x = x_ref[...]
    rstd = lax.rsqrt(jnp.mean(x * x, axis=-1, keepdims=True) + NORM_EPS)
    z = _dot((x * g_ref[...]).astype(BF16), w_ref[...]) * rstd
    tm = z.shape[0]
    first = (pl.program_id(1) % tiles_per_seq) == 0
    prev_last = jnp.where(first, 0.0, carry_ref[SUBLANES - 1:SUBLANES, :])
    row = lax.broadcasted_iota(jnp.int32, z.shape, 0)
    z_prev = jnp.where(row == 0, prev_last, pltpu.roll(z, 1, 0))
    o_ref[...] = z + (z_prev - z) * mu_ref[...]
    carry_ref[...] = z[tm - SUBLANES:tm, :]


def _inproj(x, g, w, mu, seq, tm, tn):
    m, d = x.shape
    n = w.shape[1]
    return pl.pallas_call(
        functools.partial(_inproj_kernel, tiles_per_seq=seq // tm),
        grid=(n // tn, m // tm),
        in_specs=[pl.BlockSpec((tm, d), lambda j, i: (i, 0)),
                  pl.BlockSpec((1, d), lambda j, i: (0, 0)),
                  pl.BlockSpec((d, tn), lambda j, i: (0, j)),
                  pl.BlockSpec((1, tn), lambda j, i: (0, j))],
        out_specs=pl.BlockSpec((tm, tn), lambda j, i: (i, j)),
        out_shape=jax.ShapeDtypeStruct((m, n), F32),
        scratch_shapes=[pltpu.VMEM((SUBLANES, tn), F32)],
        compiler_params=pltpu.CompilerParams(dimension_semantics=("arbitrary", "arbitrary"),
                                             vmem_limit_bytes=VMEM_LIMIT),
        name="inproj",
    )(x, g.reshape(1, d), w, mu)


def _rwkv_chunk(r, k, v, gate, lora_b, lora_t, wup, aup, w0, a0, kkp, kap, rkp, lng, lnb, s_old, *, c):
    rows, width = r.shape
    n_pairs = width // LANES
    pairs = range(n_pairs)
    subs = range(rows // c)
    blocks = [(i, p) for i in subs for p in pairs]
    cols = lambda x: [x[i * c:(i + 1) * c, p * LANES:(p + 1) * LANES] for i, p in blocks]
    lane = lax.broadcasted_iota(jnp.int32, (1, LANES), 1)
    m0 = lane < HEAD_DIM

    ri = lax.broadcasted_iota(jnp.int32, (LANES, LANES), 0)
    ci = lax.broadcasted_iota(jnp.int32, (LANES, LANES), 1)
    same_head = (ri // HEAD_DIM) == (ci // HEAD_DIM)
    ones_bd = jnp.where(same_head, 1.0, 0.0).astype(BF16)

    def segsum(x, terms):
        xr = jnp.concatenate(cols(x), axis=0)
        sr = sum(_dot(part, ones_bd) for part in _split_bf16(xr, terms))
        return jnp.concatenate([jnp.concatenate([sr[(i * n_pairs + p) * c:(i * n_pairs + p + 1) * c] for p in pairs],
                                                axis=1) for i in subs], axis=0)

    w_raw = w0 + _dot(lora_t, wup)
    logw = -EXP_NEG_HALF * _sigmoid(w_raw)
    a = _sigmoid(a0 + _dot(lora_b, aup))
    kkr = k * kkp
    kk = kkr / jnp.maximum(jnp.sqrt(segsum(kkr * kkr, NORM_TERMS)), 1e-12)
    b = kk * a
    k2 = k * (1.0 + (a - 1.0) * kap)

    ti = lax.broadcasted_iota(jnp.int32, (rows, rows), 0)
    si = lax.broadcasted_iota(jnp.int32, (rows, rows), 1)
    tri = jnp.where((si <= ti) & (si // c == ti // c), 1.0, 0.0).astype(BF16)
    g = sum(_dot(tri, part) for part in _split_bf16(logw, SPLIT_TERMS))
    g_ends = [g[(i + 1) * c - 1:(i + 1) * c, :] for i in subs]
    g_last = jnp.concatenate([jnp.broadcast_to(ge, (c, width)) for ge in g_ends], axis=0)
    egd = jnp.exp(g_last - g)
    eng = jnp.exp(-g)
    rg = cols(r * jnp.exp(g))
    kkg = cols(kk * jnp.exp(g - logw))
    kn = cols(k2 * eng)
    bn = cols(b * eng)
    kd = cols(k2 * egd)
    bdec = cols(b * egd)
    vs = cols(v)
    decay_c = [jnp.exp(g_ends[i][:, p * LANES:(p + 1) * LANES]) for i, p in blocks]
    nb = range(len(blocks))

    def stack(x):
        return jnp.concatenate([jnp.where(m0, x, 0.0), jnp.where(m0, 0.0, x)], axis=0).astype(BF16)

    t_in = lax.broadcasted_iota(jnp.int32, (c, LANES), 0)
    s_in = lax.broadcasted_iota(jnp.int32, (c, LANES), 1) % HEAD_DIM
    strict = s_in < t_in
    incl = s_in <= t_in
    eye = jnp.where(s_in == t_in, 1.0, 0.0)

    lhs_kr = [jnp.concatenate([kkg[b], rg[b]], axis=0).astype(BF16) for b in nb]
    rhs_bk = [jnp.concatenate([stack(bn[b]), stack(kn[b])], axis=0) for b in nb]
    aa = [_dot_nt(lhs_kr[b], rhs_bk[b]) for b in nb]
    a_kb = [jnp.where(strict, aa[b][0:c, 0:LANES], 0.0) for b in nb]
    a_kk = [jnp.where(strict, aa[b][0:c, LANES:2 * LANES], 0.0).astype(BF16) for b in nb]
    a_rb = [jnp.where(incl, aa[b][c:2 * c, 0:LANES], 0.0) for b in nb]
    a_rk = [jnp.where(incl, aa[b][c:2 * c, LANES:2 * LANES], 0.0) for b in nb]
    a_r = [jnp.concatenate([a_rk[b], a_rb[b]], axis=1).astype(BF16) for b in nb]
    v_st = [stack(vs[b]) for b in nb]
    kb = [jnp.concatenate([kd[b], bdec[b]], axis=0).astype(BF16) for b in nb]
    av = [_dot(a_kk[b], v_st[b]) for b in nb]

    tinv = [eye - a_kb[b] for b in nb]
    lpow = [_dot(a_kb[b].astype(BF16), stack(a_kb[b])) for b in nb]
    n_sq = int(math.log2(c)) - 1
    for i in range(n_sq):
        last = i + 1 == n_sq
        w_pow = [stack(lpow[b]) for b in nb]
        lhs = [tinv[b].astype(BF16) if last else jnp.concatenate([tinv[b], lpow[b]], axis=0).astype(BF16)
               for b in nb]
        prod = [_dot(lhs[b], w_pow[b]) for b in nb]
        tinv = [tinv[b] + prod[b][0:c] for b in nb]
        if not last:
            lpow = [prod[b][c:2 * c] for b in nb]
    tinv = [t.astype(BF16) for t in tinv]

    s_cur = list(s_old)
    ys = []
    for i in subs:
        bi = [i * n_pairs + p for p in pairs]
        s_b = [s_cur[p].astype(BF16) for p in pairs]
        sh = [_dot_nt(lhs_kr[bi[p]], s_b[p]) for p in pairs]
        x = [sh[p][0:c] + av[bi[p]] for p in pairs]
        u = [_dot(tinv[bi[p]], stack(x[p])) for p in pairs]
        vu_st = [jnp.concatenate([v_st[bi[p]], stack(-u[p])], axis=0) for p in pairs]
        ys.append(jnp.concatenate([sh[p][c:2 * c] + _dot(a_r[bi[p]], vu_st[p]) for p in pairs], axis=1))
        vu = [jnp.concatenate([vs[bi[p]], -u[p]], axis=0).astype(BF16) for p in pairs]
        s_cur = [s_cur[p] * decay_c[bi[p]] + jnp.where(same_head, _dot_tn(vu[p], kb[bi[p]]), 0.0) for p in pairs]
    y = jnp.concatenate(ys, axis=0)
    s_new = s_cur

    inv_n = 1.0 / HEAD_DIM
    mean = segsum(y, NORM_TERMS) * inv_n
    dlt = y - mean
    var = segsum(dlt * dlt, NORM_TERMS) * inv_n
    yn = dlt * lax.rsqrt(var + GN_EPS) * lng + lnb
    bonus = segsum(r * k2 * rkp, SPLIT_TERMS) * v
    out = (yn + bonus) * (gate * _sigmoid(gate))
    return out, s_new


def _rwkv_kernel(r_ref, k_ref, v_ref, g_ref, lora_ref, wup_ref, aup_ref, w0_ref, a0_ref, kk_ref, ka_ref,
                 rk_ref, lng_ref, lnb_ref, o_ref, s_ref):
    @pl.when(pl.program_id(1) == 0)
    def _():
        s_ref[...] = jnp.zeros_like(s_ref)

    lora = lora_ref[0][:, 0:LANES]
    out, s_new = _rwkv_chunk(
        r_ref[0], k_ref[0], v_ref[0], g_ref[0], lora.astype(BF16), jnp.tanh(lora).astype(BF16),
        wup_ref[...], aup_ref[...], w0_ref[...], a0_ref[...], kk_ref[...], ka_ref[...], rk_ref[...],
        lng_ref[...], lnb_ref[...], [s_ref[p] for p in range(s_ref.shape[0])], c=RW_CHUNK)
    for p in range(s_ref.shape[0]):
        s_ref[p] = s_new[p]
    o_ref[0] = out.astype(o_ref.dtype)


def _rwkv(z3, lora_blk, wup, aup, w0, a0, k_k, k_a, r_k, ln_g, ln_b, width):
    bsz, seq, _ = z3.shape
    c = min(seq, RW_CHUNK * RW_CHUNKS_PER_STEP)
    assert width % LANES == 0
    zspec = lambda j: pl.BlockSpec((1, c, width), lambda b, t, j=j: (b, t, j))
    pspec = lambda rows: pl.BlockSpec((rows, width), lambda b, t: (0, 0))
    return pl.pallas_call(
        _rwkv_kernel,
        grid=(bsz, seq // c),
        in_specs=[zspec(0), zspec(1), zspec(2), zspec(3),
                  pl.BlockSpec((1, c, 2 * LANES), lambda b, t: (b, t, lora_blk)),
                  pspec(LANES), pspec(LANES)] + [pspec(1)] * 7,
        out_specs=pl.BlockSpec((1, c, width), lambda b, t: (b, t, 0)),
        out_shape=jax.ShapeDtypeStruct((bsz, seq, width), BF16),
        scratch_shapes=[pltpu.VMEM((width // LANES, LANES, LANES), F32)],
        compiler_params=pltpu.CompilerParams(dimension_semantics=("arbitrary", "arbitrary"),
                                             vmem_limit_bytes=VMEM_LIMIT),
        name="rwkv7_scan",
    )(z3, z3, z3, z3, z3, wup, aup, w0, a0, k_k, k_a, r_k, ln_g, ln_b)


def _c_lane(pair, h):
    return (1 - h) * HEAD_DIM + C_TERMS * pair


def _fcum_kernel(l_ref, bf_ref, o_ref, carry_ref, *, n_heads):
    @pl.when(pl.program_id(1) == 0)
    def _():
        carry_ref[...] = jnp.zeros_like(carry_ref)

    x = l_ref[0] + bf_ref[...]
    log_f = jnp.minimum(x, 0.0) - jnp.log(1.0 + jnp.exp(-jnp.abs(x)))
    tc = x.shape[0]
    ti = lax.broadcasted_iota(jnp.int32, (tc, tc), 0)
    si = lax.broadcasted_iota(jnp.int32, (tc, tc), 1)
    tril = jnp.where(si <= ti, 1.0, 0.0).astype(BF16)
    c = sum(_dot(tril, part) for part in _split_bf16(log_f, 3)) + carry_ref[0:1, :]
    carry_ref[...] = jnp.broadcast_to(c[tc - 1:tc, :], carry_ref.shape)
    lin = lax.broadcasted_iota(jnp.int32, (LANES, LANES), 0)
    lout = lax.broadcasted_iota(jnp.int32, (LANES, LANES), 1)
    terms = _split_bf16(-LOG2_E * c, C_TERMS)
    sel = [jnp.where((lin < n_heads) & (lout == _c_lane(lin // 2, lin % 2) + i), 1.0, 0.0).astype(BF16)
           for i in range(C_TERMS)]
    o_ref[0] = sum(_dot(terms[i], sel[i]) for i in range(C_TERMS)).astype(o_ref.dtype)


def _fcum(z3, lora_blk, bf_pad, n_heads, tc):
    bsz, seq, _ = z3.shape
    return pl.pallas_call(
        functools.partial(_fcum_kernel, n_heads=n_heads),
        grid=(bsz, seq // tc),
        in_specs=[pl.BlockSpec((1, tc, LANES), lambda b, t: (b, t, 2 * lora_blk + 1)),
                  pl.BlockSpec((1, LANES), lambda b, t: (0, 0))],
        out_specs=pl.BlockSpec((1, tc, LANES), lambda b, t: (b, t, 0)),
        out_shape=jax.ShapeDtypeStruct((bsz, seq, LANES), BF16),
        scratch_shapes=[pltpu.VMEM((SUBLANES, LANES), F32)],
        compiler_params=pltpu.CompilerParams(dimension_semantics=("arbitrary", "arbitrary"),
                                             vmem_limit_bytes=VMEM_LIMIT),
        name="fox_cumsum",
    )(z3, bf_pad)


def _fox_kernel(q_ref, k_ref, v_ref, g_ref, c_ref, o_ref, vt_ref, qt_ref, m_ref, acc_ref, sa_ref, sb_ref, *, tq, tqs,
                t_chunk):
    pair = pl.program_id(1)
    seq = k_ref.shape[1]
    nq = seq // tq
    lane = lax.broadcasted_iota(jnp.int32, (1, LANES), 1)
    row = lax.broadcasted_iota(jnp.int32, (LANES, 1), 0)

    one_row = jnp.where(lax.broadcasted_iota(jnp.int32, (VT_ROWS - HEAD_DIM, t_chunk), 0) == 0, 1.0, 0.0)
    for t0 in range(0, seq, t_chunk):
        v_t = v_ref[0, t0:t0 + t_chunk, :].T
        q_t = (q_ref[0, t0:t0 + t_chunk, :] * (Q_SCALE * LOG2_E)).T
        for h in range(2):
            vt_ref[h, 0:HEAD_DIM, t0:t0 + t_chunk] = v_t[h * HEAD_DIM:(h + 1) * HEAD_DIM].astype(BF16)
            vt_ref[h, HEAD_DIM:VT_ROWS, t0:t0 + t_chunk] = one_row.astype(BF16)
            r0 = _c_lane(pair, h)
            ones = jnp.where((row >= r0) & (row < r0 + C_TERMS), 1.0, 0.0)
            qt_ref[h, :, t0:t0 + t_chunk] = jnp.where((row // HEAD_DIM) == h, q_t, ones).astype(BF16)
    acc_ref[...] = jnp.zeros_like(acc_ref)
    m_ref[...] = jnp.full_like(m_ref, NEG_BIG)

    chains = [(h, slice(q0, q0 + tqs)) for h in range(2) for q0 in range(0, tq, tqs)]

    def keys_seen(qs, diagonal):
        return qs.stop if diagonal else tq

    def scores(qi, j, dst_ref, diagonal):
        ks = pl.multiple_of(j * tq, tq)
        kb = k_ref[0, pl.ds(ks, tq), :].astype(BF16)
        cb = c_ref[0, pl.ds(ks, tq), :]
        ka = [jnp.where((lane // HEAD_DIM) == h, kb, cb) for h in range(2)]
        for i, (h, qs) in enumerate(chains):
            nk = keys_seen(qs, diagonal)
            dst_ref[i, 0:nk, :] = _dot(ka[h][0:nk], qt_ref[h, :, pl.ds(pl.multiple_of(qi * tq + qs.start, tqs), tqs)])

    def update(src_ref, qi, j, diagonal):
        ks = pl.multiple_of(j * tq, tq)
        nk = [keys_seen(qs, diagonal) for h, qs in chains]
        s = [src_ref[i, 0:nk[i], :] for i in range(len(chains))]
        if diagonal:
            s = [jnp.where(lax.broadcasted_iota(jnp.int32, (nk[i], tqs), 0)
                           <= lax.broadcasted_iota(jnp.int32, (nk[i], tqs), 1) + qs.start, s[i], NEG_BIG)
                 for i, (h, qs) in enumerate(chains)]
        m_old = [jnp.where(j == 0, NEG_BIG, m_ref[h, :, qs]) for h, qs in chains]
        m_new = [jnp.maximum(m_old[i], jnp.max(s[i], axis=0, keepdims=True)) for i in range(len(chains))]
        alpha = [jnp.exp2(m_old[i] - m_new[i]) for i in range(len(chains))]
        pexp = [jnp.exp2(s[i] - m_new[i]).astype(BF16) for i in range(len(chains))]
        pv = [_dot(vt_ref[h, :, pl.ds(ks, nk[i])], pexp[i]) for i, (h, qs) in enumerate(chains)]
        for i, (h, qs) in enumerate(chains):
            acc_ref[h, :, qs] = alpha[i] * acc_ref[h, :, qs] + pv[i]
            m_ref[h, :, qs] = m_new[i]
        if diagonal:
            o_t = jnp.concatenate([acc_ref[h, 0:HEAD_DIM, :] / acc_ref[h, HEAD_DIM:HEAD_DIM + 1, :]
                                   for h in range(2)], axis=0)
            q0 = pl.multiple_of(qi * tq, tq)
            gate = g_ref[0, pl.ds(q0, tq), :]
            o_ref[0, pl.ds(q0, tq), :] = (o_t.T * (gate * _sigmoid(gate))).astype(o_ref.dtype)

    def following(qi, j):
        wrap = j == qi
        return jnp.where(wrap, qi + 1, qi), jnp.where(wrap, 0, j + 1)

    def half_step(cur_ref, nxt_ref, qi, j):
        qn, jn = following(qi, j)
        qn_c = jnp.minimum(qn, nq - 1)

        @pl.when(j + 1 < qi)
        def _():
            scores(qn_c, jn, nxt_ref, False)
            update(cur_ref, qi, j, False)

        @pl.when(j + 1 == qi)
        def _():
            scores(qn_c, jn, nxt_ref, True)
            update(cur_ref, qi, j, False)

        @pl.when(j == qi)
        def _():
            scores(qn_c, jn, nxt_ref, False)
            update(cur_ref, qi, j, True)
        return qn, jn

    scores(0, 0, sa_ref, True)

    def body(n, carry):
        qi, j = carry
        qi, j = half_step(sa_ref, sb_ref, qi, j)
        return half_step(sb_ref, sa_ref, qi, j)

    lax.fori_loop(0, nq * (nq + 1) // 4, body, (jnp.int32(0), jnp.int32(0)))


def _fox(z3, c_sp, col0, width, tq):
    bsz, seq, _ = z3.shape
    n_pairs = width // LANES
    nq = seq // tq
    assert (nq * (nq + 1) // 2) % 2 == 0
    blk = lambda j: col0 // LANES + j * n_pairs
    tqs = min(tq, 2 * LANES)
    s_buf = pltpu.VMEM((2 * tq // tqs, tq, tqs), F32)
    zspec = lambda j: pl.BlockSpec((1, seq, LANES), lambda b, p, j=j: (b, 0, blk(j) + p))
    return pl.pallas_call(
        functools.partial(_fox_kernel, tq=tq, tqs=tqs, t_chunk=min(seq, 512)),
        grid=(bsz, n_pairs),
        in_specs=[zspec(0), zspec(1), zspec(2), zspec(3),
                  pl.BlockSpec((1, seq, LANES), lambda b, p: (b, 0, 0))],
        out_specs=pl.BlockSpec((1, seq, LANES), lambda b, p: (b, 0, p)),
        out_shape=jax.ShapeDtypeStruct((bsz, seq, width), BF16),
        scratch_shapes=[pltpu.VMEM((2, VT_ROWS, seq), BF16), pltpu.VMEM((2, LANES, seq), BF16),
                        pltpu.VMEM((2, 1, tq), F32), pltpu.VMEM((2, VT_ROWS, tq), F32), s_buf, s_buf],
        compiler_params=pltpu.CompilerParams(dimension_semantics=("arbitrary",) * 2,
                                             vmem_limit_bytes=VMEM_LIMIT),
        name="fox_attention",
    )(z3, z3, z3, z3, c_sp)


def _tail_kernel(yr_ref, yf_ref, gr_ref, gf_ref, x_ref, p_ref, ng_ref, fg_ref, wr_ref, wf_ref, wo_ref, wp_ref, wg_ref,
                 o_ref):
    u_rw = _dot(yr_ref[...], wr_ref[...])
    u_fox = _dot(yf_ref[...], wf_ref[...])
    merged = _sigmoid(gr_ref[...]) * u_rw + _sigmoid(gf_ref[...]) * u_fox
    x1 = x_ref[...] + _dot(merged.astype(BF16), wo_ref[...])
    hn = x1 * lax.rsqrt(jnp.mean(x1 * x1, axis=-1, keepdims=True) + NORM_EPS) * ng_ref[...]
    gate = _sigmoid(_dot(hn.astype(BF16), wg_ref[...]))
    ple = _dot(p_ref[...].astype(BF16), wp_ref[...])
    x2 = x1 + ple * gate
    o_ref[...] = x2 * lax.rsqrt(jnp.mean(x2 * x2, axis=-1, keepdims=True) + NORM_EPS) * fg_ref[...]


def _tail(y_rw, y_fox, z, gate_col0, x, p, ple_norm_g, final_norm_g, w_up_rw, w_up_fox, w_out, ple_proj, ple_gate_w,
          tm):
    m, d = x.shape
    cw = y_rw.shape[1]
    pd = p.shape[1]
    gblk = gate_col0 // d
    rows = lambda n, j=0: pl.BlockSpec((tm, n), lambda i, j=j: (i, j))
    const = lambda shape: pl.BlockSpec(shape, lambda i: (0, 0), pipeline_mode=pl.Buffered(1))
    return pl.pallas_call(
        _tail_kernel,
        grid=(m // tm,),
        in_specs=[rows(cw), rows(cw), rows(d, gblk), rows(d, gblk + 1), rows(d), rows(pd),
                  const((1, d)), const((1, d)), const((cw, d)), const((cw, d)), const((d, d)), const((pd, d)),
                  const((d, d))],
        out_specs=rows(d),
        out_shape=jax.ShapeDtypeStruct((m, d), F32),
        compiler_params=pltpu.CompilerParams(dimension_semantics=("arbitrary",),
                                             vmem_limit_bytes=VMEM_LIMIT),
        name="merge_ple_final_norm",
    )(y_rw, y_fox, z, z, x, p, ple_norm_g.reshape(1, d), final_norm_g.reshape(1, d), w_up_rw, w_up_fox, w_out,
      ple_proj, ple_gate_w)


def _regroup_kernel(main_ref, lora_ref, fl_ref, o_ref, *, n_heads):
    last = pl.num_programs(0) - 1

    @pl.when(pl.program_id(0) < last)
    def _():
        o_ref[...] = main_ref[...].T.astype(BF16)

    @pl.when(pl.program_id(0) == last)
    def _():
        row = lax.broadcasted_iota(jnp.int32, (LANES, 1), 0)
        o_ref[:, 0:LANES] = lora_ref[...].T.astype(BF16)
        o_ref[:, LANES:2 * LANES] = jnp.where(row < n_heads, fl_ref[...], 0.0).T.astype(BF16)


def _regroup_weights(w_t, cw, lw, n_heads, tr):
    n_in, d = w_t.shape
    fl0 = 8 * cw + 2 * lw
    small0 = 8 * cw + 2 * d
    n_pad = small0 + 2 * LANES
    assert tr == 2 * LANES and (4 * cw) % tr == 0 and (2 * d) % tr == 0 and n_in == fl0 + n_heads + 2 * d
    assert (2 * lw) % SUBLANES == 0 and n_heads % SUBLANES == 0

    def main_rows(i):
        r = i * tr
        src = jnp.where(r < 4 * cw, r, jnp.where(r < 8 * cw, r + 2 * lw,
                                                 jnp.where(r < small0, r + 2 * lw + n_heads, 0)))
        return pl.multiple_of(src, SUBLANES), 0

    rows = lambda n: (pl.Element(n), pl.Element(d))
    return pl.pallas_call(
        functools.partial(_regroup_kernel, n_heads=n_heads),
        grid=(n_pad // tr,),
        in_specs=[pl.BlockSpec(rows(tr), main_rows),
                  pl.BlockSpec(rows(LANES), lambda i: (4 * cw, 0)),
                  pl.BlockSpec(rows(LANES), lambda i: (fl0, 0))],
        out_specs=pl.BlockSpec((d, tr), lambda i: (0, i)),
        out_shape=jax.ShapeDtypeStruct((d, n_pad), BF16),
        compiler_params=pltpu.CompilerParams(dimension_semantics=("arbitrary",),
                                             vmem_limit_bytes=VMEM_LIMIT),
        name="regroup_w_in",
    )(w_t, w_t, w_t)


def _layer(x2d, p2d, seq, norm_g, w_in, rw_shift_mu, rw_w0, rw_w_lora_up, rw_a0, rw_a_lora_up, rw_k_k, rw_k_a,
           rw_r_k, rw_ln_g, rw_ln_b, fox_b_f, w_up_rwkv, w_up_fox, w_out, ple_proj, ple_gate_w, ple_norm_g,
           final_norm_g, *, tiles):
    m, d = x2d.shape
    bsz = m // seq
    cw = w_up_rwkv.shape[0]
    n_heads = fox_b_f.shape[0]
    lw = rw_w_lora_up.shape[0]
    assert 2 * lw == LANES and cw % LANES == 0 and C_TERMS * (cw // LANES) <= HEAD_DIM

    fox0, gate0, small0 = 4 * cw, 8 * cw, 8 * cw + 2 * d
    n_pad = small0 + 2 * LANES
    w_cat = _regroup_weights(w_in.T, cw, lw, n_heads, tiles["regroup_tr"])
    mu_cat = jnp.concatenate([
        rw_shift_mu[0:4 * cw], jnp.zeros((small0 - 4 * cw,), F32), rw_shift_mu[4 * cw:],
        jnp.zeros((LANES,), F32)]).reshape(1, n_pad)

    z = _inproj(x2d, norm_g, w_cat, mu_cat, seq, tiles["in_tm"], tiles["in_tn"])
    z3 = z.reshape(bsz, seq, n_pad)
    lora_blk = small0 // (2 * LANES)

    row = lambda a: a.reshape(1, cw)
    wup = jnp.pad(rw_w_lora_up, ((0, lw), (0, 0))).astype(BF16)
    aup = jnp.pad(rw_a_lora_up, ((lw, 0), (0, 0))).astype(BF16)
    y_rw = _rwkv(z3, lora_blk, wup, aup, row(rw_w0), row(rw_a0), row(rw_k_k), row(rw_k_a), row(rw_r_k),
                 row(rw_ln_g), row(rw_ln_b), cw)

    bf_pad = jnp.pad(fox_b_f, (0, LANES - n_heads)).reshape(1, LANES)
    c_sp = _fcum(z3, lora_blk, bf_pad, n_heads, tiles["fc_tc"])
    y_fox = _fox(z3, c_sp, fox0, cw, tiles["fox_tq"])

    return _tail(y_rw.reshape(m, cw), y_fox.reshape(m, cw), z, gate0, x2d, p2d, ple_norm_g, final_norm_g,
                 w_up_rwkv.astype(BF16), w_up_fox.astype(BF16), w_out.astype(BF16), ple_proj.astype(BF16),
                 ple_gate_w.astype(BF16), tiles["tail_tm"])


_TILES = dict(regroup_tr=256, in_tm=1024, in_tn=1792, fc_tc=512, fox_tq=1024, tail_tm=256)


def kernel(x, p, norm_g, w_in, rw_shift_mu, rw_w0, rw_w_lora_up, rw_a0, rw_a_lora_up, rw_k_k, rw_k_a, rw_r_k,
           rw_ln_g, rw_ln_b, fox_b_f, w_up_rwkv, w_up_fox, w_out, ple_proj, ple_gate_w, ple_norm_g,
           final_norm_g):
    bsz, seq, d = x.shape
    depth = p.shape[0]
    assert depth == 1, "the final norm is fused into the (single) layer"
    out = _layer(x.reshape(bsz * seq, d), p[0].reshape(bsz * seq, -1), seq, norm_g[0], w_in[0], rw_shift_mu[0],
                 rw_w0[0], rw_w_lora_up[0], rw_a0[0], rw_a_lora_up[0], rw_k_k[0], rw_k_a[0],
                 rw_r_k[0].reshape(-1), rw_ln_g[0], rw_ln_b[0], fox_b_f[0], w_up_rwkv[0], w_up_fox[0], w_out[0],
                 ple_proj[0], ple_gate_w[0], ple_norm_g[0], final_norm_g, tiles=_TILES)
    return out.reshape(bsz, seq, d)
```

```python
import functools
import math

import jax
import jax.numpy as jnp
from jax import lax
from jax.experimental import pallas as pl
from jax.experimental.pallas import tpu as pltpu

F32 = jnp.float32
BF16 = jnp.bfloat16

HEAD_DIM = 64
LANES = 128
SUBLANES = 8
NORM_EPS = 1e-6
GN_EPS = 64e-5
Q_SCALE = HEAD_DIM ** -0.5
EXP_NEG_HALF = math.exp(-0.5)
LOG2_E = math.log2(math.e)
NEG_BIG = -1e30
VMEM_LIMIT = 56 * 1024 * 1024

SPLIT_TERMS = 2
NORM_TERMS = 1
RW_CHUNK = 64
RW_CHUNKS_PER_STEP = 4
C_TERMS = 3
VT_ROWS = HEAD_DIM + 2 * SUBLANES


def _dot(a, b):
    return jnp.dot(a, b, preferred_element_type=F32)


def _dot_nt(a, b):
    return lax.dot_general(a, b, (((1,), (1,)), ((), ())), preferred_element_type=F32)


def _dot_tn(a, b):
    return lax.dot_general(a, b, (((0,), (0,)), ((), ())), preferred_element_type=F32)


def _split_bf16(x, n):
    parts = []
    rem = x
    for i in range(n):
        p = rem.astype(BF16)
        parts.append(p)
        if i + 1 < n:
            rem = rem - p.astype(F32)
    return parts


def _sigmoid(x):
    return 1.0 / (1.0 + jnp.exp(-x))


def _inproj_kernel(x_ref, g_ref, w_ref, mu_ref, o_ref, carry_ref, *, tiles_per_seq):
    x = x_ref[...]
    rstd = lax.rsqrt(jnp.mean(x * x, axis=-1, keepdims=True) + NORM_EPS)
    z = _dot((x * g_ref[...]).astype(BF16), w_ref[...]) * rstd
    tm = z.shape[0]
    first = (pl.program_id(1) % tiles_per_seq) == 0
    prev_last = jnp.where(first, 0.0, carry_ref[SUBLANES - 1:SUBLANES, :])
    row = lax.broadcasted_iota(jnp.int32, z.shape, 0)
    z_prev = jnp.where(row == 0, prev_last, pltpu.roll(z, 1, 0))
    o_ref[...] = z + (z_prev - z) * mu_ref[...]
    carry_ref[...] = z[tm - SUBLANES:tm, :]


def _inproj(x, g, w, mu, seq, tm, tn):
    m, d = x.shape
    n = w.shape[1]
    return pl.pallas_call(
        functools.partial(_inproj_kernel, tiles_per_seq=seq // tm),
        grid=(n // tn, m // tm),
        in_specs=[pl.BlockSpec((tm, d), lambda j, i: (i, 0)),
                  pl.BlockSpec((1, d), lambda j, i: (0, 0)),
                  pl.BlockSpec((d, tn), lambda j, i: (0, j)),
                  pl.BlockSpec((1, tn), lambda j, i: (0, j))],
        out_specs=pl.BlockSpec((tm, tn), lambda j, i: (i, j)),
        out_shape=jax.ShapeDtypeStruct((m, n), F32),
        scratch_shapes=[pltpu.VMEM((SUBLANES, tn), F32)],
        compiler_params=pltpu.CompilerParams(dimension_semantics=("arbitrary", "arbitrary"),
                                             vmem_limit_bytes=VMEM_LIMIT),
        name="inproj",
    )(x, g.reshape(1, d), w, mu)


def _rwkv_chunk(r, k, v, gate, lora_b, lora_t, wup, aup, w0, a0, kkp, kap, rkp, lng, lnb, s_old, *, c):
    rows, width = r.shape
    n_pairs = width // LANES
    pairs = range(n_pairs)
    subs = range(rows // c)
    blocks = [(i, p) for i in subs for p in pairs]
    cols = lambda x: [x[i * c:(i + 1) * c, p * LANES:(p + 1) * LANES] for i, p in blocks]
    lane = lax.broadcasted_iota(jnp.int32, (1, LANES), 1)
    m0 = lane < HEAD_DIM

    ri = lax.broadcasted_iota(jnp.int32, (LANES, LANES), 0)
    ci = lax.broadcasted_iota(jnp.int32, (LANES, LANES), 1)
    same_head = (ri // HEAD_DIM) == (ci // HEAD_DIM)
    ones_bd = jnp.where(same_head, 1.0, 0.0).astype(BF16)

    def segsum(x, terms):
        xr = jnp.concatenate(cols(x), axis=0)
        sr = sum(_dot(part, ones_bd) for part in _split_bf16(xr, terms))
        return jnp.concatenate([jnp.concatenate([sr[(i * n_pairs + p) * c:(i * n_pairs + p + 1) * c] for p in pairs],
                                                axis=1) for i in subs], axis=0)

    w_raw = w0 + _dot(lora_t, wup)
    logw = -EXP_NEG_HALF * _sigmoid(w_raw)
    a = _sigmoid(a0 + _dot(lora_b, aup))
    kkr = k * kkp
    kk = kkr / jnp.maximum(jnp.sqrt(segsum(kkr * kkr, NORM_TERMS)), 1e-12)
    b = kk * a
    k2 = k * (1.0 + (a - 1.0) * kap)

    ti = lax.broadcasted_iota(jnp.int32, (rows, rows), 0)
    si = lax.broadcasted_iota(jnp.int32, (rows, rows), 1)
    tri = jnp.where((si <= ti) & (si // c == ti // c), 1.0, 0.0).astype(BF16)
    g = sum(_dot(tri, part) for part in _split_bf16(logw, SPLIT_TERMS))
    g_ends = [g[(i + 1) * c - 1:(i + 1) * c, :] for i in subs]
    g_last = jnp.concatenate([jnp.broadcast_to(ge, (c, width)) for ge in g_ends], axis=0)
    egd = jnp.exp(g_last - g)
    eng = jnp.exp(-g)
    rg = cols(r * jnp.exp(g))
    kkg = cols(kk * jnp.exp(g - logw))
    kn = cols(k2 * eng)
    bn = cols(b * eng)
    kd = cols(k2 * egd)
    bdec = cols(b * egd)
    vs = cols(v)
    decay_c = [jnp.exp(g_ends[i][:, p * LANES:(p + 1) * LANES]) for i, p in blocks]
    nb = range(len(blocks))

    def stack(x):
        return jnp.concatenate([jnp.where(m0, x, 0.0), jnp.where(m0, 0.0, x)], axis=0).astype(BF16)

    t_in = lax.broadcasted_iota(jnp.int32, (c, LANES), 0)
    s_in = lax.broadcasted_iota(jnp.int32, (c, LANES), 1) % HEAD_DIM
    strict = s_in < t_in
    incl = s_in <= t_in
    eye = jnp.where(s_in == t_in, 1.0, 0.0)

    lhs_kr = [jnp.concatenate([kkg[b], rg[b]], axis=0).astype(BF16) for b in nb]
    rhs_bk = [jnp.concatenate([stack(bn[b]), stack(kn[b])], axis=0) for b in nb]
    aa = [_dot_nt(lhs_kr[b], rhs_bk[b]) for b in nb]
    a_kb = [jnp.where(strict, aa[b][0:c, 0:LANES], 0.0) for b in nb]
    a_kk = [jnp.where(strict, aa[b][0:c, LANES:2 * LANES], 0.0).astype(BF16) for b in nb]
    a_rb = [jnp.where(incl, aa[b][c:2 * c, 0:LANES], 0.0) for b in nb]
    a_rk = [jnp.where(incl, aa[b][c:2 * c, LANES:2 * LANES], 0.0) for b in nb]
    a_r = [jnp.concatenate([a_rk[b], a_rb[b]], axis=1).astype(BF16) for b in nb]
    v_st = [stack(vs[b]) for b in nb]
    kb = [jnp.concatenate([kd[b], bdec[b]], axis=0).astype(BF16) for b in nb]
    av = [_dot(a_kk[b], v_st[b]) for b in nb]

    tinv = [eye - a_kb[b] for b in nb]
    lpow = [_dot(a_kb[b].astype(BF16), stack(a_kb[b])) for b in nb]
    n_sq = int(math.log2(c)) - 1
    for i in range(n_sq):
        last = i + 1 == n_sq
        w_pow = [stack(lpow[b]) for b in nb]
        lhs = [tinv[b].astype(BF16) if last else jnp.concatenate([tinv[b], lpow[b]], axis=0).astype(BF16)
               for b in nb]
        prod = [_dot(lhs[b], w_pow[b]) for b in nb]
        tinv = [tinv[b] + prod[b][0:c] for b in nb]
        if not last:
            lpow = [prod[b][c:2 * c] for b in nb]
    tinv = [t.astype(BF16) for t in tinv]

    s_cur = list(s_old)
    ys = []
    for i in subs:
        bi = [i * n_pairs + p for p in pairs]
        s_b = [s_cur[p].astype(BF16) for p in pairs]
        sh = [_dot_nt(lhs_kr[bi[p]], s_b[p]) for p in pairs]
        x = [sh[p][0:c] + av[bi[p]] for p in pairs]
        u = [_dot(tinv[bi[p]], stack(x[p])) for p in pairs]
        vu_st = [jnp.concatenate([v_st[bi[p]], stack(-u[p])], axis=0) for p in pairs]
        ys.append(jnp.concatenate([sh[p][c:2 * c] + _dot(a_r[bi[p]], vu_st[p]) for p in pairs], axis=1))
        vu = [jnp.concatenate([vs[bi[p]], -u[p]], axis=0).astype(BF16) for p in pairs]
        s_cur = [s_cur[p] * decay_c[bi[p]] + jnp.where(same_head, _dot_tn(vu[p], kb[bi[p]]), 0.0) for p in pairs]
    y = jnp.concatenate(ys, axis=0)
    s_new = s_cur

    inv_n = 1.0 / HEAD_DIM
    mean = segsum(y, NORM_TERMS) * inv_n
    dlt = y - mean
    var = segsum(dlt * dlt, NORM_TERMS) * inv_n
    yn = dlt * lax.rsqrt(var + GN_EPS) * lng + lnb
    bonus = segsum(r * k2 * rkp, SPLIT_TERMS) * v
    out = (yn + bonus) * (gate * _sigmoid(gate))
    return out, s_new


def _rwkv_kernel(r_ref, k_ref, v_ref, g_ref, lora_ref, wup_ref, aup_ref, w0_ref, a0_ref, kk_ref, ka_ref,
                 rk_ref, lng_ref, lnb_ref, o_ref, s_ref):
    @pl.when(pl.program_id(1) == 0)
    def _():
        s_ref[...] = jnp.zeros_like(s_ref)

    lora = lora_ref[0][:, 0:LANES]
    out, s_new = _rwkv_chunk(
        r_ref[0], k_ref[0], v_ref[0], g_ref[0], lora.astype(BF16), jnp.tanh(lora).astype(BF16),
        wup_ref[...], aup_ref[...], w0_ref[...], a0_ref[...], kk_ref[...], ka_ref[...], rk_ref[...],
        lng_ref[...], lnb_ref[...], [s_ref[p] for p in range(s_ref.shape[0])], c=RW_CHUNK)
    for p in range(s_ref.shape[0]):
        s_ref[p] = s_new[p]
    o_ref[0] = out.astype(o_ref.dtype)


def _rwkv(z3, lora_blk, wup, aup, w0, a0, k_k, k_a, r_k, ln_g, ln_b, width):
    bsz, seq, _ = z3.shape
    c = min(seq, RW_CHUNK * RW_CHUNKS_PER_STEP)
    assert width % LANES == 0
    zspec = lambda j: pl.BlockSpec((1, c, width), lambda b, t, j=j: (b, t, j))
    pspec = lambda rows: pl.BlockSpec((rows, width), lambda b, t: (0, 0))
    return pl.pallas_call(
        _rwkv_kernel,
        grid=(bsz, seq // c),
        in_specs=[zspec(0), zspec(1), zspec(2), zspec(3),
                  pl.BlockSpec((1, c, 2 * LANES), lambda b, t: (b, t, lora_blk)),
                  pspec(LANES), pspec(LANES)] + [pspec(1)] * 7,
        out_specs=pl.BlockSpec((1, c, width), lambda b, t: (b, t, 0)),
        out_shape=jax.ShapeDtypeStruct((bsz, seq, width), BF16),
        scratch_shapes=[pltpu.VMEM((width // LANES, LANES, LANES), F32)],
        compiler_params=pltpu.CompilerParams(dimension_semantics=("arbitrary", "arbitrary"),
                                             vmem_limit_bytes=VMEM_LIMIT),
        name="rwkv7_scan",
    )(z3, z3, z3, z3, z3, wup, aup, w0, a0, k_k, k_a, r_k, ln_g, ln_b)


def _c_lane(pair, h):
    return (1 - h) * HEAD_DIM + C_TERMS * pair


def _fcum_kernel(l_ref, bf_ref, o_ref, *, n_heads, tc):
    seq = l_ref.shape[1]
    blks = range(seq // tc)
    ti = lax.broadcasted_iota(jnp.int32, (tc, tc), 0)
    si = lax.broadcasted_iota(jnp.int32, (tc, tc), 1)
    tril = jnp.where(si <= ti, 1.0, 0.0).astype(BF16)
    x = [l_ref[0, k * tc:(k + 1) * tc, :] + bf_ref[...] for k in blks]
    log_f = [jnp.minimum(x[k], 0.0) - jnp.log(1.0 + jnp.exp(-jnp.abs(x[k]))) for k in blks]
    terms = [_split_bf16(log_f[k], 3) for k in blks]
    local = [sum(_dot(tril, part) for part in terms[k]) for k in blks]
    c = []
    total = jnp.zeros((1, LANES), F32)
    for k in blks:
        c.append(local[k] + total)
        total = total + local[k][tc - 1:tc, :]
    lin = lax.broadcasted_iota(jnp.int32, (LANES, LANES), 0)
    lout = lax.broadcasted_iota(jnp.int32, (LANES, LANES), 1)
    sel = [jnp.where((lin < n_heads) & (lout == _c_lane(lin // 2, lin % 2) + i), 1.0, 0.0).astype(BF16)
           for i in range(C_TERMS)]
    c_terms = [_split_bf16(-LOG2_E * c[k], C_TERMS) for k in blks]
    for k in blks:
        o_ref[0, k * tc:(k + 1) * tc, :] = sum(_dot(c_terms[k][i], sel[i]) for i in range(C_TERMS)).astype(o_ref.dtype)


def _fcum(z3, lora_blk, bf_pad, n_heads, tc):
    bsz, seq, _ = z3.shape
    return pl.pallas_call(
        functools.partial(_fcum_kernel, n_heads=n_heads, tc=tc),
        grid=(bsz,),
        in_specs=[pl.BlockSpec((1, seq, LANES), lambda b: (b, 0, 2 * lora_blk + 1)),
                  pl.BlockSpec((1, LANES), lambda b: (0, 0))],
        out_specs=pl.BlockSpec((1, seq, LANES), lambda b: (b, 0, 0)),
        out_shape=jax.ShapeDtypeStruct((bsz, seq, LANES), BF16),
        compiler_params=pltpu.CompilerParams(dimension_semantics=("arbitrary",),
                                             vmem_limit_bytes=VMEM_LIMIT),
        name="fox_cumsum",
    )(z3, bf_pad)


def _fox_kernel(q_ref, k_ref, v_ref, g_ref, c_ref, o_ref, vt_ref, qt_ref, m_ref, acc_ref, sa_ref, sb_ref, *, tq, tqs,
                t_chunk):
    pair = pl.program_id(1)
    seq = k_ref.shape[1]
    nq = seq // tq
    lane = lax.broadcasted_iota(jnp.int32, (1, LANES), 1)
    row = lax.broadcasted_iota(jnp.int32, (LANES, 1), 0)

    one_row = jnp.where(lax.broadcasted_iota(jnp.int32, (VT_ROWS - HEAD_DIM, t_chunk), 0) == 0, 1.0, 0.0)
    for t0 in range(0, seq, t_chunk):
        v_t = v_ref[0, t0:t0 + t_chunk, :].T
        q_t = (q_ref[0, t0:t0 + t_chunk, :] * (Q_SCALE * LOG2_E)).T
        for h in range(2):
            vt_ref[h, 0:HEAD_DIM, t0:t0 + t_chunk] = v_t[h * HEAD_DIM:(h + 1) * HEAD_DIM].astype(BF16)
            vt_ref[h, HEAD_DIM:VT_ROWS, t0:t0 + t_chunk] = one_row.astype(BF16)
            r0 = _c_lane(pair, h)
            ones = jnp.where((row >= r0) & (row < r0 + C_TERMS), 1.0, 0.0)
            qt_ref[h, :, t0:t0 + t_chunk] = jnp.where((row // HEAD_DIM) == h, q_t, ones).astype(BF16)
    acc_ref[...] = jnp.zeros_like(acc_ref)
    m_ref[...] = jnp.full_like(m_ref, NEG_BIG)

    chains = [(h, slice(q0, q0 + tqs)) for h in range(2) for q0 in range(0, tq, tqs)]

    def keys_seen(qs, diagonal):
        return qs.stop if diagonal else tq

    def scores(qi, j, dst_ref, diagonal):
        ks = pl.multiple_of(j * tq, tq)
        kb = k_ref[0, pl.ds(ks, tq), :].astype(BF16)
        cb = c_ref[0, pl.ds(ks, tq), :]
        ka = [jnp.where((lane // HEAD_DIM) == h, kb, cb) for h in range(2)]
        for i, (h, qs) in enumerate(chains):
            nk = keys_seen(qs, diagonal)
            dst_ref[i, 0:nk, :] = _dot(ka[h][0:nk], qt_ref[h, :, pl.ds(pl.multiple_of(qi * tq + qs.start, tqs), tqs)])

    def update(src_ref, qi, j, diagonal):
        ks = pl.multiple_of(j * tq, tq)
        nk = [keys_seen(qs, diagonal) for h, qs in chains]
        s = [src_ref[i, 0:nk[i], :] for i in range(len(chains))]
        if diagonal:
            s = [jnp.where(lax.broadcasted_iota(jnp.int32, (nk[i], tqs), 0)
                           <= lax.broadcasted_iota(jnp.int32, (nk[i], tqs), 1) + qs.start, s[i], NEG_BIG)
                 for i, (h, qs) in enumerate(chains)]
        m_old = [jnp.where(j == 0, NEG_BIG, m_ref[h, :, qs]) for h, qs in chains]
        m_new = [jnp.maximum(m_old[i], jnp.max(s[i], axis=0, keepdims=True)) for i in range(len(chains))]
        alpha = [jnp.exp2(m_old[i] - m_new[i]) for i in range(len(chains))]
        pexp = [jnp.exp2(s[i] - m_new[i]).astype(BF16) for i in range(len(chains))]
        pv = [_dot(vt_ref[h, :, pl.ds(ks, nk[i])], pexp[i]) for i, (h, qs) in enumerate(chains)]
        for i, (h, qs) in enumerate(chains):
            acc_ref[h, :, qs] = alpha[i] * acc_ref[h, :, qs] + pv[i]
            m_ref[h, :, qs] = m_new[i]
        if diagonal:
            o_t = jnp.concatenate([acc_ref[h, 0:HEAD_DIM, :] / acc_ref[h, HEAD_DIM:HEAD_DIM + 1, :]
                                   for h in range(2)], axis=0)
            q0 = pl.multiple_of(qi * tq, tq)
            gate = g_ref[0, pl.ds(q0, tq), :]
            o_ref[0, pl.ds(q0, tq), :] = (o_t.T * (gate * _sigmoid(gate))).astype(o_ref.dtype)

    def following(qi, j):
        wrap = j == qi
        return jnp.where(wrap, qi + 1, qi), jnp.where(wrap, 0, j + 1)

    def half_step(cur_ref, nxt_ref, qi, j):
        qn, jn = following(qi, j)
        qn_c = jnp.minimum(qn, nq - 1)

        @pl.when(j + 1 < qi)
        def _():
            scores(qn_c, jn, nxt_ref, False)
            update(cur_ref, qi, j, False)

        @pl.when(j + 1 == qi)
        def _():
            scores(qn_c, jn, nxt_ref, True)
            update(cur_ref, qi, j, False)

        @pl.when(j == qi)
        def _():
            scores(qn_c, jn, nxt_ref, False)
            update(cur_ref, qi, j, True)
        return qn, jn

    scores(0, 0, sa_ref, True)

    def body(n, carry):
        qi, j = carry
        qi, j = half_step(sa_ref, sb_ref, qi, j)
        return half_step(sb_ref, sa_ref, qi, j)

    lax.fori_loop(0, nq * (nq + 1) // 4, body, (jnp.int32(0), jnp.int32(0)))


def _fox(z3, c_sp, col0, width, tq):
    bsz, seq, _ = z3.shape
    n_pairs = width // LANES
    nq = seq // tq
    assert (nq * (nq + 1) // 2) % 2 == 0
    blk = lambda j: col0 // LANES + j * n_pairs
    tqs = min(tq, 2 * LANES)
    s_buf = pltpu.VMEM((2 * tq // tqs, tq, tqs), F32)
    zspec = lambda j: pl.BlockSpec((1, seq, LANES), lambda b, p, j=j: (b, 0, blk(j) + p))
    return pl.pallas_call(
        functools.partial(_fox_kernel, tq=tq, tqs=tqs, t_chunk=min(seq, 512)),
        grid=(bsz, n_pairs),
        in_specs=[zspec(0), zspec(1), zspec(2), zspec(3),
                  pl.BlockSpec((1, seq, LANES), lambda b, p: (b, 0, 0))],
        out_specs=pl.BlockSpec((1, seq, LANES), lambda b, p: (b, 0, p)),
        out_shape=jax.ShapeDtypeStruct((bsz, seq, width), BF16),
        scratch_shapes=[pltpu.VMEM((2, VT_ROWS, seq), BF16), pltpu.VMEM((2, LANES, seq), BF16),
                        pltpu.VMEM((2, 1, tq), F32), pltpu.VMEM((2, VT_ROWS, tq), F32), s_buf, s_buf],
        compiler_params=pltpu.CompilerParams(dimension_semantics=("arbitrary",) * 2,
                                             vmem_limit_bytes=VMEM_LIMIT),
        name="fox_attention",
    )(z3, z3, z3, z3, c_sp)


def _tail_kernel(yr_ref, yf_ref, gr_ref, gf_ref, x_ref, p_ref, ng_ref, fg_ref, wr_ref, wf_ref, wo_ref, wp_ref, wg_ref,
                 o_ref):
    u_rw = _dot(yr_ref[...], wr_ref[...])
    u_fox = _dot(yf_ref[...], wf_ref[...])
    merged = _sigmoid(gr_ref[...]) * u_rw + _sigmoid(gf_ref[...]) * u_fox
    x1 = x_ref[...] + _dot(merged.astype(BF16), wo_ref[...])
    hn = x1 * lax.rsqrt(jnp.mean(x1 * x1, axis=-1, keepdims=True) + NORM_EPS) * ng_ref[...]
    gate = _sigmoid(_dot(hn.astype(BF16), wg_ref[...]))
    ple = _dot(p_ref[...].astype(BF16), wp_ref[...])
    x2 = x1 + ple * gate
    o_ref[...] = x2 * lax.rsqrt(jnp.mean(x2 * x2, axis=-1, keepdims=True) + NORM_EPS) * fg_ref[...]


def _tail(y_rw, y_fox, z, gate_col0, x, p, ple_norm_g, final_norm_g, w_up_rw, w_up_fox, w_out, ple_proj, ple_gate_w,
          tm):
    m, d = x.shape
    cw = y_rw.shape[1]
    pd = p.shape[1]
    gblk = gate_col0 // d
    rows = lambda n, j=0: pl.BlockSpec((tm, n), lambda i, j=j: (i, j))
    const = lambda shape: pl.BlockSpec(shape, lambda i: (0, 0), pipeline_mode=pl.Buffered(1))
    return pl.pallas_call(
        _tail_kernel,
        grid=(m // tm,),
        in_specs=[rows(cw), rows(cw), rows(d, gblk), rows(d, gblk + 1), rows(d), rows(pd),
                  const((1, d)), const((1, d)), const((cw, d)), const((cw, d)), const((d, d)), const((pd, d)),
                  const((d, d))],
        out_specs=rows(d),
        out_shape=jax.ShapeDtypeStruct((m, d), F32),
        compiler_params=pltpu.CompilerParams(dimension_semantics=("arbitrary",),
                                             vmem_limit_bytes=VMEM_LIMIT),
        name="merge_ple_final_norm",
    )(y_rw, y_fox, z, z, x, p, ple_norm_g.reshape(1, d), final_norm_g.reshape(1, d), w_up_rw, w_up_fox, w_out,
      ple_proj, ple_gate_w)


def _regroup_kernel(main_ref, lora_ref, fl_ref, o_ref, *, n_heads):
    last = pl.num_programs(0) - 1

    @pl.when(pl.program_id(0) < last)
    def _():
        o_ref[...] = main_ref[...].T.astype(BF16)

    @pl.when(pl.program_id(0) == last)
    def _():
        row = lax.broadcasted_iota(jnp.int32, (LANES, 1), 0)
        o_ref[:, 0:LANES] = lora_ref[...].T.astype(BF16)
        o_ref[:, LANES:2 * LANES] = jnp.where(row < n_heads, fl_ref[...], 0.0).T.astype(BF16)


def _regroup_weights(w_t, cw, lw, n_heads, tr):
    n_in, d = w_t.shape
    fl0 = 8 * cw + 2 * lw
    small0 = 8 * cw + 2 * d
    n_pad = small0 + 2 * LANES
    assert tr == 2 * LANES and (4 * cw) % tr == 0 and (2 * d) % tr == 0 and n_in == fl0 + n_heads + 2 * d
    assert (2 * lw) % SUBLANES == 0 and n_heads % SUBLANES == 0

    def main_rows(i):
        r = i * tr
        src = jnp.where(r < 4 * cw, r, jnp.where(r < 8 * cw, r + 2 * lw,
                                                 jnp.where(r < small0, r + 2 * lw + n_heads, 0)))
        return pl.multiple_of(src, SUBLANES), 0

    rows = lambda n: (pl.Element(n), pl.Element(d))
    return pl.pallas_call(
        functools.partial(_regroup_kernel, n_heads=n_heads),
        grid=(n_pad // tr,),
        in_specs=[pl.BlockSpec(rows(tr), main_rows),
                  pl.BlockSpec(rows(LANES), lambda i: (4 * cw, 0)),
                  pl.BlockSpec(rows(LANES), lambda i: (fl0, 0))],
        out_specs=pl.BlockSpec((d, tr), lambda i: (0, i)),
        out_shape=jax.ShapeDtypeStruct((d, n_pad), BF16),
        compiler_params=pltpu.CompilerParams(dimension_semantics=("arbitrary",),
                                             vmem_limit_bytes=VMEM_LIMIT),
        name="regroup_w_in",
    )(w_t, w_t, w_t)


def _cast_kernel(*refs):
    n = len(refs) // 2
    for i_ref, o_ref in zip(refs[:n], refs[n:]):
        o_ref[...] = i_ref[...].astype(o_ref.dtype)


def _cast_bf16(weights, steps):
    specs = [pl.BlockSpec((w.shape[0] // steps, w.shape[1]), lambda i: (i, 0)) for w in weights]
    assert all(w.shape[0] % (steps * 2 * SUBLANES) == 0 for w in weights)
    return pl.pallas_call(
        _cast_kernel,
        grid=(steps,),
        in_specs=specs,
        out_specs=specs,
        out_shape=[jax.ShapeDtypeStruct(w.shape, BF16) for w in weights],
        compiler_params=pltpu.CompilerParams(dimension_semantics=("arbitrary",),
                                             vmem_limit_bytes=VMEM_LIMIT),
        name="cast_tail_weights",
    )(*weights)


def _layer(x2d, p2d, seq, norm_g, w_in, rw_shift_mu, rw_w0, rw_w_lora_up, rw_a0, rw_a_lora_up, rw_k_k, rw_k_a,
           rw_r_k, rw_ln_g, rw_ln_b, fox_b_f, w_up_rwkv, w_up_fox, w_out, ple_proj, ple_gate_w, ple_norm_g,
           final_norm_g, *, tiles):
    m, d = x2d.shape
    bsz = m // seq
    cw = w_up_rwkv.shape[0]
    n_heads = fox_b_f.shape[0]
    lw = rw_w_lora_up.shape[0]
    assert 2 * lw == LANES and cw % LANES == 0 and C_TERMS * (cw // LANES) <= HEAD_DIM

    fox0, gate0, small0 = 4 * cw, 8 * cw, 8 * cw + 2 * d
    n_pad = small0 + 2 * LANES
    w_cat = _regroup_weights(w_in.T, cw, lw, n_heads, tiles["regroup_tr"])
    mu_cat = jnp.concatenate([
        rw_shift_mu[0:4 * cw], jnp.zeros((small0 - 4 * cw,), F32), rw_shift_mu[4 * cw:],
        jnp.zeros((LANES,), F32)]).reshape(1, n_pad)

    z = _inproj(x2d, norm_g, w_cat, mu_cat, seq, tiles["in_tm"], tiles["in_tn"])
    z3 = z.reshape(bsz, seq, n_pad)
    lora_blk = small0 // (2 * LANES)

    row = lambda a: a.reshape(1, cw)
    wup = jnp.pad(rw_w_lora_up, ((0, lw), (0, 0))).astype(BF16)
    aup = jnp.pad(rw_a_lora_up, ((lw, 0), (0, 0))).astype(BF16)
    y_rw = _rwkv(z3, lora_blk, wup, aup, row(rw_w0), row(rw_a0), row(rw_k_k), row(rw_k_a), row(rw_r_k),
                 row(rw_ln_g), row(rw_ln_b), cw)

    bf_pad = jnp.pad(fox_b_f, (0, LANES - n_heads)).reshape(1, LANES)
    c_sp = _fcum(z3, lora_blk, bf_pad, n_heads, tiles["fc_tc"])
    y_fox = _fox(z3, c_sp, fox0, cw, tiles["fox_tq"])

    tail_w = _cast_bf16([w_up_rwkv, w_up_fox, w_out, ple_proj, ple_gate_w], tiles["cast_steps"])
    return _tail(y_rw.reshape(m, cw), y_fox.reshape(m, cw), z, gate0, x2d, p2d, ple_norm_g, final_norm_g,
                 *tail_w, tiles["tail_tm"])


_TILES = dict(regroup_tr=256, in_tm=1024, in_tn=1792, fc_tc=512, fox_tq=1024, tail_tm=256, cast_steps=8)


def kernel(x, p, norm_g, w_in, rw_shift_mu, rw_w0, rw_w_lora_up, rw_a0, rw_a_lora_up, rw_k_k, rw_k_a, rw_r_k,
           rw_ln_g, rw_ln_b, fox_b_f, w_up_rwkv, w_up_fox, w_out, ple_proj, ple_gate_w, ple_norm_g,
           final_norm_g):
    bsz, seq, d = x.shape
    depth = p.shape[0]
    assert depth == 1, "the final norm is fused into the (single) layer"
    out = _layer(x.reshape(bsz * seq, d), p[0].reshape(bsz * seq, -1), seq, norm_g[0], w_in[0], rw_shift_mu[0],
                 rw_w0[0], rw_w_lora_up[0], rw_a0[0], rw_a_lora_up[0], rw_k_k[0], rw_k_a[0],
                 rw_r_k[0].reshape(-1), rw_ln_g[0], rw_ln_b[0], fox_b_f[0], w_up_rwkv[0], w_up_fox[0], w_out[0],
                 ple_proj[0], ple_gate_w[0], ple_norm_g[0], final_norm_g, tiles=_TILES)
    return out.reshape(bsz, seq, d)
```

```python
import functools
import math

import jax
import jax.numpy as jnp
from jax import lax
from jax.experimental import pallas as pl
from jax.experimental.pallas import tpu as pltpu

F32 = jnp.float32
BF16 = jnp.bfloat16

HEAD_DIM = 64
LANES = 128
SUBLANES = 8
NORM_EPS = 1e-6
GN_EPS = 64e-5
Q_SCALE = HEAD_DIM ** -0.5
EXP_NEG_HALF = math.exp(-0.5)
LOG2_E = math.log2(math.e)
NEG_BIG = -1e30
VMEM_LIMIT = 56 * 1024 * 1024

SPLIT_TERMS = 2
NORM_TERMS = 1
RW_CHUNK = 64
RW_CHUNKS_PER_STEP = 4
C_TERMS = 3
VT_ROWS = HEAD_DIM + 2 * SUBLANES


def _dot(a, b):
    return jnp.dot(a, b, preferred_element_type=F32)


def _dot_nt(a, b):
    return lax.dot_general(a, b, (((1,), (1,)), ((), ())), preferred_element_type=F32)


def _dot_tn(a, b):
    return lax.dot_general(a, b, (((0,), (0,)), ((), ())), preferred_element_type=F32)


def _split_bf16(x, n):
    parts = []
    rem = x
    for i in range(n):
        p = rem.astype(BF16)
        parts.append(p)
        if i + 1 < n:
            rem = rem - p.astype(F32)
    return parts


def _sigmoid(x):
    return 1.0 / (1.0 + jnp.exp(-x))


def _inproj_kernel(x_ref, g_ref, w_ref, mu_ref, o_ref, carry_ref, *, tiles_per_seq):
    x = x_ref[...]
    rstd = lax.rsqrt(jnp.mean(x * x, axis=-1, keepdims=True) + NORM_EPS)
    z = _dot((x * g_ref[...]).astype(BF16), w_ref[...]) * rstd
    tm = z.shape[0]
    first = (pl.program_id(1) % tiles_per_seq) == 0
    prev_last = jnp.where(first, 0.0, carry_ref[SUBLANES - 1:SUBLANES, :])
    row = lax.broadcasted_iota(jnp.int32, z.shape, 0)
    z_prev = jnp.where(row == 0, prev_last, pltpu.roll(z, 1, 0))
    o_ref[...] = z + (z_prev - z) * mu_ref[...]
    carry_ref[...] = z[tm - SUBLANES:tm, :]


def _inproj(x, g, w, mu, seq, tm, tn):
    m, d = x.shape
    n = w.shape[1]
    return pl.pallas_call(
        functools.partial(_inproj_kernel, tiles_per_seq=seq // tm),
        grid=(n // tn, m // tm),
        in_specs=[pl.BlockSpec((tm, d), lambda j, i: (i, 0)),
                  pl.BlockSpec((1, d), lambda j, i: (0, 0)),
                  pl.BlockSpec((d, tn), lambda j, i: (0, j)),
                  pl.BlockSpec((1, tn), lambda j, i: (0, j))],
        out_specs=pl.BlockSpec((tm, tn), lambda j, i: (i, j)),
        out_shape=jax.ShapeDtypeStruct((m, n), F32),
        scratch_shapes=[pltpu.VMEM((SUBLANES, tn), F32)],
        compiler_params=pltpu.CompilerParams(dimension_semantics=("arbitrary", "arbitrary"),
                                             vmem_limit_bytes=VMEM_LIMIT),
        name="inproj",
    )(x, g.reshape(1, d), w, mu)


def _rwkv_chunk(r, k, v, gate, lora_b, lora_t, wup, aup, w0, a0, kkp, kap, rkp, lng, lnb, s_old, *, c):
    rows, width = r.shape
    n_pairs = width // LANES
    pairs = range(n_pairs)
    subs = range(rows // c)
    blocks = [(i, p) for i in subs for p in pairs]
    cols = lambda x: [x[i * c:(i + 1) * c, p * LANES:(p + 1) * LANES] for i, p in blocks]
    lane = lax.broadcasted_iota(jnp.int32, (1, LANES), 1)
    m0 = lane < HEAD_DIM

    ri = lax.broadcasted_iota(jnp.int32, (LANES, LANES), 0)
    ci = lax.broadcasted_iota(jnp.int32, (LANES, LANES), 1)
    same_head = (ri // HEAD_DIM) == (ci // HEAD_DIM)
    ones_bd = jnp.where(same_head, 1.0, 0.0).astype(BF16)

    def segsum(x, terms):
        xr = jnp.concatenate(cols(x), axis=0)
        sr = sum(_dot(part, ones_bd) for part in _split_bf16(xr, terms))
        return jnp.concatenate([jnp.concatenate([sr[(i * n_pairs + p) * c:(i * n_pairs + p + 1) * c] for p in pairs],
                                                axis=1) for i in subs], axis=0)

    w_raw = w0 + _dot(lora_t, wup)
    logw = -EXP_NEG_HALF * _sigmoid(w_raw)
    a = _sigmoid(a0 + _dot(lora_b, aup))
    kkr = k * kkp
    kk = kkr / jnp.maximum(jnp.sqrt(segsum(kkr * kkr, NORM_TERMS)), 1e-12)
    b = kk * a
    k2 = k * (1.0 + (a - 1.0) * kap)

    ti = lax.broadcasted_iota(jnp.int32, (rows, rows), 0)
    si = lax.broadcasted_iota(jnp.int32, (rows, rows), 1)
    tri = jnp.where((si <= ti) & (si // c == ti // c), 1.0, 0.0).astype(BF16)
    g = sum(_dot(tri, part) for part in _split_bf16(logw, SPLIT_TERMS))
    g_ends = [g[(i + 1) * c - 1:(i + 1) * c, :] for i in subs]
    g_last = jnp.concatenate([jnp.broadcast_to(ge, (c, width)) for ge in g_ends], axis=0)
    egd = jnp.exp(g_last - g)
    eng = jnp.exp(-g)
    rg = cols(r * jnp.exp(g))
    kkg = cols(kk * jnp.exp(g - logw))
    kn = cols(k2 * eng)
    bn = cols(b * eng)
    kd = cols(k2 * egd)
    bdec = cols(b * egd)
    vs = cols(v)
    decay_c = [jnp.exp(g_ends[i][:, p * LANES:(p + 1) * LANES]) for i, p in blocks]
    nb = range(len(blocks))

    def stack(x):
        return jnp.concatenate([jnp.where(m0, x, 0.0), jnp.where(m0, 0.0, x)], axis=0).astype(BF16)

    t_in = lax.broadcasted_iota(jnp.int32, (c, LANES), 0)
    s_in = lax.broadcasted_iota(jnp.int32, (c, LANES), 1) % HEAD_DIM
    strict = s_in < t_in
    incl = s_in <= t_in
    eye = jnp.where(s_in == t_in, 1.0, 0.0)

    lhs_kr = [jnp.concatenate([kkg[b], rg[b]], axis=0).astype(BF16) for b in nb]
    rhs_bk = [jnp.concatenate([stack(bn[b]), stack(kn[b])], axis=0) for b in nb]
    aa = [_dot_nt(lhs_kr[b], rhs_bk[b]) for b in nb]
    a_kb = [jnp.where(strict, aa[b][0:c, 0:LANES], 0.0) for b in nb]
    a_kk = [jnp.where(strict, aa[b][0:c, LANES:2 * LANES], 0.0).astype(BF16) for b in nb]
    a_rb = [jnp.where(incl, aa[b][c:2 * c, 0:LANES], 0.0) for b in nb]
    a_rk = [jnp.where(incl, aa[b][c:2 * c, LANES:2 * LANES], 0.0) for b in nb]
    a_r = [jnp.concatenate([a_rk[b], a_rb[b]], axis=1).astype(BF16) for b in nb]
    v_st = [stack(vs[b]) for b in nb]
    kb = [jnp.concatenate([kd[b], bdec[b]], axis=0).astype(BF16) for b in nb]
    av = [_dot(a_kk[b], v_st[b]) for b in nb]

    tinv = [eye - a_kb[b] for b in nb]
    lpow = [_dot(a_kb[b].astype(BF16), stack(a_kb[b])) for b in nb]
    n_sq = int(math.log2(c)) - 1
    for i in range(n_sq):
        last = i + 1 == n_sq
        w_pow = [stack(lpow[b]) for b in nb]
        lhs = [tinv[b].astype(BF16) if last else jnp.concatenate([tinv[b], lpow[b]], axis=0).astype(BF16)
               for b in nb]
        prod = [_dot(lhs[b], w_pow[b]) for b in nb]
        tinv = [tinv[b] + prod[b][0:c] for b in nb]
        if not last:
            lpow = [prod[b][c:2 * c] for b in nb]
    tinv = [t.astype(BF16) for t in tinv]

    s_cur = list(s_old)
    ys = []
    for i in subs:
        bi = [i * n_pairs + p for p in pairs]
        s_b = [s_cur[p].astype(BF16) for p in pairs]
        sh = [_dot_nt(lhs_kr[bi[p]], s_b[p]) for p in pairs]
        x = [sh[p][0:c] + av[bi[p]] for p in pairs]
        u = [_dot(tinv[bi[p]], stack(x[p])) for p in pairs]
        vu_st = [jnp.concatenate([v_st[bi[p]], stack(-u[p])], axis=0) for p in pairs]
        ys.append(jnp.concatenate([sh[p][c:2 * c] + _dot(a_r[bi[p]], vu_st[p]) for p in pairs], axis=1))
        vu = [jnp.concatenate([vs[bi[p]], -u[p]], axis=0).astype(BF16) for p in pairs]
        s_cur = [s_cur[p] * decay_c[bi[p]] + jnp.where(same_head, _dot_tn(vu[p], kb[bi[p]]), 0.0) for p in pairs]
    y = jnp.concatenate(ys, axis=0)
    s_new = s_cur

    inv_n = 1.0 / HEAD_DIM
    mean = segsum(y, NORM_TERMS) * inv_n
    dlt = y - mean
    var = segsum(dlt * dlt, NORM_TERMS) * inv_n
    yn = dlt * lax.rsqrt(var + GN_EPS) * lng + lnb
    bonus = segsum(r * k2 * rkp, SPLIT_TERMS) * v
    out = (yn + bonus) * (gate * _sigmoid(gate))
    return out, s_new


def _rwkv_kernel(r_ref, k_ref, v_ref, g_ref, lora_ref, wup_ref, aup_ref, w0_ref, a0_ref, kk_ref, ka_ref,
                 rk_ref, lng_ref, lnb_ref, o_ref, s_ref):
    @pl.when(pl.program_id(1) == 0)
    def _():
        s_ref[...] = jnp.zeros_like(s_ref)

    lora = lora_ref[0][:, 0:LANES]
    out, s_new = _rwkv_chunk(
        r_ref[0], k_ref[0], v_ref[0], g_ref[0], lora.astype(BF16), jnp.tanh(lora).astype(BF16),
        wup_ref[...], aup_ref[...], w0_ref[...], a0_ref[...], kk_ref[...], ka_ref[...], rk_ref[...],
        lng_ref[...], lnb_ref[...], [s_ref[p] for p in range(s_ref.shape[0])], c=RW_CHUNK)
    for p in range(s_ref.shape[0]):
        s_ref[p] = s_new[p]
    o_ref[0] = out.astype(o_ref.dtype)


def _rwkv(z3, lora_blk, wup, aup, w0, a0, k_k, k_a, r_k, ln_g, ln_b, width):
    bsz, seq, _ = z3.shape
    c = min(seq, RW_CHUNK * RW_CHUNKS_PER_STEP)
    assert width % LANES == 0
    zspec = lambda j: pl.BlockSpec((1, c, width), lambda b, t, j=j: (b, t, j))
    pspec = lambda rows: pl.BlockSpec((rows, width), lambda b, t: (0, 0))
    return pl.pallas_call(
        _rwkv_kernel,
        grid=(bsz, seq // c),
        in_specs=[zspec(0), zspec(1), zspec(2), zspec(3),
                  pl.BlockSpec((1, c, 2 * LANES), lambda b, t: (b, t, lora_blk)),
                  pspec(LANES), pspec(LANES)] + [pspec(1)] * 7,
        out_specs=pl.BlockSpec((1, c, width), lambda b, t: (b, t, 0)),
        out_shape=jax.ShapeDtypeStruct((bsz, seq, width), BF16),
        scratch_shapes=[pltpu.VMEM((width // LANES, LANES, LANES), F32)],
        compiler_params=pltpu.CompilerParams(dimension_semantics=("arbitrary", "arbitrary"),
                                             vmem_limit_bytes=VMEM_LIMIT),
        name="rwkv7_scan",
    )(z3, z3, z3, z3, z3, wup, aup, w0, a0, k_k, k_a, r_k, ln_g, ln_b)


def _c_lane(pair, h):
    return (1 - h) * HEAD_DIM + C_TERMS * pair


def _fcum_kernel(l_ref, bf_ref, o_ref, *, n_heads, tc):
    seq = l_ref.shape[1]
    blks = range(seq // tc)
    ti = lax.broadcasted_iota(jnp.int32, (tc, tc), 0)
    si = lax.broadcasted_iota(jnp.int32, (tc, tc), 1)
    tril = jnp.where(si <= ti, 1.0, 0.0).astype(BF16)
    x = [l_ref[0, k * tc:(k + 1) * tc, :] + bf_ref[...] for k in blks]
    log_f = [jnp.minimum(x[k], 0.0) - jnp.log(1.0 + jnp.exp(-jnp.abs(x[k]))) for k in blks]
    terms = [_split_bf16(log_f[k], 3) for k in blks]
    local = [sum(_dot(tril, part) for part in terms[k]) for k in blks]
    c = []
    total = jnp.zeros((1, LANES), F32)
    for k in blks:
        c.append(local[k] + total)
        total = total + local[k][tc - 1:tc, :]
    lin = lax.broadcasted_iota(jnp.int32, (LANES, LANES), 0)
    lout = lax.broadcasted_iota(jnp.int32, (LANES, LANES), 1)
    sel = [jnp.where((lin < n_heads) & (lout == _c_lane(lin // 2, lin % 2) + i), 1.0, 0.0).astype(BF16)
           for i in range(C_TERMS)]
    c_terms = [_split_bf16(-LOG2_E * c[k], C_TERMS) for k in blks]
    for k in blks:
        o_ref[0, k * tc:(k + 1) * tc, :] = sum(_dot(c_terms[k][i], sel[i]) for i in range(C_TERMS)).astype(o_ref.dtype)


def _fcum(z3, lora_blk, bf_pad, n_heads, tc):
    bsz, seq, _ = z3.shape
    return pl.pallas_call(
        functools.partial(_fcum_kernel, n_heads=n_heads, tc=tc),
        grid=(bsz,),
        in_specs=[pl.BlockSpec((1, seq, LANES), lambda b: (b, 0, 2 * lora_blk + 1)),
                  pl.BlockSpec((1, LANES), lambda b: (0, 0))],
        out_specs=pl.BlockSpec((1, seq, LANES), lambda b: (b, 0, 0)),
        out_shape=jax.ShapeDtypeStruct((bsz, seq, LANES), BF16),
        compiler_params=pltpu.CompilerParams(dimension_semantics=("arbitrary",),
                                             vmem_limit_bytes=VMEM_LIMIT),
        name="fox_cumsum",
    )(z3, bf_pad)


def _fox_kernel(q_ref, k_ref, v_ref, g_ref, c_ref, o_ref, vt_ref, qt_ref, m_ref, acc_ref, sa_ref, sb_ref, *, tq, tqs,
                t_chunk):
    pair = pl.program_id(1)
    seq = k_ref.shape[1]
    nq = seq // tq
    lane = lax.broadcasted_iota(jnp.int32, (1, LANES), 1)
    row = lax.broadcasted_iota(jnp.int32, (LANES, 1), 0)

    one_row = jnp.where(lax.broadcasted_iota(jnp.int32, (VT_ROWS - HEAD_DIM, t_chunk), 0) == 0, 1.0, 0.0)
    for t0 in range(0, seq, t_chunk):
        v_t = v_ref[0, t0:t0 + t_chunk, :].astype(BF16).T
        q_t = (q_ref[0, t0:t0 + t_chunk, :] * (Q_SCALE * LOG2_E)).astype(BF16).T
        for h in range(2):
            vt_ref[h, 0:HEAD_DIM, t0:t0 + t_chunk] = v_t[h * HEAD_DIM:(h + 1) * HEAD_DIM]
            vt_ref[h, HEAD_DIM:VT_ROWS, t0:t0 + t_chunk] = one_row.astype(BF16)
            r0 = _c_lane(pair, h)
            ones = jnp.where((row >= r0) & (row < r0 + C_TERMS), 1.0, 0.0)
            qt_ref[h, :, t0:t0 + t_chunk] = jnp.where((row // HEAD_DIM) == h, q_t, ones.astype(BF16))
    acc_ref[...] = jnp.zeros_like(acc_ref)
    m_ref[...] = jnp.full_like(m_ref, NEG_BIG)

    chains = [(h, slice(q0, q0 + tqs)) for h in range(2) for q0 in range(0, tq, tqs)]

    def keys_seen(qs, diagonal):
        return qs.stop if diagonal else tq

    def scores(qi, j, dst_ref, diagonal):
        ks = pl.multiple_of(j * tq, tq)
        kb = k_ref[0, pl.ds(ks, tq), :].astype(BF16)
        cb = c_ref[0, pl.ds(ks, tq), :]
        ka = [jnp.where((lane // HEAD_DIM) == h, kb, cb) for h in range(2)]
        for i, (h, qs) in enumerate(chains):
            nk = keys_seen(qs, diagonal)
            dst_ref[i, 0:nk, :] = _dot(ka[h][0:nk], qt_ref[h, :, pl.ds(pl.multiple_of(qi * tq + qs.start, tqs), tqs)])

    def update(src_ref, qi, j, diagonal):
        ks = pl.multiple_of(j * tq, tq)
        nk = [keys_seen(qs, diagonal) for h, qs in chains]
        s = [src_ref[i, 0:nk[i], :] for i in range(len(chains))]
        if diagonal:
            s = [jnp.where(lax.broadcasted_iota(jnp.int32, (nk[i], tqs), 0)
                           <= lax.broadcasted_iota(jnp.int32, (nk[i], tqs), 1) + qs.start, s[i], NEG_BIG)
                 for i, (h, qs) in enumerate(chains)]
        m_old = [jnp.where(j == 0, NEG_BIG, m_ref[h, :, qs]) for h, qs in chains]
        m_new = [jnp.maximum(m_old[i], jnp.max(s[i], axis=0, keepdims=True)) for i in range(len(chains))]
        alpha = [jnp.exp2(m_old[i] - m_new[i]) for i in range(len(chains))]
        pexp = [jnp.exp2(s[i] - m_new[i]).astype(BF16) for i in range(len(chains))]
        pv = [_dot(vt_ref[h, :, pl.ds(ks, nk[i])], pexp[i]) for i, (h, qs) in enumerate(chains)]
        for i, (h, qs) in enumerate(chains):
            acc_ref[h, :, qs] = alpha[i] * acc_ref[h, :, qs] + pv[i]
            m_ref[h, :, qs] = m_new[i]
        if diagonal:
            o_t = jnp.concatenate([acc_ref[h, 0:HEAD_DIM, :] / acc_ref[h, HEAD_DIM:HEAD_DIM + 1, :]
                                   for h in range(2)], axis=0)
            q0 = pl.multiple_of(qi * tq, tq)
            gate = g_ref[0, pl.ds(q0, tq), :]
            o_ref[0, pl.ds(q0, tq), :] = (o_t.T * (gate * _sigmoid(gate))).astype(o_ref.dtype)

    def following(qi, j):
        wrap = j == qi
        return jnp.where(wrap, qi + 1, qi), jnp.where(wrap, 0, j + 1)

    def half_step(cur_ref, nxt_ref, qi, j):
        qn, jn = following(qi, j)

        @pl.when(j + 1 < qi)
        def _():
            scores(qn, jn, nxt_ref, False)
            update(cur_ref, qi, j, False)

        @pl.when(j + 1 == qi)
        def _():
            scores(qn, jn, nxt_ref, True)
            update(cur_ref, qi, j, False)

        @pl.when((j == qi) & (qi + 1 < nq))
        def _():
            scores(qn, jn, nxt_ref, False)
            update(cur_ref, qi, j, True)

        @pl.when((j == qi) & (qi + 1 == nq))
        def _():
            update(cur_ref, qi, j, True)
        return qn, jn

    scores(0, 0, sa_ref, True)

    def body(n, carry):
        qi, j = carry
        qi, j = half_step(sa_ref, sb_ref, qi, j)
        return half_step(sb_ref, sa_ref, qi, j)

    lax.fori_loop(0, nq * (nq + 1) // 4, body, (jnp.int32(0), jnp.int32(0)))


def _fox(z3, c_sp, col0, width, tq):
    bsz, seq, _ = z3.shape
    n_pairs = width // LANES
    nq = seq // tq
    assert (nq * (nq + 1) // 2) % 2 == 0
    blk = lambda j: col0 // LANES + j * n_pairs
    tqs = min(tq, 2 * LANES)
    s_buf = pltpu.VMEM((2 * tq // tqs, tq, tqs), F32)
    zspec = lambda j: pl.BlockSpec((1, seq, LANES), lambda b, p, j=j: (b, 0, blk(j) + p))
    return pl.pallas_call(
        functools.partial(_fox_kernel, tq=tq, tqs=tqs, t_chunk=min(seq, 512)),
        grid=(bsz, n_pairs),
        in_specs=[zspec(0), zspec(1), zspec(2), zspec(3),
                  pl.BlockSpec((1, seq, LANES), lambda b, p: (b, 0, 0))],
        out_specs=pl.BlockSpec((1, seq, LANES), lambda b, p: (b, 0, p)),
        out_shape=jax.ShapeDtypeStruct((bsz, seq, width), BF16),
        scratch_shapes=[pltpu.VMEM((2, VT_ROWS, seq), BF16), pltpu.VMEM((2, LANES, seq), BF16),
                        pltpu.VMEM((2, 1, tq), F32), pltpu.VMEM((2, VT_ROWS, tq), F32), s_buf, s_buf],
        compiler_params=pltpu.CompilerParams(dimension_semantics=("arbitrary",) * 2,
                                             vmem_limit_bytes=VMEM_LIMIT),
        name="fox_attention",
    )(z3, z3, z3, z3, c_sp)


def _tail_kernel(yr_ref, yf_ref, gr_ref, gf_ref, x_ref, p_ref, ng_ref, fg_ref, wr_ref, wf_ref, wo_ref, wp_ref, wg_ref,
                 o_ref):
    u_rw = _dot(yr_ref[...], wr_ref[...])
    u_fox = _dot(yf_ref[...], wf_ref[...])
    merged = _sigmoid(gr_ref[...]) * u_rw + _sigmoid(gf_ref[...]) * u_fox
    x1 = x_ref[...] + _dot(merged.astype(BF16), wo_ref[...])
    hn = x1 * lax.rsqrt(jnp.mean(x1 * x1, axis=-1, keepdims=True) + NORM_EPS) * ng_ref[...]
    gate = _sigmoid(_dot(hn.astype(BF16), wg_ref[...]))
    ple = _dot(p_ref[...].astype(BF16), wp_ref[...])
    x2 = x1 + ple * gate
    o_ref[...] = x2 * lax.rsqrt(jnp.mean(x2 * x2, axis=-1, keepdims=True) + NORM_EPS) * fg_ref[...]


def _tail(y_rw, y_fox, z, gate_col0, x, p, ple_norm_g, final_norm_g, w_up_rw, w_up_fox, w_out, ple_proj, ple_gate_w,
          tm):
    m, d = x.shape
    cw = y_rw.shape[1]
    pd = p.shape[1]
    gblk = gate_col0 // d
    rows = lambda n, j=0: pl.BlockSpec((tm, n), lambda i, j=j: (i, j))
    const = lambda shape: pl.BlockSpec(shape, lambda i: (0, 0), pipeline_mode=pl.Buffered(1))
    return pl.pallas_call(
        _tail_kernel,
        grid=(m // tm,),
        in_specs=[rows(cw), rows(cw), rows(d, gblk), rows(d, gblk + 1), rows(d), rows(pd),
                  const((1, d)), const((1, d)), const((cw, d)), const((cw, d)), const((d, d)), const((pd, d)),
                  const((d, d))],
        out_specs=rows(d),
        out_shape=jax.ShapeDtypeStruct((m, d), F32),
        compiler_params=pltpu.CompilerParams(dimension_semantics=("arbitrary",),
                                             vmem_limit_bytes=VMEM_LIMIT),
        name="merge_ple_final_norm",
    )(y_rw, y_fox, z, z, x, p, ple_norm_g.reshape(1, d), final_norm_g.reshape(1, d), w_up_rw, w_up_fox, w_out,
      ple_proj, ple_gate_w)


def _regroup_kernel(main_ref, lora_ref, fl_ref, o_ref, *, n_heads):
    last = pl.num_programs(0) - 1

    @pl.when(pl.program_id(0) < last)
    def _():
        o_ref[...] = main_ref[...].T.astype(BF16)

    @pl.when(pl.program_id(0) == last)
    def _():
        row = lax.broadcasted_iota(jnp.int32, (LANES, 1), 0)
        o_ref[:, 0:LANES] = lora_ref[...].T.astype(BF16)
        o_ref[:, LANES:2 * LANES] = jnp.where(row < n_heads, fl_ref[...], 0.0).T.astype(BF16)


def _regroup_weights(w_t, cw, lw, n_heads, tr):
    n_in, d = w_t.shape
    fl0 = 8 * cw + 2 * lw
    small0 = 8 * cw + 2 * d
    n_pad = small0 + 2 * LANES
    assert tr == 2 * LANES and (4 * cw) % tr == 0 and (2 * d) % tr == 0 and n_in == fl0 + n_heads + 2 * d
    assert (2 * lw) % SUBLANES == 0 and n_heads % SUBLANES == 0

    def main_rows(i):
        r = i * tr
        src = jnp.where(r < 4 * cw, r, jnp.where(r < 8 * cw, r + 2 * lw,
                                                 jnp.where(r < small0, r + 2 * lw + n_heads, 0)))
        return pl.multiple_of(src, SUBLANES), 0

    rows = lambda n: (pl.Element(n), pl.Element(d))
    return pl.pallas_call(
        functools.partial(_regroup_kernel, n_heads=n_heads),
        grid=(n_pad // tr,),
        in_specs=[pl.BlockSpec(rows(tr), main_rows),
                  pl.BlockSpec(rows(LANES), lambda i: (4 * cw, 0)),
                  pl.BlockSpec(rows(LANES), lambda i: (fl0, 0))],
        out_specs=pl.BlockSpec((d, tr), lambda i: (0, i)),
        out_shape=jax.ShapeDtypeStruct((d, n_pad), BF16),
        compiler_params=pltpu.CompilerParams(dimension_semantics=("arbitrary",),
                                             vmem_limit_bytes=VMEM_LIMIT),
        name="regroup_w_in",
    )(w_t, w_t, w_t)


def _cast_kernel(*refs):
    n = len(refs) // 2
    for i_ref, o_ref in zip(refs[:n], refs[n:]):
        o_ref[...] = i_ref[...].astype(o_ref.dtype)


def _cast_bf16(weights, steps):
    specs = [pl.BlockSpec((w.shape[0] // steps, w.shape[1]), lambda i: (i, 0)) for w in weights]
    assert all(w.shape[0] % (steps * 2 * SUBLANES) == 0 for w in weights)
    return pl.pallas_call(
        _cast_kernel,
        grid=(steps,),
        in_specs=specs,
        out_specs=specs,
        out_shape=[jax.ShapeDtypeStruct(w.shape, BF16) for w in weights],
        compiler_params=pltpu.CompilerParams(dimension_semantics=("arbitrary",),
                                             vmem_limit_bytes=VMEM_LIMIT),
        name="cast_tail_weights",
    )(*weights)


def _layer(x2d, p2d, seq, norm_g, w_in, rw_shift_mu, rw_w0, rw_w_lora_up, rw_a0, rw_a_lora_up, rw_k_k, rw_k_a,
           rw_r_k, rw_ln_g, rw_ln_b, fox_b_f, w_up_rwkv, w_up_fox, w_out, ple_proj, ple_gate_w, ple_norm_g,
           final_norm_g, *, tiles):
    m, d = x2d.shape
    bsz = m // seq
    cw = w_up_rwkv.shape[0]
    n_heads = fox_b_f.shape[0]
    lw = rw_w_lora_up.shape[0]
    assert 2 * lw == LANES and cw % LANES == 0 and C_TERMS * (cw // LANES) <= HEAD_DIM

    fox0, gate0, small0 = 4 * cw, 8 * cw, 8 * cw + 2 * d
    n_pad = small0 + 2 * LANES
    w_cat = _regroup_weights(w_in.T, cw, lw, n_heads, tiles["regroup_tr"])
    mu_cat = jnp.concatenate([
        rw_shift_mu[0:4 * cw], jnp.zeros((small0 - 4 * cw,), F32), rw_shift_mu[4 * cw:],
        jnp.zeros((LANES,), F32)]).reshape(1, n_pad)

    z = _inproj(x2d, norm_g, w_cat, mu_cat, seq, tiles["in_tm"], tiles["in_tn"])
    z3 = z.reshape(bsz, seq, n_pad)
    lora_blk = small0 // (2 * LANES)

    row = lambda a: a.reshape(1, cw)
    wup = jnp.pad(rw_w_lora_up, ((0, lw), (0, 0))).astype(BF16)
    aup = jnp.pad(rw_a_lora_up, ((lw, 0), (0, 0))).astype(BF16)
    y_rw = _rwkv(z3, lora_blk, wup, aup, row(rw_w0), row(rw_a0), row(rw_k_k), row(rw_k_a), row(rw_r_k),
                 row(rw_ln_g), row(rw_ln_b), cw)

    bf_pad = jnp.pad(fox_b_f, (0, LANES - n_heads)).reshape(1, LANES)
    c_sp = _fcum(z3, lora_blk, bf_pad, n_heads, tiles["fc_tc"])
    y_fox = _fox(z3, c_sp, fox0, cw, tiles["fox_tq"])

    tail_w = _cast_bf16([w_up_rwkv, w_up_fox, w_out, ple_proj, ple_gate_w], tiles["cast_steps"])
    return _tail(y_rw.reshape(m, cw), y_fox.reshape(m, cw), z, gate0, x2d, p2d, ple_norm_g, final_norm_g,
                 *tail_w, tiles["tail_tm"])


_TILES = dict(regroup_tr=256, in_tm=1024, in_tn=1792, fc_tc=512, fox_tq=1024, tail_tm=256, cast_steps=8)


def kernel(x, p, norm_g, w_in, rw_shift_mu, rw_w0, rw_w_lora_up, rw_a0, rw_a_lora_up, rw_k_k, rw_k_a, rw_r_k,
           rw_ln_g, rw_ln_b, fox_b_f, w_up_rwkv, w_up_fox, w_out, ple_proj, ple_gate_w, ple_norm_g,
           final_norm_g):
    bsz, seq, d = x.shape
    depth = p.shape[0]
    assert depth == 1, "the final norm is fused into the (single) layer"
    out = _layer(x.reshape(bsz * seq, d), p[0].reshape(bsz * seq, -1), seq, norm_g[0], w_in[0], rw_shift_mu[0],
                 rw_w0[0], rw_w_lora_up[0], rw_a0[0], rw_a_lora_up[0], rw_k_k[0], rw_k_a[0],
                 rw_r_k[0].reshape(-1), rw_ln_g[0], rw_ln_b[0], fox_b_f[0], w_up_rwkv[0], w_up_fox[0], w_out[0],
                 ple_proj[0], ple_gate_w[0], ple_norm_g[0], final_norm_g, tiles=_TILES)
    return out.reshape(bsz, seq, d)
```

```python
import functools
import math

import jax
import jax.numpy as jnp
from jax import lax
from jax.experimental import pallas as pl
from jax.experimental.pallas import tpu as pltpu

F32 = jnp.float32
BF16 = jnp.bfloat16

HEAD_DIM = 64
LANES = 128
SUBLANES = 8
NORM_EPS = 1e-6
GN_EPS = 64e-5
Q_SCALE = HEAD_DIM ** -0.5
EXP_NEG_HALF = math.exp(-0.5)
LOG2_E = math.log2(math.e)
NEG_BIG = -1e30
VMEM_LIMIT = 56 * 1024 * 1024

SPLIT_TERMS = 2
NORM_TERMS = 1
RW_CHUNK = 64
RW_CHUNKS_PER_STEP = 4
C_TERMS = 3
VT_ROWS = HEAD_DIM + 2 * SUBLANES


def _dot(a, b):
    return jnp.dot(a, b, preferred_element_type=F32)


def _dot_nt(a, b):
    return lax.dot_general(a, b, (((1,), (1,)), ((), ())), preferred_element_type=F32)


def _dot_tn(a, b):
    return lax.dot_general(a, b, (((0,), (0,)), ((), ())), preferred_element_type=F32)


def _split_bf16(x, n):
    parts = []
    rem = x
    for i in range(n):
        p = rem.astype(BF16)
        parts.append(p)
        if i + 1 < n:
            rem = rem - p.astype(F32)
    return parts


def _sigmoid(x):
    return 1.0 / (1.0 + jnp.exp(-x))


def _inproj_kernel(x_ref, g_ref, w_ref, mu_ref, o_ref, carry_ref, *, tiles_per_seq):
    x = x_ref[...]
    rstd = lax.rsqrt(jnp.mean(x * x, axis=-1, keepdims=True) + NORM_EPS)
    z = _dot((x * g_ref[...]).astype(BF16), w_ref[...]) * rstd
    tm = z.shape[0]
    first = (pl.program_id(1) % tiles_per_seq) == 0
    prev_last = jnp.where(first, 0.0, carry_ref[SUBLANES - 1:SUBLANES, :])
    row = lax.broadcasted_iota(jnp.int32, z.shape, 0)
    z_prev = jnp.where(row == 0, prev_last, pltpu.roll(z, 1, 0))
    o_ref[...] = z + (z_prev - z) * mu_ref[...]
    carry_ref[...] = z[tm - SUBLANES:tm, :]


def _inproj(x, g, w, mu, seq, tm, tn):
    m, d = x.shape
    n = w.shape[1]
    return pl.pallas_call(
        functools.partial(_inproj_kernel, tiles_per_seq=seq // tm),
        grid=(n // tn, m // tm),
        in_specs=[pl.BlockSpec((tm, d), lambda j, i: (i, 0)),
                  pl.BlockSpec((1, d), lambda j, i: (0, 0)),
                  pl.BlockSpec((d, tn), lambda j, i: (0, j)),
                  pl.BlockSpec((1, tn), lambda j, i: (0, j))],
        out_specs=pl.BlockSpec((tm, tn), lambda j, i: (i, j)),
        out_shape=jax.ShapeDtypeStruct((m, n), F32),
        scratch_shapes=[pltpu.VMEM((SUBLANES, tn), F32)],
        compiler_params=pltpu.CompilerParams(dimension_semantics=("arbitrary", "arbitrary"),
                                             vmem_limit_bytes=VMEM_LIMIT),
        name="inproj",
    )(x, g.reshape(1, d), w, mu)


def _rwkv_chunk(r, k, v, gate, lora_b, lora_t, wup, aup, w0, a0, kkp, kap, rkp, lng, lnb, s_old, *, c):
    rows, width = r.shape
    n_pairs = width // LANES
    pairs = range(n_pairs)
    subs = range(rows // c)
    blocks = [(i, p) for i in subs for p in pairs]
    cols = lambda x: [x[i * c:(i + 1) * c, p * LANES:(p + 1) * LANES] for i, p in blocks]
    lane = lax.broadcasted_iota(jnp.int32, (1, LANES), 1)
    m0 = lane < HEAD_DIM

    ri = lax.broadcasted_iota(jnp.int32, (LANES, LANES), 0)
    ci = lax.broadcasted_iota(jnp.int32, (LANES, LANES), 1)
    same_head = (ri // HEAD_DIM) == (ci // HEAD_DIM)
    ones_bd = jnp.where(same_head, 1.0, 0.0).astype(BF16)

    def segsum(x, terms):
        xr = jnp.concatenate(cols(x), axis=0)
        sr = sum(_dot(part, ones_bd) for part in _split_bf16(xr, terms))
        return jnp.concatenate([jnp.concatenate([sr[(i * n_pairs + p) * c:(i * n_pairs + p + 1) * c] for p in pairs],
                                                axis=1) for i in subs], axis=0)

    w_raw = w0 + _dot(lora_t, wup)
    logw = -EXP_NEG_HALF * _sigmoid(w_raw)
    a = _sigmoid(a0 + _dot(lora_b, aup))
    kkr = k * kkp
    kk = kkr / jnp.maximum(jnp.sqrt(segsum(kkr * kkr, NORM_TERMS)), 1e-12)
    b = kk * a
    k2 = k * (1.0 + (a - 1.0) * kap)

    ti = lax.broadcasted_iota(jnp.int32, (rows, rows), 0)
    si = lax.broadcasted_iota(jnp.int32, (rows, rows), 1)
    tri = jnp.where((si <= ti) & (si // c == ti // c), 1.0, 0.0).astype(BF16)
    g = sum(_dot(tri, part) for part in _split_bf16(logw, SPLIT_TERMS))
    g_ends = [g[(i + 1) * c - 1:(i + 1) * c, :] for i in subs]
    g_last = jnp.concatenate([jnp.broadcast_to(ge, (c, width)) for ge in g_ends], axis=0)
    egd = jnp.exp(g_last - g)
    eng = jnp.exp(-g)
    rg = cols(r * jnp.exp(g))
    kkg = cols(kk * jnp.exp(g - logw))
    kn = cols(k2 * eng)
    bn = cols(b * eng)
    kd = cols(k2 * egd)
    bdec = cols(b * egd)
    vs = cols(v)
    decay_c = [jnp.exp(g_ends[i][:, p * LANES:(p + 1) * LANES]) for i, p in blocks]
    nb = range(len(blocks))

    def stack(x):
        return jnp.concatenate([jnp.where(m0, x, 0.0), jnp.where(m0, 0.0, x)], axis=0).astype(BF16)

    t_in = lax.broadcasted_iota(jnp.int32, (c, LANES), 0)
    s_in = lax.broadcasted_iota(jnp.int32, (c, LANES), 1) % HEAD_DIM
    strict = s_in < t_in
    incl = s_in <= t_in
    eye = jnp.where(s_in == t_in, 1.0, 0.0)

    lhs_kr = [jnp.concatenate([kkg[b], rg[b]], axis=0).astype(BF16) for b in nb]
    rhs_bk = [jnp.concatenate([stack(bn[b]), stack(kn[b])], axis=0) for b in nb]
    aa = [_dot_nt(lhs_kr[b], rhs_bk[b]) for b in nb]
    a_kb = [jnp.where(strict, aa[b][0:c, 0:LANES], 0.0) for b in nb]
    a_kk = [jnp.where(strict, aa[b][0:c, LANES:2 * LANES], 0.0).astype(BF16) for b in nb]
    a_rb = [jnp.where(incl, aa[b][c:2 * c, 0:LANES], 0.0) for b in nb]
    a_rk = [jnp.where(incl, aa[b][c:2 * c, LANES:2 * LANES], 0.0) for b in nb]
    a_r = [jnp.concatenate([a_rk[b], a_rb[b]], axis=1).astype(BF16) for b in nb]
    v_st = [stack(vs[b]) for b in nb]
    kb = [jnp.concatenate([kd[b], bdec[b]], axis=0).astype(BF16) for b in nb]
    av = [_dot(a_kk[b], v_st[b]) for b in nb]

    tinv = [eye - a_kb[b] for b in nb]
    lpow = [_dot(a_kb[b].astype(BF16), stack(a_kb[b])) for b in nb]
    n_sq = int(math.log2(c)) - 1
    for i in range(n_sq):
        last = i + 1 == n_sq
        w_pow = [stack(lpow[b]) for b in nb]
        lhs = [tinv[b].astype(BF16) if last else jnp.concatenate([tinv[b], lpow[b]], axis=0).astype(BF16)
               for b in nb]
        prod = [_dot(lhs[b], w_pow[b]) for b in nb]
        tinv = [tinv[b] + prod[b][0:c] for b in nb]
        if not last:
            lpow = [prod[b][c:2 * c] for b in nb]
    tinv = [t.astype(BF16) for t in tinv]

    s_cur = list(s_old)
    ys = []
    for i in subs:
        bi = [i * n_pairs + p for p in pairs]
        s_b = [s_cur[p].astype(BF16) for p in pairs]
        sh = [_dot_nt(lhs_kr[bi[p]], s_b[p]) for p in pairs]
        x = [sh[p][0:c] + av[bi[p]] for p in pairs]
        u = [_dot(tinv[bi[p]], stack(x[p])) for p in pairs]
        vu_st = [jnp.concatenate([v_st[bi[p]], stack(-u[p])], axis=0) for p in pairs]
        ys.append(jnp.concatenate([sh[p][c:2 * c] + _dot(a_r[bi[p]], vu_st[p]) for p in pairs], axis=1))
        vu = [jnp.concatenate([vs[bi[p]], -u[p]], axis=0).astype(BF16) for p in pairs]
        s_cur = [s_cur[p] * decay_c[bi[p]] + jnp.where(same_head, _dot_tn(vu[p], kb[bi[p]]), 0.0) for p in pairs]
    y = jnp.concatenate(ys, axis=0)
    s_new = s_cur

    inv_n = 1.0 / HEAD_DIM
    mean = segsum(y, NORM_TERMS) * inv_n
    dlt = y - mean
    var = segsum(dlt * dlt, NORM_TERMS) * inv_n
    yn = dlt * lax.rsqrt(var + GN_EPS) * lng + lnb
    bonus = segsum(r * k2 * rkp, SPLIT_TERMS) * v
    out = (yn + bonus) * (gate * _sigmoid(gate))
    return out, s_new


def _rwkv_kernel(r_ref, k_ref, v_ref, g_ref, lora_ref, wup_ref, aup_ref, w0_ref, a0_ref, kk_ref, ka_ref,
                 rk_ref, lng_ref, lnb_ref, o_ref, s_ref):
    @pl.when(pl.program_id(1) == 0)
    def _():
        s_ref[...] = jnp.zeros_like(s_ref)

    lora = lora_ref[0][:, 0:LANES]
    out, s_new = _rwkv_chunk(
        r_ref[0], k_ref[0], v_ref[0], g_ref[0], lora.astype(BF16), jnp.tanh(lora).astype(BF16),
        wup_ref[...], aup_ref[...], w0_ref[...], a0_ref[...], kk_ref[...], ka_ref[...], rk_ref[...],
        lng_ref[...], lnb_ref[...], [s_ref[p] for p in range(s_ref.shape[0])], c=RW_CHUNK)
    for p in range(s_ref.shape[0]):
        s_ref[p] = s_new[p]
    o_ref[0] = out.astype(o_ref.dtype)


def _rwkv(z3, lora_blk, wup, aup, w0, a0, k_k, k_a, r_k, ln_g, ln_b, width):
    bsz, seq, _ = z3.shape
    c = min(seq, RW_CHUNK * RW_CHUNKS_PER_STEP)
    assert width % LANES == 0
    zspec = lambda j: pl.BlockSpec((1, c, width), lambda b, t, j=j: (b, t, j))
    pspec = lambda rows: pl.BlockSpec((rows, width), lambda b, t: (0, 0))
    return pl.pallas_call(
        _rwkv_kernel,
        grid=(bsz, seq // c),
        in_specs=[zspec(0), zspec(1), zspec(2), zspec(3),
                  pl.BlockSpec((1, c, 2 * LANES), lambda b, t: (b, t, lora_blk)),
                  pspec(LANES), pspec(LANES)] + [pspec(1)] * 7,
        out_specs=pl.BlockSpec((1, c, width), lambda b, t: (b, t, 0)),
        out_shape=jax.ShapeDtypeStruct((bsz, seq, width), BF16),
        scratch_shapes=[pltpu.VMEM((width // LANES, LANES, LANES), F32)],
        compiler_params=pltpu.CompilerParams(dimension_semantics=("arbitrary", "arbitrary"),
                                             vmem_limit_bytes=VMEM_LIMIT),
        name="rwkv7_scan",
    )(z3, z3, z3, z3, z3, wup, aup, w0, a0, k_k, k_a, r_k, ln_g, ln_b)


def _c_lane(pair, h):
    return (1 - h) * HEAD_DIM + C_TERMS * pair


def _fcum_kernel(l_ref, bf_ref, o_ref, *, n_heads, tc):
    seq = l_ref.shape[1]
    blks = range(seq // tc)
    ti = lax.broadcasted_iota(jnp.int32, (tc, tc), 0)
    si = lax.broadcasted_iota(jnp.int32, (tc, tc), 1)
    tril = jnp.where(si <= ti, 1.0, 0.0).astype(BF16)
    x = [l_ref[0, k * tc:(k + 1) * tc, :] + bf_ref[...] for k in blks]
    log_f = [jnp.minimum(x[k], 0.0) - jnp.log(1.0 + jnp.exp(-jnp.abs(x[k]))) for k in blks]
    terms = [_split_bf16(log_f[k], 3) for k in blks]
    local = [sum(_dot(tril, part) for part in terms[k]) for k in blks]
    c = []
    total = jnp.zeros((1, LANES), F32)
    for k in blks:
        c.append(local[k] + total)
        total = total + local[k][tc - 1:tc, :]
    lin = lax.broadcasted_iota(jnp.int32, (LANES, LANES), 0)
    lout = lax.broadcasted_iota(jnp.int32, (LANES, LANES), 1)
    sel = [jnp.where((lin < n_heads) & (lout == _c_lane(lin // 2, lin % 2) + i), 1.0, 0.0).astype(BF16)
           for i in range(C_TERMS)]
    c_terms = [_split_bf16(-LOG2_E * c[k], C_TERMS) for k in blks]
    for k in blks:
        o_ref[0, k * tc:(k + 1) * tc, :] = sum(_dot(c_terms[k][i], sel[i]) for i in range(C_TERMS)).astype(o_ref.dtype)


def _fcum(z3, lora_blk, bf_pad, n_heads, tc):
    bsz, seq, _ = z3.shape
    return pl.pallas_call(
        functools.partial(_fcum_kernel, n_heads=n_heads, tc=tc),
        grid=(bsz,),
        in_specs=[pl.BlockSpec((1, seq, LANES), lambda b: (b, 0, 2 * lora_blk + 1)),
                  pl.BlockSpec((1, LANES), lambda b: (0, 0))],
        out_specs=pl.BlockSpec((1, seq, LANES), lambda b: (b, 0, 0)),
        out_shape=jax.ShapeDtypeStruct((bsz, seq, LANES), BF16),
        compiler_params=pltpu.CompilerParams(dimension_semantics=("arbitrary",),
                                             vmem_limit_bytes=VMEM_LIMIT),
        name="fox_cumsum",
    )(z3, bf_pad)


def _fox_kernel(q_ref, k_ref, v_ref, g_ref, c_ref, o_ref, vt_ref, qt_ref, m_ref, acc_ref, sa_ref, sb_ref, *, tq, tqs,
                t_chunk):
    pair = pl.program_id(1)
    seq = k_ref.shape[1]
    nq = seq // tq
    lane = lax.broadcasted_iota(jnp.int32, (1, LANES), 1)
    row = lax.broadcasted_iota(jnp.int32, (LANES, 1), 0)

    one_row = jnp.where(lax.broadcasted_iota(jnp.int32, (VT_ROWS - HEAD_DIM, t_chunk), 0) == 0, 1.0, 0.0)
    for t0 in range(0, seq, t_chunk):
        v_t = v_ref[0, t0:t0 + t_chunk, :].astype(BF16).T
        q_t = (q_ref[0, t0:t0 + t_chunk, :] * (Q_SCALE * LOG2_E)).astype(BF16).T
        for h in range(2):
            vt_ref[h, 0:HEAD_DIM, t0:t0 + t_chunk] = v_t[h * HEAD_DIM:(h + 1) * HEAD_DIM]
            vt_ref[h, HEAD_DIM:VT_ROWS, t0:t0 + t_chunk] = one_row.astype(BF16)
            r0 = _c_lane(pair, h)
            ones = jnp.where((row >= r0) & (row < r0 + C_TERMS), 1.0, 0.0)
            qt_ref[h, :, t0:t0 + t_chunk] = jnp.where((row // HEAD_DIM) == h, q_t, ones.astype(BF16))
    acc_ref[...] = jnp.zeros_like(acc_ref)
    m_ref[...] = jnp.full_like(m_ref, NEG_BIG)

    chains = [(h, slice(q0, q0 + tqs)) for h in range(2) for q0 in range(0, tq, tqs)]

    def keys_seen(qs, diagonal):
        return qs.stop if diagonal else tq

    def scores(qi, j, dst_ref, diagonal):
        ks = pl.multiple_of(j * tq, tq)
        kb = k_ref[0, pl.ds(ks, tq), :].astype(BF16)
        cb = c_ref[0, pl.ds(ks, tq), :]
        ka = [jnp.where((lane // HEAD_DIM) == h, kb, cb) for h in range(2)]
        for i, (h, qs) in enumerate(chains):
            nk = keys_seen(qs, diagonal)
            dst_ref[i, 0:nk, :] = _dot(ka[h][0:nk], qt_ref[h, :, pl.ds(pl.multiple_of(qi * tq + qs.start, tqs), tqs)])

    def update(src_ref, qi, j, diagonal):
        ks = pl.multiple_of(j * tq, tq)
        nk = [keys_seen(qs, diagonal) for h, qs in chains]
        s = [src_ref[i, 0:nk[i], :] for i in range(len(chains))]
        if diagonal:
            s = [jnp.where(lax.broadcasted_iota(jnp.int32, (nk[i], tqs), 0)
                           <= lax.broadcasted_iota(jnp.int32, (nk[i], tqs), 1) + qs.start, s[i], NEG_BIG)
                 for i, (h, qs) in enumerate(chains)]
        m_old = [jnp.where(j == 0, NEG_BIG, m_ref[h, :, qs]) for h, qs in chains]
        m_new = [jnp.maximum(m_old[i], jnp.max(s[i], axis=0, keepdims=True)) for i in range(len(chains))]
        alpha = [jnp.exp2(m_old[i] - m_new[i]) for i in range(len(chains))]
        pexp = [jnp.exp2(s[i] - m_new[i]).astype(BF16) for i in range(len(chains))]
        pv = [_dot(vt_ref[h, :, pl.ds(ks, nk[i])], pexp[i]) for i, (h, qs) in enumerate(chains)]
        for i, (h, qs) in enumerate(chains):
            acc_ref[h, :, qs] = alpha[i] * acc_ref[h, :, qs] + pv[i]
            m_ref[h, :, qs] = m_new[i]
        if diagonal:
            o_t = jnp.concatenate([acc_ref[h, 0:HEAD_DIM, :] / acc_ref[h, HEAD_DIM:HEAD_DIM + 1, :]
                                   for h in range(2)], axis=0)
            q0 = pl.multiple_of(qi * tq, tq)
            gate = g_ref[0, pl.ds(q0, tq), :]
            o_ref[0, pl.ds(q0, tq), :] = (o_t.T * (gate * _sigmoid(gate))).astype(o_ref.dtype)

    def following(qi, j):
        wrap = j == qi
        return jnp.where(wrap, qi + 1, qi), jnp.where(wrap, 0, j + 1)

    def half_step(cur_ref, nxt_ref, qi, j):
        qn, jn = following(qi, j)

        @pl.when(j + 1 < qi)
        def _():
            scores(qn, jn, nxt_ref, False)
            update(cur_ref, qi, j, False)

        @pl.when(j + 1 == qi)
        def _():
            scores(qn, jn, nxt_ref, True)
            update(cur_ref, qi, j, False)

        @pl.when((j == qi) & (qi + 1 < nq))
        def _():
            scores(qn, jn, nxt_ref, False)
            update(cur_ref, qi, j, True)

        @pl.when((j == qi) & (qi + 1 == nq))
        def _():
            update(cur_ref, qi, j, True)
        return qn, jn

    scores(0, 0, sa_ref, True)

    def body(n, carry):
        qi, j = carry
        qi, j = half_step(sa_ref, sb_ref, qi, j)
        return half_step(sb_ref, sa_ref, qi, j)

    lax.fori_loop(0, nq * (nq + 1) // 4, body, (jnp.int32(0), jnp.int32(0)))


def _fox(z3, c_sp, col0, width, tq):
    bsz, seq, _ = z3.shape
    n_pairs = width // LANES
    nq = seq // tq
    assert (nq * (nq + 1) // 2) % 2 == 0
    blk = lambda j: col0 // LANES + j * n_pairs
    tqs = min(tq, 2 * LANES)
    s_buf = pltpu.VMEM((2 * tq // tqs, tq, tqs), F32)
    zspec = lambda j: pl.BlockSpec((1, seq, LANES), lambda b, p, j=j: (b, 0, blk(j) + p))
    return pl.pallas_call(
        functools.partial(_fox_kernel, tq=tq, tqs=tqs, t_chunk=min(seq, 512)),
        grid=(bsz, n_pairs),
        in_specs=[zspec(0), zspec(1), zspec(2), zspec(3),
                  pl.BlockSpec((1, seq, LANES), lambda b, p: (b, 0, 0))],
        out_specs=pl.BlockSpec((1, seq, LANES), lambda b, p: (b, 0, p)),
        out_shape=jax.ShapeDtypeStruct((bsz, seq, width), BF16),
        scratch_shapes=[pltpu.VMEM((2, VT_ROWS, seq), BF16), pltpu.VMEM((2, LANES, seq), BF16),
                        pltpu.VMEM((2, 1, tq), F32), pltpu.VMEM((2, VT_ROWS, tq), F32), s_buf, s_buf],
        compiler_params=pltpu.CompilerParams(dimension_semantics=("arbitrary",) * 2,
                                             vmem_limit_bytes=VMEM_LIMIT),
        name="fox_attention",
    )(z3, z3, z3, z3, c_sp)


def _tail_kernel(yr_ref, yf_ref, gr_ref, gf_ref, x_ref, p_ref, ng_ref, fg_ref, wr_ref, wf_ref, wo_ref, wp_ref, wg_ref,
                 o_ref):
    u_rw = _dot(yr_ref[...], wr_ref[...])
    u_fox = _dot(yf_ref[...], wf_ref[...])
    merged = _sigmoid(gr_ref[...]) * u_rw + _sigmoid(gf_ref[...]) * u_fox
    x1 = x_ref[...] + _dot(merged.astype(BF16), wo_ref[...])
    hn = x1 * lax.rsqrt(jnp.mean(x1 * x1, axis=-1, keepdims=True) + NORM_EPS) * ng_ref[...]
    gate = _sigmoid(_dot(hn.astype(BF16), wg_ref[...]))
    ple = _dot(p_ref[...].astype(BF16), wp_ref[...])
    x2 = x1 + ple * gate
    o_ref[...] = x2 * lax.rsqrt(jnp.mean(x2 * x2, axis=-1, keepdims=True) + NORM_EPS) * fg_ref[...]


def _tail(y_rw, y_fox, z, gate_col0, x, p, ple_norm_g, final_norm_g, w_up_rw, w_up_fox, w_out, ple_proj, ple_gate_w,
          tm):
    m, d = x.shape
    cw = y_rw.shape[1]
    pd = p.shape[1]
    gblk = gate_col0 // d
    rows = lambda n, j=0: pl.BlockSpec((tm, n), lambda i, j=j: (i, j))
    const = lambda shape: pl.BlockSpec(shape, lambda i: (0, 0), pipeline_mode=pl.Buffered(1))
    return pl.pallas_call(
        _tail_kernel,
        grid=(m // tm,),
        in_specs=[rows(cw), rows(cw), rows(d, gblk), rows(d, gblk + 1), rows(d), rows(pd),
                  const((1, d)), const((1, d)), const((cw, d)), const((cw, d)), const((d, d)), const((pd, d)),
                  const((d, d))],
        out_specs=rows(d),
        out_shape=jax.ShapeDtypeStruct((m, d), F32),
        compiler_params=pltpu.CompilerParams(dimension_semantics=("arbitrary",),
                                             vmem_limit_bytes=VMEM_LIMIT),
        name="merge_ple_final_norm",
    )(y_rw, y_fox, z, z, x, p, ple_norm_g.reshape(1, d), final_norm_g.reshape(1, d), w_up_rw, w_up_fox, w_out,
      ple_proj, ple_gate_w)


def _regroup_kernel(main_ref, lora_ref, fl_ref, o_ref, *, n_heads):
    last = pl.num_programs(0) - 1

    @pl.when(pl.program_id(0) < last)
    def _():
        o_ref[...] = main_ref[...].T.astype(BF16)

    @pl.when(pl.program_id(0) == last)
    def _():
        row = lax.broadcasted_iota(jnp.int32, (LANES, 1), 0)
        o_ref[:, 0:LANES] = lora_ref[...].T.astype(BF16)
        o_ref[:, LANES:2 * LANES] = jnp.where(row < n_heads, fl_ref[...], 0.0).T.astype(BF16)


def _regroup_weights(w_t, cw, lw, n_heads, tr):
    n_in, d = w_t.shape
    fl0 = 8 * cw + 2 * lw
    small0 = 8 * cw + 2 * d
    n_pad = small0 + 2 * LANES
    assert tr == 2 * LANES and (4 * cw) % tr == 0 and (2 * d) % tr == 0 and n_in == fl0 + n_heads + 2 * d
    assert (2 * lw) % SUBLANES == 0 and n_heads % SUBLANES == 0

    def main_rows(i):
        r = i * tr
        src = jnp.where(r < 4 * cw, r, jnp.where(r < 8 * cw, r + 2 * lw,
                                                 jnp.where(r < small0, r + 2 * lw + n_heads, 0)))
        return pl.multiple_of(src, SUBLANES), 0

    rows = lambda n: (pl.Element(n), pl.Element(d))
    return pl.pallas_call(
        functools.partial(_regroup_kernel, n_heads=n_heads),
        grid=(n_pad // tr,),
        in_specs=[pl.BlockSpec(rows(tr), main_rows),
                  pl.BlockSpec(rows(LANES), lambda i: (4 * cw, 0)),
                  pl.BlockSpec(rows(LANES), lambda i: (fl0, 0))],
        out_specs=pl.BlockSpec((d, tr), lambda i: (0, i)),
        out_shape=jax.ShapeDtypeStruct((d, n_pad), BF16),
        compiler_params=pltpu.CompilerParams(dimension_semantics=("arbitrary",),
                                             vmem_limit_bytes=VMEM_LIMIT),
        name="regroup_w_in",
    )(w_t, w_t, w_t)


def _cast_kernel(*refs):
    n = len(refs) // 2
    for i_ref, o_ref in zip(refs[:n], refs[n:]):
        o_ref[...] = i_ref[...].astype(o_ref.dtype)


def _cast_bf16(weights, steps):
    specs = [pl.BlockSpec((w.shape[0] // steps, w.shape[1]), lambda i: (i, 0)) for w in weights]
    assert all(w.shape[0] % (steps * 2 * SUBLANES) == 0 for w in weights)
    return pl.pallas_call(
        _cast_kernel,
        grid=(steps,),
        in_specs=specs,
        out_specs=specs,
        out_shape=[jax.ShapeDtypeStruct(w.shape, BF16) for w in weights],
        compiler_params=pltpu.CompilerParams(dimension_semantics=("arbitrary",),
                                             vmem_limit_bytes=VMEM_LIMIT),
        name="cast_tail_weights",
    )(*weights)


def _layer(x2d, p2d, seq, norm_g, w_in, rw_shift_mu, rw_w0, rw_w_lora_up, rw_a0, rw_a_lora_up, rw_k_k, rw_k_a,
           rw_r_k, rw_ln_g, rw_ln_b, fox_b_f, w_up_rwkv, w_up_fox, w_out, ple_proj, ple_gate_w, ple_norm_g,
           final_norm_g, *, tiles):
    m, d = x2d.shape
    bsz = m // seq
    cw = w_up_rwkv.shape[0]
    n_heads = fox_b_f.shape[0]
    lw = rw_w_lora_up.shape[0]
    assert 2 * lw == LANES and cw % LANES == 0 and C_TERMS * (cw // LANES) <= HEAD_DIM

    fox0, gate0, small0 = 4 * cw, 8 * cw, 8 * cw + 2 * d
    n_pad = small0 + 2 * LANES
    w_cat = _regroup_weights(w_in.T, cw, lw, n_heads, tiles["regroup_tr"])
    mu_cat = jnp.concatenate([
        rw_shift_mu[0:4 * cw], jnp.zeros((small0 - 4 * cw,), F32), rw_shift_mu[4 * cw:],
        jnp.zeros((LANES,), F32)]).reshape(1, n_pad)

    z = _inproj(x2d, norm_g, w_cat, mu_cat, seq, tiles["in_tm"], tiles["in_tn"])
    z3 = z.reshape(bsz, seq, n_pad)
    lora_blk = small0 // (2 * LANES)

    row = lambda a: a.reshape(1, cw)
    wup = jnp.pad(rw_w_lora_up, ((0, lw), (0, 0))).astype(BF16)
    aup = jnp.pad(rw_a_lora_up, ((lw, 0), (0, 0))).astype(BF16)
    y_rw = _rwkv(z3, lora_blk, wup, aup, row(rw_w0), row(rw_a0), row(rw_k_k), row(rw_k_a), row(rw_r_k),
                 row(rw_ln_g), row(rw_ln_b), cw)

    bf_pad = jnp.pad(fox_b_f, (0, LANES - n_heads)).reshape(1, LANES)
    c_sp = _fcum(z3, lora_blk, bf_pad, n_heads, tiles["fc_tc"])
    y_fox = _fox(z3, c_sp, fox0, cw, tiles["fox_tq"])

    tail_w = _cast_bf16([w_up_rwkv, w_up_fox, w_out, ple_proj, ple_gate_w], tiles["cast_steps"])
    return _tail(y_rw.reshape(m, cw), y_fox.reshape(m, cw), z, gate0, x2d, p2d, ple_norm_g, final_norm_g,
                 *tail_w, tiles["tail_tm"])


_TILES = dict(regroup_tr=256, in_tm=1024, in_tn=1792, fc_tc=256, fox_tq=1024, tail_tm=256, cast_steps=8)


def kernel(x, p, norm_g, w_in, rw_shift_mu, rw_w0, rw_w_lora_up, rw_a0, rw_a_lora_up, rw_k_k, rw_k_a, rw_r_k,
           rw_ln_g, rw_ln_b, fox_b_f, w_up_rwkv, w_up_fox, w_out, ple_proj, ple_gate_w, ple_norm_g,
           final_norm_g):
    bsz, seq, d = x.shape
    depth = p.shape[0]
    assert depth == 1, "the final norm is fused into the (single) layer"
    out = _layer(x.reshape(bsz * seq, d), p[0].reshape(bsz * seq, -1), seq, norm_g[0], w_in[0], rw_shift_mu[0],
                 rw_w0[0], rw_w_lora_up[0], rw_a0[0], rw_a_lora_up[0], rw_k_k[0], rw_k_a[0],
                 rw_r_k[0].reshape(-1), rw_ln_g[0], rw_ln_b[0], fox_b_f[0], w_up_rwkv[0], w_up_fox[0], w_out[0],
                 ple_proj[0], ple_gate_w[0], ple_norm_g[0], final_norm_g, tiles=_TILES)
    return out.reshape(bsz, seq, d)
```
